```python
import math
import jax, jax.numpy as jnp
from jax import lax
import numpy as np

D_MODEL = 2048
BATCH = 2
SEQ = 4096
DEPTH = 4

HEAD_DIM = 64
D_MIX = D_MODEL
N_MIX_HEADS = D_MIX // HEAD_DIM
N_CONV_GROUPS = N_MIX_HEADS // 4
N_FOX_HEADS = (N_MIX_HEADS - N_CONV_GROUPS) // 2
N_NSA_HEADS = N_MIX_HEADS - N_CONV_GROUPS - N_FOX_HEADS
NSA_GROUP = 4
N_NSA_KV = N_NSA_HEADS // NSA_GROUP
D_FOX = N_FOX_HEADS * HEAD_DIM
D_NSA = N_NSA_HEADS * HEAD_DIM
D_NSA_KV = N_NSA_KV * HEAD_DIM
D_CONV = N_CONV_GROUPS * HEAD_DIM
IN_SPLITS = (D_FOX, D_FOX, D_FOX, N_FOX_HEADS,
             D_NSA, D_NSA_KV, D_NSA_KV, D_NSA_KV, D_NSA_KV, D_NSA_KV, D_NSA_KV, 3 * N_NSA_HEADS,
             D_CONV, D_CONV)
D_IN = sum(IN_SPLITS)
D_FF = 4 * D_MODEL
Q_BLOCK = 128
CMP_LEN = 32
CMP_STRIDE = 16
SEL_LEN = 64
N_SEL = 16
WINDOW = 512
CONV_WIDTH = 31
N_BUCKETS = 32
MAX_DISTANCE = 128
EPS = 1e-6
NEG = -1e30
BIG = 1e9

kernel_name = "hybrid_fox_nsa_conformer_adaln_trunk"


def rms_norm(x, g):
    xf = x.astype(jnp.float32)
    y = xf * lax.rsqrt(jnp.mean(xf * xf, axis=-1, keepdims=True) + EPS)
    return (y * g.astype(jnp.float32)).astype(x.dtype)


def layer_norm(x, g, b):
    xf = x.astype(jnp.float32)
    mu = jnp.mean(xf, axis=-1, keepdims=True)
    var = jnp.mean(jnp.square(xf - mu), axis=-1, keepdims=True)
    y = (xf - mu) * lax.rsqrt(var + EPS) * g.astype(jnp.float32) + b.astype(jnp.float32)
    return y.astype(x.dtype)


def masked_softmax(s, mask):
    p = jax.nn.softmax(jnp.where(mask, s, NEG), axis=-1)
    return jnp.where(mask, p, 0.0)


def t5_bucket(dist):
    n = jnp.maximum(dist, 0)
    max_exact = N_BUCKETS // 2
    nf = jnp.maximum(n, 1).astype(jnp.float32)
    large = max_exact + (jnp.log(nf / max_exact) / math.log(MAX_DISTANCE / max_exact)
                         * (N_BUCKETS - max_exact)).astype(jnp.int32)
    large = jnp.minimum(large, N_BUCKETS - 1)
    return jnp.where(n < max_exact, n, large)


def qk_bias(table, dist):
    q_len, k_len = dist.shape
    b = table[t5_bucket(dist)]
    return b.reshape(q_len, k_len, N_NSA_KV, NSA_GROUP).transpose(2, 3, 0, 1)


def fox_attention(q, k, v, log_f):
    B, S, H, Dh = q.shape
    scale = Dh ** -0.5
    Fh = jnp.cumsum(log_f, axis=1).transpose(0, 2, 1)
    kpos = jnp.arange(S)

    def block(i):
        t0 = i * Q_BLOCK
        qpos = t0 + jnp.arange(Q_BLOCK)
        qb = lax.dynamic_slice_in_dim(q, t0, Q_BLOCK, axis=1)
        Fq = lax.dynamic_slice_in_dim(Fh, t0, Q_BLOCK, axis=2)
        s = jnp.einsum('bqhd,bkhd->bhqk', qb, k, preferred_element_type=jnp.float32) * scale
        s = s + Fq[..., :, None] - Fh[..., None, :]
        p = masked_softmax(s, kpos[None, :] <= qpos[:, None])
        return jnp.einsum('bhqk,bkhd->bqhd', p.astype(v.dtype), v)

    out = lax.map(block, jnp.arange(S // Q_BLOCK))
    return out.transpose(1, 0, 2, 3, 4).reshape(B, S, H * Dh)


def nsa_attention(q, kc, vc, ks, vs, kw, vw, gates, w_cmp_k, w_cmp_v, pos_cmp, rel_bias):
    B, S, H, Dh = q.shape
    G, R = N_NSA_KV, NSA_GROUP
    scale = Dh ** -0.5
    n_cmp = (S - CMP_LEN) // CMP_STRIDE + 1
    n_sel = S // SEL_LEN
    n_top = min(N_SEL, n_sel)
    T = n_top * SEL_LEN
    cidx = jnp.arange(n_cmp)[:, None] * CMP_STRIDE + jnp.arange(CMP_LEN)[None, :]
    cmp_end = cidx[:, -1]
    pos = pos_cmp[None, None, :, None, :]
    k_cmp = jnp.einsum('bnlgd,lde->bnge', kc[:, cidx] + pos, w_cmp_k)
    v_cmp = jnp.einsum('bnlgd,lde->bnge', vc[:, cidx] + pos, w_cmp_v)
    sel_start = jnp.arange(n_sel) * SEL_LEN
    overlap = ((cidx[:, 0][:, None] < sel_start[None, :] + SEL_LEN)
               & (cmp_end[:, None] >= sel_start[None, :])).astype(jnp.float32)
    ks_t = ks.transpose(0, 2, 1, 3)
    vs_t = vs.transpose(0, 2, 1, 3)
    pad = ((0, 0), (WINDOW, 0), (0, 0), (0, 0))
    kw_pad = jnp.pad(kw, pad)
    vw_pad = jnp.pad(vw, pad)
    table = rel_bias.astype(jnp.float32)
    tbl_g = table.reshape(N_BUCKETS, G, R).transpose(1, 0, 2)
    gates = gates.reshape(B, S, H, 3)
    gather = jax.vmap(jax.vmap(lambda a, ix: a[ix]))
    lookup = jax.vmap(lambda tb, bk: tb[bk], in_axes=(0, 1), out_axes=1)
    j = jnp.arange(n_sel)

    def block(i):
        t0 = i * Q_BLOCK
        qpos = t0 + jnp.arange(Q_BLOCK)
        qg = lax.dynamic_slice_in_dim(q, t0, Q_BLOCK, axis=1).reshape(B, Q_BLOCK, G, R, Dh)
        gb = lax.dynamic_slice_in_dim(gates, t0, Q_BLOCK, axis=1).reshape(B, Q_BLOCK, G, R, 3)
        s_c = jnp.einsum('bqgrd,bngd->bgrqn', qg, k_cmp, preferred_element_type=jnp.float32) * scale
        s_c = s_c + qk_bias(table, qpos[:, None] - cmp_end[None, :])
        p_c = masked_softmax(s_c, cmp_end[None, :] <= qpos[:, None])
        o_c = jnp.einsum('bgrqn,bngd->bqgrd', p_c.astype(v_cmp.dtype), v_cmp)
        imp = jnp.einsum('bgrqn,nj->bgqj', p_c, overlap)
        cur = qpos // SEL_LEN
        forced = (j[None, :] == 0) | (j[None, :] == cur[:, None]) | (j[None, :] == cur[:, None] - 1)
        valid = sel_start[None, :] <= qpos[:, None]
        imp = jnp.where(forced, BIG, jnp.where(valid, imp, NEG))
        _, top = lax.top_k(imp, n_top)
        tok = (top[..., None] * SEL_LEN + jnp.arange(SEL_LEN)).reshape(B, G, Q_BLOCK, T)
        flat = tok.reshape(B, G, Q_BLOCK * T)
        k_sel = gather(ks_t, flat).reshape(B, G, Q_BLOCK, T, Dh)
        v_sel = gather(vs_t, flat).reshape(B, G, Q_BLOCK, T, Dh)
        b_sel = lookup(tbl_g, t5_bucket(qpos[None, None, :, None] - tok))
        s_s = jnp.einsum('bqgrd,bgqtd->bgrqt', qg, k_sel, preferred_element_type=jnp.float32) * scale
        s_s = s_s + b_sel.transpose(0, 1, 4, 2, 3)
        p_s = masked_softmax(s_s, (tok <= qpos[None, None, :, None])[:, :, None])
        o_s = jnp.einsum('bgrqt,bgqtd->bqgrd', p_s.astype(v_sel.dtype), v_sel)
        kwb = lax.dynamic_slice_in_dim(kw_pad, t0, WINDOW + Q_BLOCK, axis=1)
        vwb = lax.dynamic_slice_in_dim(vw_pad, t0, WINDOW + Q_BLOCK, axis=1)
        kpos = t0 - WINDOW + jnp.arange(WINDOW + Q_BLOCK)
        dist = qpos[:, None] - kpos[None, :]
        s_w = jnp.einsum('bqgrd,bkgd->bgrqk', qg, kwb, preferred_element_type=jnp.float32) * scale
        s_w = s_w + qk_bias(table, dist)
        p_w = masked_softmax(s_w, (dist >= 0) & (dist < WINDOW) & (kpos[None, :] >= 0))
        o_w = jnp.einsum('bgrqk,bkgd->bqgrd', p_w.astype(vwb.dtype), vwb)
        g = jax.nn.sigmoid(gb.astype(jnp.float32))
        out = g[..., 0:1] * o_c + g[..., 1:2] * o_s + g[..., 2:3] * o_w
        return out.astype(q.dtype)

    out = lax.map(block, jnp.arange(S // Q_BLOCK))
    return out.transpose(1, 0, 2, 3, 4, 5).reshape(B, S, H * Dh)


def conformer_conv(a, gate, w_dw, b_dw, ln_g, ln_b):
    u = a * jax.nn.sigmoid(gate)
    y = lax.conv_general_dilated(u, w_dw[:, None, :], window_strides=(1,),
                                 padding=[(CONV_WIDTH - 1, 0)],
                                 dimension_numbers=('NWC', 'WIO', 'NWC'),
                                 feature_group_count=u.shape[-1])
    y = layer_norm(y + b_dw, ln_g, ln_b)
    return jax.nn.silu(y)


def split_in(z):
    pts = np.cumsum(np.array(IN_SPLITS))[:-1].tolist()
    return jnp.split(z, pts, axis=-1)


def setup_inputs(seed: int = 0) -> dict:
    key = jax.random.key(seed)
    ks = jax.random.split(key, 24)
    n = jax.random.normal
    f32 = jnp.float32
    L, Dh = CMP_LEN, HEAD_DIM
    return {
        "x": n(ks[0], (BATCH, SEQ, D_MODEL), f32),
        "c": n(ks[1], (BATCH, D_MODEL), f32),
        "w_mod": n(ks[2], (DEPTH, D_MODEL, 6 * D_MODEL), f32) * (0.5 * D_MODEL ** -0.5),
        "b_mod": n(ks[3], (DEPTH, 6 * D_MODEL), f32) * 0.02,
        "norm1_g": 1.0 + 0.05 * n(ks[4], (DEPTH, D_MODEL), f32),
        "w_in": n(ks[5], (DEPTH, D_MODEL, D_IN), f32) * D_MODEL ** -0.5,
        "b_f": jax.random.uniform(ks[6], (DEPTH, N_FOX_HEADS), f32, 3.0, 6.0),
        "w_cmp_k": n(ks[7], (DEPTH, L, Dh, Dh), f32) * (L * Dh) ** -0.5,
        "w_cmp_v": n(ks[8], (DEPTH, L, Dh, Dh), f32) * (L * Dh) ** -0.5,
        "pos_cmp": n(ks[9], (DEPTH, L, Dh), f32) * 0.5,
        "conv_w": n(ks[10], (DEPTH, CONV_WIDTH, D_CONV), f32) * CONV_WIDTH ** -0.5,
        "conv_b": n(ks[11], (DEPTH, D_CONV), f32) * 0.02,
        "conv_ln_g": 1.0 + 0.05 * n(ks[12], (DEPTH, D_CONV), f32),
        "conv_ln_b": n(ks[13], (DEPTH, D_CONV), f32) * 0.02,
        "w_out": n(ks[14], (DEPTH, D_MIX, D_MODEL), f32) * D_MIX ** -0.5,
        "norm2_g": 1.0 + 0.05 * n(ks[15], (DEPTH, D_MODEL), f32),
        "w_mlp1": n(ks[16], (DEPTH, D_MODEL, D_FF), f32) * D_MODEL ** -0.5,
        "w_mlp2": n(ks[17], (DEPTH, D_FF, D_MODEL), f32) * D_FF ** -0.5,
        "rel_bias": n(ks[18], (N_BUCKETS, N_NSA_HEADS), f32) * 0.5,
        "final_g": 1.0 + 0.05 * n(ks[19], (D_MODEL,), f32),
    }


def reference(x, c, w_mod, b_mod, norm1_g, w_in, b_f, w_cmp_k, w_cmp_v, pos_cmp,
              conv_w, conv_b, conv_ln_g, conv_ln_b, w_out, norm2_g, w_mlp1, w_mlp2,
              rel_bias, final_g):
    B, S, _ = x.shape
    c_act = jax.nn.silu(c)
    for l in range(DEPTH):
        mod = (c_act @ w_mod[l] + b_mod[l])[:, None, :]
        sh1, sc1, g1, sh2, sc2, g2 = jnp.split(mod, 6, axis=-1)
        h = rms_norm(x, norm1_g[l]) * (1.0 + sc1) + sh1
        (fq, fk, fv, ff, nq, kc, vc, ksl, vsl, kwn, vwn, ng, ca, cg) = split_in(h @ w_in[l])
        hd = lambda t: t.reshape(B, S, -1, HEAD_DIM)
        log_f = jax.nn.log_sigmoid(ff.astype(jnp.float32) + b_f[l].astype(jnp.float32))
        o_fox = fox_attention(hd(fq), hd(fk), hd(fv), log_f)
        o_nsa = nsa_attention(hd(nq), hd(kc), hd(vc), hd(ksl), hd(vsl), hd(kwn), hd(vwn), ng,
                              w_cmp_k[l], w_cmp_v[l], pos_cmp[l], rel_bias)
        o_conv = conformer_conv(ca, cg, conv_w[l], conv_b[l], conv_ln_g[l], conv_ln_b[l])
        mixed = jnp.concatenate([o_fox, o_nsa, o_conv], axis=-1)
        x = x + g1 * (mixed @ w_out[l])
        h = rms_norm(x, norm2_g[l]) * (1.0 + sc2) + sh2
        x = x + g2 * (jnp.square(jax.nn.relu(h @ w_mlp1[l])) @ w_mlp2[l])
    return rms_norm(x, final_g)
```

```python
import functools
import math

import numpy as np
import jax
import jax.numpy as jnp
from jax import lax
from jax.experimental import pallas as pl
from jax.experimental.pallas import tpu as pltpu

F32 = jnp.float32
BF16 = jnp.bfloat16
HIGHEST = lax.Precision.HIGHEST

D_MODEL = 2048
HEAD_DIM = 64
N_FOX = 12
N_NSA = 12
N_KV = 3
GROUP = 4
D_CONV = 512
D_FF = 4 * D_MODEL
CMP_LEN = 32
CMP_STRIDE = 16
SEL_LEN = 64
N_SEL = 16
WINDOW = 512
CONV_WIDTH = 31
N_BUCKETS = 32
MAX_DISTANCE = 128
EPS = 1e-6
NEG = -1e30
BIG = 1e9

LANES = 128
HALF = LANES // 2

ZB_FQ, ZB_FK, ZB_FV, ZB_NQ = 0, 6, 12, 18
ZB_CA, ZB_CG = 24, 28
ZB_KVC, ZB_KVS, ZB_KVW = 32, 35, 38
NZ_BLOCKS = 42
NZ = NZ_BLOCKS * LANES
SMALL_FF, SMALL_NG = 0, 12

_SRC = dict(fq=0, fk=768, fv=1536, ff=2304, nq=2316, kc=3084, vc=3276, ks=3468, vs=3660,
            kw=3852, vw=4044, ng=4236, ca=4272, cg=4784)

TM_INPROJ = 1024
TN_INPROJ = 768
TM_OUT = 512
TM_MLP = 512
TF_MLP = 512
ROW_CHUNK = 128
T_FOX = 512
T_NSA = 256
TQ_CMP = 128
TS_CONV = 512
TS_CUM = 512
HALO = 32
VMEM_LIMIT = 56 * 1024 * 1024


def _cparams(sem):
    return pltpu.CompilerParams(dimension_semantics=sem, vmem_limit_bytes=VMEM_LIMIT)


def _dot(a, b):
    return jnp.dot(a, b, preferred_element_type=F32)


def _dot_nt(a, b):
    return lax.dot_general(a, b, (((1,), (1,)), ((), ())), preferred_element_type=F32)


def _bucket_thresholds():
    d = np.arange(MAX_DISTANCE, dtype=np.int32)
    max_exact = N_BUCKETS // 2
    nf = np.maximum(d, 1).astype(np.float32)
    large = max_exact + (np.log(nf / np.float32(max_exact)) / np.float32(math.log(MAX_DISTANCE / max_exact))
                         * np.float32(N_BUCKETS - max_exact)).astype(np.int32)
    large = np.minimum(large, N_BUCKETS - 1)
    bucket = np.where(d < max_exact, d, large)
    assert bucket[-1] == N_BUCKETS - 1 and np.all(np.diff(bucket) >= 0)
    return [int(np.argmax(bucket >= k)) for k in range(N_BUCKETS)]


_T5_THRESH = _bucket_thresholds()


def _fvec_kernel(tbl_ref, out_ref):
    d = lax.broadcasted_iota(jnp.int32, out_ref.shape, 1)
    out = jnp.broadcast_to(tbl_ref[:, 0:1], out_ref.shape)
    for k in range(1, N_BUCKETS):
        out = jnp.where(d >= _T5_THRESH[k], tbl_ref[:, k:k + 1], out)
    out_ref[...] = out


def _bias_by_distance(rel_bias):
    tbl = jnp.zeros((16, LANES), F32).at[:N_NSA, :N_BUCKETS].set(rel_bias.astype(F32).T)
    return pl.pallas_call(
        _fvec_kernel, out_shape=jax.ShapeDtypeStruct((16, LANES), F32), name="t5_bias_by_distance",
    )(tbl)


def _mod_kernel(c_ref, w_ref, b_ref, o_ref):
    c = c_ref[...]
    ca = c * jax.nn.sigmoid(c)
    o_ref[...] = jnp.dot(ca, w_ref[...], precision=HIGHEST, preferred_element_type=F32) + b_ref[...]


def _modulation(c, w_mod, b_mod):
    depth, d, n = w_mod.shape
    b = c.shape[0]
    tn = 1024
    c8 = jnp.zeros((8, d), F32).at[:b].set(c)
    out = pl.pallas_call(
        _mod_kernel,
        grid=(depth, n // tn),
        in_specs=[pl.BlockSpec((8, d), lambda l, j: (0, 0)),
                  pl.BlockSpec((None, d, tn), lambda l, j: (l, 0, j)),
                  pl.BlockSpec((None, 1, tn), lambda l, j: (l, 0, j))],
        out_specs=pl.BlockSpec((None, 8, tn), lambda l, j: (l, 0, j)),
        out_shape=jax.ShapeDtypeStruct((depth, 8, n), F32),
        compiler_params=_cparams(("parallel", "parallel")),
        name="adaln_modulation",
    )(c8, w_mod, b_mod.reshape(depth, 1, n))
    return out[:, :b].reshape(depth, b, 6, d)


def _norm_mod(x, g, sc, sh):
    y = x * lax.rsqrt(jnp.mean(x * x, axis=-1, keepdims=True) + EPS) * g
    return y * (1.0 + sc) + sh


def _inproj_kernel(x_ref, mod_ref, g_ref, w_ref, ws_ref, z_ref, sm_ref, h_ref, *, tm):
    @pl.when(pl.program_id(1) == 0)
    def _():
        def chunk(c, carry):
            r = pl.multiple_of(c * ROW_CHUNK, ROW_CHUNK)
            h = _norm_mod(x_ref[pl.ds(r, ROW_CHUNK), :], g_ref[...], mod_ref[1:2, :], mod_ref[0:1, :])
            hb = h.astype(BF16)
            h_ref[pl.ds(r, ROW_CHUNK), :] = hb
            sm_ref[pl.ds(r, ROW_CHUNK), :] = _dot(hb, ws_ref[...])
            return carry
        lax.fori_loop(0, tm // ROW_CHUNK, chunk, 0)

    z_ref[...] = _dot(h_ref[...], w_ref[...]).astype(BF16)


def _inproj(x2, mod_l, g, wz, ws, seq):
    m, d = x2.shape
    tm, tn = TM_INPROJ, TN_INPROJ
    assert seq % tm == 0 and NZ % tn == 0
    return pl.pallas_call(
        functools.partial(_inproj_kernel, tm=tm),
        grid=(m // tm, NZ // tn),
        in_specs=[pl.BlockSpec((tm, d), lambda i, j: (i, 0)),
                  pl.BlockSpec((None, 6, d), lambda i, j: (i * tm // seq, 0, 0)),
                  pl.BlockSpec((1, d), lambda i, j: (0, 0)),
                  pl.BlockSpec((d, tn), lambda i, j: (0, j)),
                  pl.BlockSpec((d, LANES), lambda i, j: (0, 0))],
        out_specs=[pl.BlockSpec((tm, tn), lambda i, j: (i, j)),
                   pl.BlockSpec((tm, LANES), lambda i, j: (i, 0))],
        out_shape=[jax.ShapeDtypeStruct((m, NZ), BF16), jax.ShapeDtypeStruct((m, LANES), F32)],
        scratch_shapes=[pltpu.VMEM((tm, d), BF16)],
        compiler_params=_cparams(("parallel", "arbitrary")),
        name="inproj",
    )(x2, mod_l, g, wz, ws)


def _foxcum_kernel(sm_ref, bf_ref, fcol_ref, frow_ref, carry_ref, *, ts):
    @pl.when(pl.program_id(1) == 0)
    def _():
        carry_ref[...] = jnp.zeros_like(carry_ref)

    x = sm_ref[...] + bf_ref[...]
    ls = jnp.minimum(x, 0.0) - jnp.log1p(jnp.exp(-jnp.abs(x)))
    r = lax.broadcasted_iota(jnp.int32, (ts, ts), 0)
    c = lax.broadcasted_iota(jnp.int32, (ts, ts), 1)
    tri = jnp.where(r >= c, 1.0, 0.0).astype(F32)
    cum = jnp.dot(tri, ls, precision=HIGHEST, preferred_element_type=F32) + carry_ref[0:1, :]
    carry_ref[0:1, :] = cum[ts - 1:ts, :]
    cum_t = cum.T
    for hp in range(N_FOX // 2):
        fcol_ref[hp] = cum if hp == 0 else pltpu.roll(cum, LANES - 2 * hp, 1)
        frow_ref[hp, 0:2, :] = cum_t[2 * hp:2 * hp + 2, :]
        frow_ref[hp, 2:8, :] = jnp.zeros((6, ts), F32)


def _fox_cumsum(small, bf_row, batch, seq):
    ts = TS_CUM
    ns = seq // ts
    npair = N_FOX // 2
    return pl.pallas_call(
        functools.partial(_foxcum_kernel, ts=ts),
        grid=(batch, ns),
        in_specs=[pl.BlockSpec((ts, LANES), lambda b, s: (b * ns + s, 0)),
                  pl.BlockSpec((1, LANES), lambda b, s: (0, 0))],
        out_specs=[pl.BlockSpec((None, npair, ts, LANES), lambda b, s: (b, 0, s, 0)),
                   pl.BlockSpec((None, npair, 8, ts), lambda b, s: (b, 0, 0, s))],
        out_shape=[jax.ShapeDtypeStruct((batch, npair, seq, LANES), F32),
                   jax.ShapeDtypeStruct((batch, npair, 8, seq), F32)],
        scratch_shapes=[pltpu.VMEM((8, LANES), F32)],
        compiler_params=_cparams(("parallel", "arbitrary")),
        name="fox_cumsum",
    )(small, bf_row)


def _fox_kernel(q_ref, k_ref, v_ref, fc_ref, fr_ref, o_ref, m_ref, l_ref, acc_ref, *, t):
    qi = pl.program_id(2)
    ki = pl.program_id(3)

    @pl.when(ki == 0)
    def _():
        m_ref[...] = jnp.full(m_ref.shape, NEG, F32)
        l_ref[...] = jnp.zeros_like(l_ref)
        acc_ref[...] = jnp.zeros_like(acc_ref)

    lane = lax.broadcasted_iota(jnp.int32, (1, LANES), 1)

    def step(diagonal):
        q2 = q_ref[...]
        k2 = k_ref[...]
        v2 = v_ref[...]
        if diagonal:
            causal = (lax.broadcasted_iota(jnp.int32, (t, t), 1) <= lax.broadcasted_iota(jnp.int32, (t, t), 0))
        for h in range(2):
            keep = (lane < HALF) if h == 0 else (lane >= HALF)
            qh = jnp.where(keep, q2, jnp.zeros_like(q2))
            s = _dot_nt(qh, k2) + fc_ref[:, h:h + 1] - fr_ref[h:h + 1, :]
            if diagonal:
                s = jnp.where(causal, s, NEG)
            m_old = m_ref[h]
            m_new = jnp.maximum(m_old, jnp.max(s, axis=1, keepdims=True))
            p = jnp.exp(s - m_new)
            alpha = jnp.exp(m_old - m_new)
            l_ref[h] = alpha * l_ref[h] + jnp.sum(p, axis=1, keepdims=True)
            acc_ref[h] = alpha * acc_ref[h] + _dot(p.astype(BF16), v2)
            m_ref[h] = m_new

    @pl.when(ki < qi)
    def _():
        step(False)

    @pl.when(ki == qi)
    def _():
        step(True)
        o0 = acc_ref[0] / l_ref[0]
        o1 = acc_ref[1] / l_ref[1]
        o_ref[...] = jnp.where(lane < HALF, o0, o1).astype(BF16)


def _fox_attention(z, fcol, frow, batch, seq):
    t = T_FOX
    nt = seq // t
    npair = N_FOX // 2
    m = batch * seq
    kv_row = lambda b, hp, qi, ki: b * nt + jnp.minimum(ki, qi)
    return pl.pallas_call(
        functools.partial(_fox_kernel, t=t),
        grid=(batch, npair, nt, nt),
        in_specs=[pl.BlockSpec((t, LANES), lambda b, hp, qi, ki: (b * nt + qi, ZB_FQ + hp)),
                  pl.BlockSpec((t, LANES), lambda b, hp, qi, ki: (kv_row(b, hp, qi, ki), ZB_FK + hp)),
                  pl.BlockSpec((t, LANES), lambda b, hp, qi, ki: (kv_row(b, hp, qi, ki), ZB_FV + hp)),
                  pl.BlockSpec((None, None, t, LANES), lambda b, hp, qi, ki: (b, hp, qi, 0)),
                  pl.BlockSpec((None, None, 8, t), lambda b, hp, qi, ki: (b, hp, 0, jnp.minimum(ki, qi)))],
        out_specs=pl.BlockSpec((t, LANES), lambda b, hp, qi, ki: (b * nt + qi, hp)),
        out_shape=jax.ShapeDtypeStruct((m, N_FOX * HEAD_DIM), BF16),
        scratch_shapes=[pltpu.VMEM((2, t, 1), F32), pltpu.VMEM((2, t, 1), F32),
                        pltpu.VMEM((2, t, LANES), F32)],
        compiler_params=_cparams(("parallel", "parallel", "parallel", "arbitrary")),
        name="fox_attention",
    )(z, z, z, fcol, frow)


def _compress_kernel(r_ref, wlo_ref, whi_ref, plo_ref, phi_ref, o_ref, *, nc):
    r = r_ref[...]
    p_lo = _dot(r, wlo_ref[...])
    p_hi = _dot(r, whi_ref[...])
    bias = _dot(plo_ref[...], wlo_ref[...]) + _dot(phi_ref[...], whi_ref[...])
    out = p_lo + pltpu.roll(p_hi, nc - 1, 0) + bias[0:1, :]
    o_ref[...] = out.astype(BF16)


def _compress(r, wlo, whi, plo, phi):
    batch, nc, k = r.shape
    n = wlo.shape[1]
    return pl.pallas_call(
        functools.partial(_compress_kernel, nc=nc),
        grid=(batch,),
        in_specs=[pl.BlockSpec((None, nc, k), lambda b: (b, 0, 0)),
                  pl.BlockSpec((k, n), lambda b: (0, 0)),
                  pl.BlockSpec((k, n), lambda b: (0, 0)),
                  pl.BlockSpec((8, k), lambda b: (0, 0)),
                  pl.BlockSpec((8, k), lambda b: (0, 0))],
        out_specs=pl.BlockSpec((None, nc, n), lambda b: (b, 0, 0)),
        out_shape=jax.ShapeDtypeStruct((batch, nc, n), BF16),
        compiler_params=_cparams(("parallel",)),
        name="nsa_compress",
    )(r, wlo, whi, plo, phi)


def _head_queries(qf, lane):
    zero = jnp.zeros_like(qf)
    qa = jnp.where(lane < HALF, qf, zero).astype(BF16)
    qb = jnp.where(lane < HALF, pltpu.roll(qf, HALF, 1), zero).astype(BF16)
    return qa, qb


def _gather_bias(frow, idx, tq):
    tab = jnp.broadcast_to(frow, (tq, LANES))
    parts = [jnp.take_along_axis(tab, idx[:, c * LANES:(c + 1) * LANES], axis=1)
             for c in range(idx.shape[1] // LANES)]
    return parts[0] if len(parts) == 1 else jnp.concatenate(parts, axis=1)


def _nsacmp_kernel(q_ref, kvc_ref, sm_ref, fv_ref, ov_ref, oc_ref, sel_ref, gs_ref, imp_ref, *, tq, nc):
    qi = pl.program_id(1)
    t0 = qi * tq
    lane = lax.broadcasted_iota(jnp.int32, (1, LANES), 1)
    t_col = t0 + lax.broadcasted_iota(jnp.int32, (tq, 1), 0)
    cmp_end = lax.broadcasted_iota(jnp.int32, (1, nc), 1) * CMP_STRIDE + (CMP_LEN - 1)
    dist = t_col - cmp_end
    valid_c = dist >= 0
    idx = jnp.clip(dist, 0, MAX_DISTANCE - 1)

    gs = jax.nn.sigmoid(sm_ref[...])
    n_sel = SEL_LEN
    j_row = lax.broadcasted_iota(jnp.int32, (n_sel, tq), 0)
    t_row = t0 + lax.broadcasted_iota(jnp.int32, (n_sel, tq), 1)
    cur = jnp.right_shift(t_row, SEL_LEN.bit_length() - 1)
    forced = (j_row == 0) | (j_row == cur) | (j_row == cur - 1)
    valid_s = j_row * SEL_LEN <= t_row

    for g in range(N_KV):
        kv = kvc_ref[:, g * LANES:(g + 1) * LANES]
        gs_g = pltpu.roll(gs, LANES - (SMALL_NG + g * GROUP * 3), 1)
        gs_ref[:, g * LANES:(g + 1) * LANES] = gs_g
        psum = jnp.zeros((tq, nc), F32)
        outs = []
        for pair in range(2):
            c0 = (g * 2 + pair) * LANES
            qpair = _head_queries(q_ref[:, c0:c0 + LANES].astype(F32), lane)
            for sub in range(2):
                r = pair * 2 + sub
                h = g * GROUP + r
                s = _dot_nt(qpair[sub], kv) + _gather_bias(fv_ref[h:h + 1, :], idx, tq)
                s = jnp.where(valid_c, s, NEG)
                m = jnp.max(s, axis=1, keepdims=True)
                e = jnp.where(valid_c, jnp.exp(s - m), 0.0)
                l = jnp.sum(e, axis=1, keepdims=True)
                p = e / jnp.where(l > 0.0, l, 1.0)
                psum = psum + p
                o = _dot(p.astype(BF16), kv)
                outs.append(o * gs_g[:, r * 3:r * 3 + 1])
            oc_ref[:, c0:c0 + LANES] = jnp.where(lane < HALF, pltpu.roll(outs[-2], HALF, 1), outs[-1])

        imp = jnp.dot(psum, ov_ref[...], precision=HIGHEST, preferred_element_type=F32)
        imp_t = imp.T[0:n_sel, :]
        val = jnp.where(forced, BIG, jnp.where(valid_s, imp_t, NEG))
        imp_ref[...] = val

        def rank(j, cnt):
            row = imp_ref[pl.ds(j, 1), :]
            ahead = (row > val) | ((row == val) & (j < j_row))
            return cnt + jnp.where(ahead, 1.0, 0.0)

        cnt = lax.fori_loop(0, n_sel, rank, jnp.zeros((n_sel, tq), F32), unroll=4)
        chosen = jnp.where(cnt < float(N_SEL), 1.0, 0.0)
        chosen = jnp.concatenate([chosen, jnp.zeros((LANES - n_sel, tq), F32)], axis=0)
        sel_ref[:, g * LANES:(g + 1) * LANES] = chosen.T.astype(BF16)


def _nsa_compressed(z, kvcmp, small, fvec, overlap, batch, seq):
    tq = TQ_CMP
    nq = seq // tq
    nc = kvcmp.shape[1]
    m = batch * seq
    w3 = N_KV * LANES
    return pl.pallas_call(
        functools.partial(_nsacmp_kernel, tq=tq, nc=nc),
        grid=(batch, nq),
        in_specs=[pl.BlockSpec((tq, N_NSA * HEAD_DIM), lambda b, i: (b * nq + i, ZB_NQ * LANES // (N_NSA * HEAD_DIM))),
                  pl.BlockSpec((None, nc, w3), lambda b, i: (b, 0, 0)),
                  pl.BlockSpec((tq, LANES), lambda b, i: (b * nq + i, 0)),
                  pl.BlockSpec((16, LANES), lambda b, i: (0, 0)),
                  pl.BlockSpec((nc, LANES), lambda b, i: (0, 0))],
        out_specs=[pl.BlockSpec((tq, N_NSA * HEAD_DIM), lambda b, i: (b * nq + i, 0)),
                   pl.BlockSpec((tq, w3), lambda b, i: (b * nq + i, 0)),
                   pl.BlockSpec((tq, w3), lambda b, i: (b * nq + i, 0))],
        out_shape=[jax.ShapeDtypeStruct((m, N_NSA * HEAD_DIM), F32),
                   jax.ShapeDtypeStruct((m, w3), BF16),
                   jax.ShapeDtypeStruct((m, w3), F32)],
        scratch_shapes=[pltpu.VMEM((SEL_LEN, tq), F32)],
        compiler_params=_cparams(("parallel", "parallel")),
        name="nsa_compressed",
    )(z, kvcmp, small, fvec, overlap)


def _nsaflash_kernel(*refs, t, nkk, windowed, gate_col):
    if windowed:
        qa_ref, qb_ref, kv_ref, gs_ref, fv_ref, o_ref, m_ref, l_ref, acc_ref = refs
        sel_ref = e_ref = None
    else:
        qa_ref, qb_ref, kv_ref, sel_ref, e_ref, gs_ref, fv_ref, o_ref, m_ref, l_ref, acc_ref = refs
    g = pl.program_id(1)
    qi = pl.program_id(2)
    kk = pl.program_id(3)
    ki = qi - (nkk - 1) + kk if windowed else kk

    @pl.when(kk == 0)
    def _():
        m_ref[...] = jnp.full(m_ref.shape, NEG, F32)
        l_ref[...] = jnp.zeros_like(l_ref)
        acc_ref[...] = jnp.zeros_like(acc_ref)

    lane = lax.broadcasted_iota(jnp.int32, (1, LANES), 1)

    def step(near):
        kv = kv_ref[...]
        dist = ((qi - ki) * t + lax.broadcasted_iota(jnp.int32, (t, t), 0)
                - lax.broadcasted_iota(jnp.int32, (t, t), 1))
        if windowed:
            mask = (dist >= 0) & (dist < WINDOW)
        else:
            chosen = _dot(sel_ref[...], e_ref[...]) > 0.5
            mask = (chosen & (dist >= 0)) if near else chosen
        if near:
            idx = jnp.clip(dist, 0, MAX_DISTANCE - 1)
        queries = _head_queries(qa_ref[...].astype(F32), lane) + _head_queries(qb_ref[...].astype(F32), lane)
        for r in range(GROUP):
            frow = fv_ref[pl.ds(g * GROUP + r, 1), :]
            bias = _gather_bias(frow, idx, t) if near else frow[:, LANES - 1:LANES]
            s = jnp.where(mask, _dot_nt(queries[r], kv) + bias, NEG)
            m_old = m_ref[r]
            m_new = jnp.maximum(m_old, jnp.max(s, axis=1, keepdims=True))
            p = jnp.where(mask, jnp.exp(s - m_new), 0.0)
            alpha = jnp.exp(m_old - m_new)
            l_ref[r] = alpha * l_ref[r] + jnp.sum(p, axis=1, keepdims=True)
            acc_ref[r] = alpha * acc_ref[r] + _dot(p.astype(BF16), kv)
            m_ref[r] = m_new

    live = (ki >= 0) if windowed else (ki <= qi)

    @pl.when(live & (ki >= qi - 1))
    def _():
        step(True)

    @pl.when(live & (ki < qi - 1))
    def _():
        step(False)

    @pl.when(ki == qi)
    def _():
        gs = gs_ref[...]
        o = [acc_ref[r] * (gs[:, r * 3 + gate_col:r * 3 + gate_col + 1] / l_ref[r]) for r in range(GROUP)]
        o_ref[:, 0:LANES] = jnp.where(lane < HALF, pltpu.roll(o[0], HALF, 1), o[1])
        o_ref[:, LANES:2 * LANES] = jnp.where(lane < HALF, pltpu.roll(o[2], HALF, 1), o[3])


def _nsa_flash(z, sel, emat, gsig, fvec, batch, seq, windowed):
    t = T_NSA
    nt = seq // t
    m = batch * seq
    nkk = (WINDOW // t + 1) if windowed else nt
    zb_kv = ZB_KVW if windowed else ZB_KVS
    if windowed:
        kidx = lambda qi, kk: jnp.maximum(qi - (nkk - 1) + kk, 0)
    else:
        kidx = lambda qi, kk: jnp.minimum(kk, qi)
    in_specs = [pl.BlockSpec((t, LANES), lambda b, g, qi, kk: (b * nt + qi, ZB_NQ + 2 * g)),
                pl.BlockSpec((t, LANES), lambda b, g, qi, kk: (b * nt + qi, ZB_NQ + 2 * g + 1)),
                pl.BlockSpec((t, LANES), lambda b, g, qi, kk: (b * nt + kidx(qi, kk), zb_kv + g))]
    args = [z, z, z]
    if not windowed:
        in_specs += [pl.BlockSpec((t, LANES), lambda b, g, qi, kk: (b * nt + qi, g)),
                     pl.BlockSpec((LANES, t), lambda b, g, qi, kk: (0, kidx(qi, kk)))]
        args += [sel, emat]
    in_specs += [pl.BlockSpec((t, LANES), lambda b, g, qi, kk: (b * nt + qi, g)),
                 pl.BlockSpec((16, LANES), lambda b, g, qi, kk: (0, 0))]
    args += [gsig, fvec]
    return pl.pallas_call(
        functools.partial(_nsaflash_kernel, t=t, nkk=nkk, windowed=windowed, gate_col=2 if windowed else 1),
        grid=(batch, N_KV, nt, nkk),
        in_specs=in_specs,
        out_specs=pl.BlockSpec((t, 2 * LANES), lambda b, g, qi, kk: (b * nt + qi, g)),
        out_shape=jax.ShapeDtypeStruct((m, N_NSA * HEAD_DIM), F32),
        scratch_shapes=[pltpu.VMEM((GROUP, t, 1), F32), pltpu.VMEM((GROUP, t, 1), F32),
                        pltpu.VMEM((GROUP, t, LANES), F32)],
        compiler_params=_cparams(("parallel", "parallel", "parallel", "arbitrary")),
        name="nsa_window" if windowed else "nsa_selected",
    )(*args)


def _conv_kernel(a_ref, g_ref, ah_ref, gh_ref, w_ref, b_ref, lg_ref, lb_ref, o_ref, u_ref, *, ts):
    first = pl.program_id(1) == 0
    halo = ah_ref[...].astype(F32) * jax.nn.sigmoid(gh_ref[...].astype(F32))
    u_ref[0:HALO, :] = jnp.where(first, 0.0, halo)
    u_ref[HALO:HALO + ts, :] = a_ref[...].astype(F32) * jax.nn.sigmoid(g_ref[...].astype(F32))
    base = HALO - (CONV_WIDTH - 1)
    y = jnp.zeros((ts, D_CONV), F32)
    for k in range(CONV_WIDTH):
        y = y + u_ref[base + k:base + k + ts, :] * w_ref[k:k + 1, :]
    y = y + b_ref[...]
    mu = jnp.mean(y, axis=-1, keepdims=True)
    yc = y - mu
    var = jnp.mean(yc * yc, axis=-1, keepdims=True)
    yn = yc * lax.rsqrt(var + EPS) * lg_ref[...] + lb_ref[...]
    o_ref[...] = (yn * jax.nn.sigmoid(yn)).astype(BF16)


def _conformer_conv(z, w, b, lg, lb, batch, seq):
    ts = TS_CONV
    ns = seq // ts
    m = batch * seq
    cb = D_CONV // LANES
    hb = ts // HALO
    halo_row = lambda bb, s: jnp.maximum((bb * ns + s) * hb - 1, 0)
    return pl.pallas_call(
        functools.partial(_conv_kernel, ts=ts),
        grid=(batch, ns),
        in_specs=[pl.BlockSpec((ts, D_CONV), lambda bb, s: (bb * ns + s, ZB_CA // cb)),
                  pl.BlockSpec((ts, D_CONV), lambda bb, s: (bb * ns + s, ZB_CG // cb)),
                  pl.BlockSpec((HALO, D_CONV), lambda bb, s: (halo_row(bb, s), ZB_CA // cb)),
                  pl.BlockSpec((HALO, D_CONV), lambda bb, s: (halo_row(bb, s), ZB_CG // cb)),
                  pl.BlockSpec((CONV_WIDTH, D_CONV), lambda bb, s: (0, 0)),
                  pl.BlockSpec((1, D_CONV), lambda bb, s: (0, 0)),
                  pl.BlockSpec((1, D_CONV), lambda bb, s: (0, 0)),
                  pl.BlockSpec((1, D_CONV), lambda bb, s: (0, 0))],
        out_specs=pl.BlockSpec((ts, D_CONV), lambda bb, s: (bb * ns + s, 0)),
        out_shape=jax.ShapeDtypeStruct((m, D_CONV), BF16),
        scratch_shapes=[pltpu.VMEM((HALO + ts, D_CONV), F32)],
        compiler_params=_cparams(("parallel", "parallel")),
        name="conformer_conv",
    )(z, z, z, z, w, b, lg, lb)


def _outproj_kernel(x_ref, mod_ref, fox_ref, oc_ref, os_ref, ow_ref, cv_ref, w_ref, o_ref):
    d_fox = N_FOX * HEAD_DIM
    d_nsa = N_NSA * HEAD_DIM
    nsa = (oc_ref[...] + os_ref[...] + ow_ref[...]).astype(BF16)
    acc = _dot(fox_ref[...], w_ref[0:d_fox, :])
    acc = acc + _dot(nsa, w_ref[d_fox:d_fox + d_nsa, :])
    acc = acc + _dot(cv_ref[...], w_ref[d_fox + d_nsa:, :])
    o_ref[...] = x_ref[...] + mod_ref[2:3, :] * acc


def _outproj(x2, mod_l, fox, oc, osel, owin, conv, w, seq):
    m, d = x2.shape
    tm = TM_OUT
    row = lambda i: (i, 0)
    return pl.pallas_call(
        _outproj_kernel,
        grid=(m // tm,),
        in_specs=[pl.BlockSpec((tm, d), row),
                  pl.BlockSpec((None, 6, d), lambda i: (i * tm // seq, 0, 0)),
                  pl.BlockSpec((tm, N_FOX * HEAD_DIM), row),
                  pl.BlockSpec((tm, N_NSA * HEAD_DIM), row),
                  pl.BlockSpec((tm, N_NSA * HEAD_DIM), row),
                  pl.BlockSpec((tm, N_NSA * HEAD_DIM), row),
                  pl.BlockSpec((tm, D_CONV), row),
                  pl.BlockSpec((d, d), lambda i: (0, 0))],
        out_specs=pl.BlockSpec((tm, d), row),
        out_shape=jax.ShapeDtypeStruct((m, d), F32),
        compiler_params=_cparams(("parallel",)),
        name="outproj",
    )(x2, mod_l, fox, oc, osel, owin, conv, w)


def _mlp_kernel(x_ref, mod_ref, g_ref, w1_ref, w2_ref, fg_ref, o_ref, h_ref, *, tm, final):
    f = pl.program_id(1)

    @pl.when(f == 0)
    def _():
        def chunk(c, carry):
            r = pl.multiple_of(c * ROW_CHUNK, ROW_CHUNK)
            h = _norm_mod(x_ref[pl.ds(r, ROW_CHUNK), :], g_ref[...], mod_ref[4:5, :], mod_ref[3:4, :])
            h_ref[pl.ds(r, ROW_CHUNK), :] = h.astype(BF16)
            return carry
        lax.fori_loop(0, tm // ROW_CHUNK, chunk, 0)
        o_ref[...] = jnp.zeros_like(o_ref)

    a = jnp.maximum(_dot(h_ref[...], w1_ref[...]), 0.0)
    o_ref[...] += _dot((a * a).astype(BF16), w2_ref[...])

    @pl.when(f == pl.num_programs(1) - 1)
    def _():
        def chunk(c, carry):
            r = pl.multiple_of(c * ROW_CHUNK, ROW_CHUNK)
            y = x_ref[pl.ds(r, ROW_CHUNK), :] + mod_ref[5:6, :] * o_ref[pl.ds(r, ROW_CHUNK), :]
            if final:
                y = y * lax.rsqrt(jnp.mean(y * y, axis=-1, keepdims=True) + EPS) * fg_ref[...]
            o_ref[pl.ds(r, ROW_CHUNK), :] = y
            return carry
        lax.fori_loop(0, tm // ROW_CHUNK, chunk, 0)


def _mlp(x2, mod_l, g, w1, w2, fg, seq, final):
    m, d = x2.shape
    tm, tf = TM_MLP, TF_MLP
    dff = w1.shape[1]
    return pl.pallas_call(
        functools.partial(_mlp_kernel, tm=tm, final=final),
        grid=(m // tm, dff // tf),
        in_specs=[pl.BlockSpec((tm, d), lambda i, f: (i, 0)),
                  pl.BlockSpec((None, 6, d), lambda i, f: (i * tm // seq, 0, 0)),
                  pl.BlockSpec((1, d), lambda i, f: (0, 0)),
                  pl.BlockSpec((d, tf), lambda i, f: (0, f)),
                  pl.BlockSpec((tf, d), lambda i, f: (f, 0)),
                  pl.BlockSpec((1, d), lambda i, f: (0, 0))],
        out_specs=pl.BlockSpec((tm, d), lambda i, f: (i, 0)),
        out_shape=jax.ShapeDtypeStruct((m, d), F32),
        scratch_shapes=[pltpu.VMEM((tm, d), BF16)],
        compiler_params=_cparams(("parallel", "arbitrary")),
        name="mlp_final" if final else "mlp",
    )(x2, mod_l, g, w1, w2, fg)


def _prep_inproj_weights(w_in):
    depth, d, _ = w_in.shape
    sl = lambda name, width, off=0: w_in[:, :, _SRC[name] + off:_SRC[name] + off + width]
    scale = HEAD_DIM ** -0.5
    pieces = [sl("fq", 768) * scale, sl("fk", 768), sl("fv", 768), sl("nq", 768) * scale,
              sl("ca", D_CONV), sl("cg", D_CONV)]
    for kname, vname in (("kc", "vc"), ("ks", "vs"), ("kw", "vw")):
        for g in range(N_KV):
            pieces += [sl(kname, HEAD_DIM, g * HEAD_DIM), sl(vname, HEAD_DIM, g * HEAD_DIM)]
    pieces.append(jnp.zeros((depth, d, LANES), w_in.dtype))
    wz = jnp.concatenate(pieces, axis=2).astype(BF16)
    assert wz.shape[2] == NZ
    ws = jnp.concatenate([sl("ff", N_FOX), sl("ng", 3 * N_NSA),
                          jnp.zeros((depth, d, LANES - N_FOX - 3 * N_NSA), w_in.dtype)], axis=2).astype(BF16)
    return wz, ws


def _prep_compress_weights(w_cmp_k, w_cmp_v, pos_cmp):
    depth = w_cmp_k.shape[0]
    slots = 2 * N_KV
    wkv = jnp.stack([w_cmp_k, w_cmp_v], axis=1)
    big = jnp.einsum("zklde,gh,kq->zlgkdhqe", wkv, jnp.eye(N_KV, dtype=F32), jnp.eye(2, dtype=F32))
    big = big.reshape(depth, CMP_LEN, slots * HEAD_DIM, slots * HEAD_DIM)
    half = CMP_LEN // 2
    kdim = half * slots * HEAD_DIM
    wlo = big[:, :half].reshape(depth, kdim, slots * HEAD_DIM).astype(BF16)
    whi = big[:, half:].reshape(depth, kdim, slots * HEAD_DIM).astype(BF16)
    pos = jnp.broadcast_to(pos_cmp[:, :, None, :], (depth, CMP_LEN, slots, HEAD_DIM))
    plo = jnp.zeros((depth, 8, kdim), F32).at[:, 0].set(pos[:, :half].reshape(depth, kdim)).astype(BF16)
    phi = jnp.zeros((depth, 8, kdim), F32).at[:, 0].set(pos[:, half:].reshape(depth, kdim)).astype(BF16)
    return wlo, whi, plo, phi


def _selection_constants(seq):
    nc = seq // CMP_STRIDE
    n_cmp = (seq - CMP_LEN) // CMP_STRIDE + 1
    n = np.arange(nc)[:, None]
    j = np.arange(LANES)[None, :]
    overlap = ((n * CMP_STRIDE < j * SEL_LEN + SEL_LEN) & (n * CMP_STRIDE + CMP_LEN - 1 >= j * SEL_LEN)
               & (n < n_cmp) & (j < seq // SEL_LEN))
    emat = (np.arange(seq)[None, :] // SEL_LEN) == np.arange(LANES)[:, None]
    return jnp.asarray(overlap, F32), jnp.asarray(emat, BF16)


def kernel(x, c, w_mod, b_mod, norm1_g, w_in, b_f, w_cmp_k, w_cmp_v, pos_cmp, conv_w, conv_b, conv_ln_g,
           conv_ln_b, w_out, norm2_g, w_mlp1, w_mlp2, rel_bias, final_g):
    batch, seq, d = x.shape
    depth = w_in.shape[0]
    assert d == D_MODEL and seq % TM_INPROJ == 0 and seq // SEL_LEN <= SEL_LEN

    wz, ws = _prep_inproj_weights(w_in)
    wlo, whi, plo, phi = _prep_compress_weights(w_cmp_k, w_cmp_v, pos_cmp)
    w_out_b = w_out.astype(BF16)
    w1_b = w_mlp1.astype(BF16)
    w2_b = w_mlp2.astype(BF16)
    overlap, emat = _selection_constants(seq)
    bf_rows = jnp.zeros((depth, 1, LANES), F32).at[:, 0, SMALL_FF:SMALL_FF + N_FOX].set(b_f)

    mod = _modulation(c, w_mod, b_mod)
    fvec = _bias_by_distance(rel_bias)
    fg = final_g.reshape(1, d)

    x2 = x.reshape(batch * seq, d)
    kv0 = ZB_KVC * LANES
    for l in range(depth):
        z, small = _inproj(x2, mod[l], norm1_g[l].reshape(1, d), wz[l], ws[l], seq)
        fcol, frow = _fox_cumsum(small, bf_rows[l], batch, seq)
        o_fox = _fox_attention(z, fcol, frow, batch, seq)
        r = z[:, kv0:kv0 + N_KV * LANES].reshape(batch, seq // CMP_STRIDE, CMP_STRIDE * N_KV * LANES)
        kvcmp = _compress(r, wlo[l], whi[l], plo[l], phi[l])
        o_cmp, sel, gsig = _nsa_compressed(z, kvcmp, small, fvec, overlap, batch, seq)
        o_sel = _nsa_flash(z, sel, emat, gsig, fvec, batch, seq, windowed=False)
        o_win = _nsa_flash(z, sel, emat, gsig, fvec, batch, seq, windowed=True)
        o_conv = _conformer_conv(z, conv_w[l], conv_b[l].reshape(1, -1), conv_ln_g[l].reshape(1, -1),
                                 conv_ln_b[l].reshape(1, -1), batch, seq)
        x2 = _outproj(x2, mod[l], o_fox, o_cmp, o_sel, o_win, o_conv, w_out_b[l], seq)
        x2 = _mlp(x2, mod[l], norm2_g[l].reshape(1, d), w1_b[l], w2_b[l], fg, seq, final=(l == depth - 1))
    return x2.reshape(batch, seq, d)
```

```python
import functools
import math

import numpy as np
import jax
import jax.numpy as jnp
from jax import lax
from jax.experimental import pallas as pl
from jax.experimental.pallas import tpu as pltpu

F32 = jnp.float32
BF16 = jnp.bfloat16
HIGHEST = lax.Precision.HIGHEST

D_MODEL = 2048
HEAD_DIM = 64
N_FOX = 12
N_NSA = 12
N_KV = 3
GROUP = 4
D_CONV = 512
D_FF = 4 * D_MODEL
CMP_LEN = 32
CMP_STRIDE = 16
SEL_LEN = 64
N_SEL = 16
WINDOW = 512
CONV_WIDTH = 31
N_BUCKETS = 32
MAX_DISTANCE = 128
EPS = 1e-6
NEG = -1e30
BIG = 1e9

LANES = 128
HALF = LANES // 2

ZB_FQ, ZB_FK, ZB_FV, ZB_NQ = 0, 6, 12, 18
ZB_CA, ZB_CG = 24, 28
ZB_KVC, ZB_KVS, ZB_KVW = 32, 35, 38
NZ_BLOCKS = 42
NZ = NZ_BLOCKS * LANES
SMALL_FF, SMALL_NG = 0, 12

_SRC = dict(fq=0, fk=768, fv=1536, ff=2304, nq=2316, kc=3084, vc=3276, ks=3468, vs=3660,
            kw=3852, vw=4044, ng=4236, ca=4272, cg=4784)

TM_INPROJ = 1024
TN_INPROJ = 768
TM_OUT = 512
TM_MLP = 512
TF_MLP = 512
ROW_CHUNK = 128
T_FOX = 512
T_NSA = 256
TQ_CMP = 128
TS_CONV = 512
TS_CUM = 512
HALO = 32
VMEM_LIMIT = 56 * 1024 * 1024


def _cparams(sem):
    return pltpu.CompilerParams(dimension_semantics=sem, vmem_limit_bytes=VMEM_LIMIT)


def _dot(a, b):
    return jnp.dot(a, b, preferred_element_type=F32)


def _dot_nt(a, b):
    return lax.dot_general(a, b, (((1,), (1,)), ((), ())), preferred_element_type=F32)


def _bucket_thresholds():
    d = np.arange(MAX_DISTANCE, dtype=np.int32)
    max_exact = N_BUCKETS // 2
    nf = np.maximum(d, 1).astype(np.float32)
    large = max_exact + (np.log(nf / np.float32(max_exact)) / np.float32(math.log(MAX_DISTANCE / max_exact))
                         * np.float32(N_BUCKETS - max_exact)).astype(np.int32)
    large = np.minimum(large, N_BUCKETS - 1)
    bucket = np.where(d < max_exact, d, large)
    assert bucket[-1] == N_BUCKETS - 1 and np.all(np.diff(bucket) >= 0)
    return [int(np.argmax(bucket >= k)) for k in range(N_BUCKETS)]


_T5_THRESH = _bucket_thresholds()


def _fvec_kernel(tbl_ref, out_ref):
    d = lax.broadcasted_iota(jnp.int32, out_ref.shape, 1)
    out = jnp.broadcast_to(tbl_ref[:, 0:1], out_ref.shape)
    for k in range(1, N_BUCKETS):
        out = jnp.where(d >= _T5_THRESH[k], tbl_ref[:, k:k + 1], out)
    out_ref[...] = out


def _bias_by_distance(rel_bias):
    tbl = jnp.zeros((16, LANES), F32).at[:N_NSA, :N_BUCKETS].set(rel_bias.astype(F32).T)
    return pl.pallas_call(
        _fvec_kernel, out_shape=jax.ShapeDtypeStruct((16, LANES), F32), name="t5_bias_by_distance",
    )(tbl)


def _mod_kernel(c_ref, w_ref, b_ref, o_ref):
    c = c_ref[...]
    ca = c * jax.nn.sigmoid(c)
    o_ref[...] = jnp.dot(ca, w_ref[...], precision=HIGHEST, preferred_element_type=F32) + b_ref[...]


def _modulation(c, w_mod, b_mod):
    depth, d, n = w_mod.shape
    b = c.shape[0]
    tn = 1024
    c8 = jnp.zeros((8, d), F32).at[:b].set(c)
    out = pl.pallas_call(
        _mod_kernel,
        grid=(depth, n // tn),
        in_specs=[pl.BlockSpec((8, d), lambda l, j: (0, 0)),
                  pl.BlockSpec((None, d, tn), lambda l, j: (l, 0, j)),
                  pl.BlockSpec((None, 1, tn), lambda l, j: (l, 0, j))],
        out_specs=pl.BlockSpec((None, 8, tn), lambda l, j: (l, 0, j)),
        out_shape=jax.ShapeDtypeStruct((depth, 8, n), F32),
        compiler_params=_cparams(("parallel", "parallel")),
        name="adaln_modulation",
    )(c8, w_mod, b_mod.reshape(depth, 1, n))
    return out[:, :b].reshape(depth, b, 6, d)


def _norm_mod(x, g, sc, sh):
    y = x * lax.rsqrt(jnp.mean(x * x, axis=-1, keepdims=True) + EPS) * g
    return y * (1.0 + sc) + sh


def _inproj_kernel(x_ref, mod_ref, g_ref, w_ref, ws_ref, z_ref, sm_ref, h_ref, *, tm):
    @pl.when(pl.program_id(1) == 0)
    def _():
        def chunk(c, carry):
            r = pl.multiple_of(c * ROW_CHUNK, ROW_CHUNK)
            h = _norm_mod(x_ref[pl.ds(r, ROW_CHUNK), :], g_ref[...], mod_ref[1:2, :], mod_ref[0:1, :])
            hb = h.astype(BF16)
            h_ref[pl.ds(r, ROW_CHUNK), :] = hb
            sm_ref[pl.ds(r, ROW_CHUNK), :] = _dot(hb, ws_ref[...])
            return carry
        lax.fori_loop(0, tm // ROW_CHUNK, chunk, 0)

    z_ref[...] = _dot(h_ref[...], w_ref[...]).astype(BF16)


def _inproj(x2, mod_l, g, wz, ws, seq):
    m, d = x2.shape
    tm, tn = TM_INPROJ, TN_INPROJ
    assert seq % tm == 0 and NZ % tn == 0
    return pl.pallas_call(
        functools.partial(_inproj_kernel, tm=tm),
        grid=(m // tm, NZ // tn),
        in_specs=[pl.BlockSpec((tm, d), lambda i, j: (i, 0)),
                  pl.BlockSpec((None, 6, d), lambda i, j: (i * tm // seq, 0, 0)),
                  pl.BlockSpec((1, d), lambda i, j: (0, 0)),
                  pl.BlockSpec((d, tn), lambda i, j: (0, j)),
                  pl.BlockSpec((d, LANES), lambda i, j: (0, 0))],
        out_specs=[pl.BlockSpec((tm, tn), lambda i, j: (i, j)),
                   pl.BlockSpec((tm, LANES), lambda i, j: (i, 0))],
        out_shape=[jax.ShapeDtypeStruct((m, NZ), BF16), jax.ShapeDtypeStruct((m, LANES), F32)],
        scratch_shapes=[pltpu.VMEM((tm, d), BF16)],
        compiler_params=_cparams(("parallel", "arbitrary")),
        name="inproj",
    )(x2, mod_l, g, wz, ws)


def _foxcum_kernel(sm_ref, bf_ref, fcol_ref, frow_ref, carry_ref, *, ts):
    @pl.when(pl.program_id(1) == 0)
    def _():
        carry_ref[...] = jnp.zeros_like(carry_ref)

    x = sm_ref[...] + bf_ref[...]
    ls = jnp.minimum(x, 0.0) - jnp.log1p(jnp.exp(-jnp.abs(x)))
    r = lax.broadcasted_iota(jnp.int32, (ts, ts), 0)
    c = lax.broadcasted_iota(jnp.int32, (ts, ts), 1)
    tri = jnp.where(r >= c, 1.0, 0.0).astype(F32)
    cum = jnp.dot(tri, ls, precision=HIGHEST, preferred_element_type=F32) + carry_ref[0:1, :]
    carry_ref[0:1, :] = cum[ts - 1:ts, :]
    cum_t = cum.T
    for hp in range(N_FOX // 2):
        fcol_ref[hp] = cum if hp == 0 else pltpu.roll(cum, LANES - 2 * hp, 1)
        frow_ref[hp, 0:2, :] = cum_t[2 * hp:2 * hp + 2, :]
        frow_ref[hp, 2:8, :] = jnp.zeros((6, ts), F32)


def _fox_cumsum(small, bf_row, batch, seq):
    ts = TS_CUM
    ns = seq // ts
    npair = N_FOX // 2
    return pl.pallas_call(
        functools.partial(_foxcum_kernel, ts=ts),
        grid=(batch, ns),
        in_specs=[pl.BlockSpec((ts, LANES), lambda b, s: (b * ns + s, 0)),
                  pl.BlockSpec((1, LANES), lambda b, s: (0, 0))],
        out_specs=[pl.BlockSpec((None, npair, ts, LANES), lambda b, s: (b, 0, s, 0)),
                   pl.BlockSpec((None, npair, 8, ts), lambda b, s: (b, 0, 0, s))],
        out_shape=[jax.ShapeDtypeStruct((batch, npair, seq, LANES), F32),
                   jax.ShapeDtypeStruct((batch, npair, 8, seq), F32)],
        scratch_shapes=[pltpu.VMEM((8, LANES), F32)],
        compiler_params=_cparams(("parallel", "arbitrary")),
        name="fox_cumsum",
    )(small, bf_row)


def _fox_kernel(q_ref, k_ref, v_ref, fc_ref, fr_ref, o_ref, m_ref, fq_ref, acc_ref, *, t):
    qi = pl.program_id(2)
    ki = pl.program_id(3)

    @pl.when(ki == 0)
    def _():
        m_ref[...] = jnp.full(m_ref.shape, NEG, F32)
        acc_ref[...] = jnp.zeros_like(acc_ref)
        for h in range(2):
            fq_ref[h] = jnp.broadcast_to(fc_ref[:, h:h + 1], (t, LANES))

    lane = lax.broadcasted_iota(jnp.int32, (1, LANES), 1)

    def step(diagonal):
        q2 = q_ref[...]
        k2 = k_ref[...]
        v2 = v_ref[...]
        if diagonal:
            causal = (lax.broadcasted_iota(jnp.int32, (t, t), 1) <= lax.broadcasted_iota(jnp.int32, (t, t), 0))
        for h in range(2):
            keep = (lane < HALF) if h == 0 else (lane >= HALF)
            qh = jnp.where(keep, q2, jnp.zeros_like(q2))
            vh = jnp.where(keep, v2, jnp.ones_like(v2))
            s = _dot_nt(qh, k2) - fr_ref[h:h + 1, :]
            if diagonal:
                s = jnp.where(causal, s, NEG)
            fq = fq_ref[h]
            m_old = m_ref[h]
            m_new = jnp.maximum(m_old, jnp.max(s, axis=1, keepdims=True) + fq)
            shift = jnp.tile(fq - m_new, (1, t // LANES))
            p = jnp.exp(s + shift)
            acc_ref[h] = jnp.exp(m_old - m_new) * acc_ref[h] + _dot(p.astype(BF16), vh)
            m_ref[h] = m_new

    @pl.when(ki < qi)
    def _():
        step(False)

    @pl.when(ki == qi)
    def _():
        step(True)
        a0 = acc_ref[0]
        a1 = acc_ref[1]
        o_ref[...] = jnp.where(lane < HALF, a0 / pltpu.roll(a0, HALF, 1),
                               a1 / pltpu.roll(a1, HALF, 1)).astype(BF16)


def _fox_attention(z, fcol, frow, batch, seq):
    t = T_FOX
    nt = seq // t
    npair = N_FOX // 2
    m = batch * seq
    kv_row = lambda b, hp, qi, ki: b * nt + jnp.minimum(ki, qi)
    return pl.pallas_call(
        functools.partial(_fox_kernel, t=t),
        grid=(batch, npair, nt, nt),
        in_specs=[pl.BlockSpec((t, LANES), lambda b, hp, qi, ki: (b * nt + qi, ZB_FQ + hp)),
                  pl.BlockSpec((t, LANES), lambda b, hp, qi, ki: (kv_row(b, hp, qi, ki), ZB_FK + hp)),
                  pl.BlockSpec((t, LANES), lambda b, hp, qi, ki: (kv_row(b, hp, qi, ki), ZB_FV + hp)),
                  pl.BlockSpec((None, None, t, LANES), lambda b, hp, qi, ki: (b, hp, qi, 0)),
                  pl.BlockSpec((None, None, 8, t), lambda b, hp, qi, ki: (b, hp, 0, jnp.minimum(ki, qi)))],
        out_specs=pl.BlockSpec((t, LANES), lambda b, hp, qi, ki: (b * nt + qi, hp)),
        out_shape=jax.ShapeDtypeStruct((m, N_FOX * HEAD_DIM), BF16),
        scratch_shapes=[pltpu.VMEM((2, t, LANES), F32), pltpu.VMEM((2, t, LANES), F32),
                        pltpu.VMEM((2, t, LANES), F32)],
        compiler_params=_cparams(("parallel", "parallel", "parallel", "arbitrary")),
        name="fox_attention",
    )(z, z, z, fcol, frow)


def _compress_kernel(r_ref, wlo_ref, whi_ref, plo_ref, phi_ref, o_ref, *, nc):
    r = r_ref[...]
    p_lo = _dot(r, wlo_ref[...])
    p_hi = _dot(r, whi_ref[...])
    bias = _dot(plo_ref[...], wlo_ref[...]) + _dot(phi_ref[...], whi_ref[...])
    out = p_lo + pltpu.roll(p_hi, nc - 1, 0) + bias[0:1, :]
    o_ref[...] = out.astype(BF16)


def _compress(r, wlo, whi, plo, phi):
    batch, nc, k = r.shape
    n = wlo.shape[1]
    return pl.pallas_call(
        functools.partial(_compress_kernel, nc=nc),
        grid=(batch,),
        in_specs=[pl.BlockSpec((None, nc, k), lambda b: (b, 0, 0)),
                  pl.BlockSpec((k, n), lambda b: (0, 0)),
                  pl.BlockSpec((k, n), lambda b: (0, 0)),
                  pl.BlockSpec((8, k), lambda b: (0, 0)),
                  pl.BlockSpec((8, k), lambda b: (0, 0))],
        out_specs=pl.BlockSpec((None, nc, n), lambda b: (b, 0, 0)),
        out_shape=jax.ShapeDtypeStruct((batch, nc, n), BF16),
        compiler_params=_cparams(("parallel",)),
        name="nsa_compress",
    )(r, wlo, whi, plo, phi)


def _head_queries(qf, lane):
    zero = jnp.zeros_like(qf)
    qa = jnp.where(lane < HALF, qf, zero).astype(BF16)
    qb = jnp.where(lane < HALF, pltpu.roll(qf, HALF, 1), zero).astype(BF16)
    return qa, qb


def _gather_bias(frow, idx, tq):
    tab = jnp.broadcast_to(frow, (tq, LANES))
    parts = [jnp.take_along_axis(tab, idx[:, c * LANES:(c + 1) * LANES], axis=1)
             for c in range(idx.shape[1] // LANES)]
    return parts[0] if len(parts) == 1 else jnp.concatenate(parts, axis=1)


def _nsacmp_kernel(q_ref, kvc_ref, sm_ref, fv_ref, ov_ref, oc_ref, sel_ref, gs_ref, imp_ref, *, tq, nc):
    qi = pl.program_id(1)
    t0 = qi * tq
    lane = lax.broadcasted_iota(jnp.int32, (1, LANES), 1)
    t_col = t0 + lax.broadcasted_iota(jnp.int32, (tq, 1), 0)
    cmp_end = lax.broadcasted_iota(jnp.int32, (1, nc), 1) * CMP_STRIDE + (CMP_LEN - 1)
    dist = t_col - cmp_end
    valid_c = dist >= 0
    idx = jnp.clip(dist, 0, MAX_DISTANCE - 1)

    gs = jax.nn.sigmoid(sm_ref[...])
    n_sel = SEL_LEN
    j_row = lax.broadcasted_iota(jnp.int32, (n_sel, tq), 0)
    t_row = t0 + lax.broadcasted_iota(jnp.int32, (n_sel, tq), 1)
    cur = jnp.right_shift(t_row, SEL_LEN.bit_length() - 1)
    forced = (j_row == 0) | (j_row == cur) | (j_row == cur - 1)
    valid_s = j_row * SEL_LEN <= t_row

    for g in range(N_KV):
        kv = kvc_ref[:, g * LANES:(g + 1) * LANES]
        gs_g = pltpu.roll(gs, LANES - (SMALL_NG + g * GROUP * 3), 1)
        gs_ref[:, g * LANES:(g + 1) * LANES] = gs_g
        psum = jnp.zeros((tq, nc), F32)
        outs = []
        for pair in range(2):
            c0 = (g * 2 + pair) * LANES
            qpair = _head_queries(q_ref[:, c0:c0 + LANES].astype(F32), lane)
            for sub in range(2):
                r = pair * 2 + sub
                h = g * GROUP + r
                s = _dot_nt(qpair[sub], kv) + _gather_bias(fv_ref[h:h + 1, :], idx, tq)
                s = jnp.where(valid_c, s, NEG)
                m = jnp.max(s, axis=1, keepdims=True)
                e = jnp.where(valid_c, jnp.exp(s - m), 0.0)
                l = jnp.sum(e, axis=1, keepdims=True)
                p = e / jnp.where(l > 0.0, l, 1.0)
                psum = psum + p
                o = _dot(p.astype(BF16), kv)
                outs.append(o * gs_g[:, r * 3:r * 3 + 1])
            oc_ref[:, c0:c0 + LANES] = jnp.where(lane < HALF, pltpu.roll(outs[-2], HALF, 1), outs[-1])

        imp = jnp.dot(psum, ov_ref[...], precision=HIGHEST, preferred_element_type=F32)
        imp_t = imp.T[0:n_sel, :]
        val = jnp.where(forced, BIG, jnp.where(valid_s, imp_t, NEG))
        imp_ref[...] = val

        def rank(j, cnt):
            row = imp_ref[pl.ds(j, 1), :]
            ahead = (row > val) | ((row == val) & (j < j_row))
            return cnt + jnp.where(ahead, 1.0, 0.0)

        cnt = lax.fori_loop(0, n_sel, rank, jnp.zeros((n_sel, tq), F32), unroll=4)
        unchosen = jnp.where(cnt < float(N_SEL), 0.0, NEG)
        unchosen = jnp.concatenate([unchosen, jnp.full((LANES - n_sel, tq), NEG, F32)], axis=0)
        sel_ref[:, g * LANES:(g + 1) * LANES] = unchosen.T.astype(BF16)


def _nsa_compressed(z, kvcmp, small, fvec, overlap, batch, seq):
    tq = TQ_CMP
    nq = seq // tq
    nc = kvcmp.shape[1]
    m = batch * seq
    w3 = N_KV * LANES
    return pl.pallas_call(
        functools.partial(_nsacmp_kernel, tq=tq, nc=nc),
        grid=(batch, nq),
        in_specs=[pl.BlockSpec((tq, N_NSA * HEAD_DIM), lambda b, i: (b * nq + i, ZB_NQ * LANES // (N_NSA * HEAD_DIM))),
                  pl.BlockSpec((None, nc, w3), lambda b, i: (b, 0, 0)),
                  pl.BlockSpec((tq, LANES), lambda b, i: (b * nq + i, 0)),
                  pl.BlockSpec((16, LANES), lambda b, i: (0, 0)),
                  pl.BlockSpec((nc, LANES), lambda b, i: (0, 0))],
        out_specs=[pl.BlockSpec((tq, N_NSA * HEAD_DIM), lambda b, i: (b * nq + i, 0)),
                   pl.BlockSpec((tq, w3), lambda b, i: (b * nq + i, 0)),
                   pl.BlockSpec((tq, w3), lambda b, i: (b * nq + i, 0))],
        out_shape=[jax.ShapeDtypeStruct((m, N_NSA * HEAD_DIM), F32),
                   jax.ShapeDtypeStruct((m, w3), BF16),
                   jax.ShapeDtypeStruct((m, w3), F32)],
        scratch_shapes=[pltpu.VMEM((SEL_LEN, tq), F32)],
        compiler_params=_cparams(("parallel", "parallel")),
        name="nsa_compressed",
    )(z, kvcmp, small, fvec, overlap)


def _nsaflash_kernel(*refs, t, nkk, windowed, gate_col):
    if windowed:
        qa_ref, qb_ref, kv_ref, gs_ref, tb_ref, o_ref, q4_ref, m_ref, acc_ref = refs
        sel_ref = e_ref = None
    else:
        qa_ref, qb_ref, kv_ref, sel_ref, e_ref, gs_ref, tb_ref, o_ref, q4_ref, m_ref, acc_ref = refs
    g = pl.program_id(1)
    qi = pl.program_id(2)
    kk = pl.program_id(3)
    ki = qi - (nkk - 1) + kk if windowed else kk
    lane = lax.broadcasted_iota(jnp.int32, (1, LANES), 1)

    @pl.when(kk == 0)
    def _():
        m_ref[...] = jnp.full(m_ref.shape, NEG, F32)
        acc_ref[...] = jnp.zeros_like(acc_ref)
        queries = _head_queries(qa_ref[...].astype(F32), lane) + _head_queries(qb_ref[...].astype(F32), lane)
        for r in range(GROUP):
            q4_ref[r * t:(r + 1) * t, :] = queries[r]

    def step():
        kv = kv_ref[...]
        ones_v = jnp.where(lane < HALF, jnp.ones_like(kv), kv)
        rel = qi - ki
        if not windowed:
            rel = jnp.where(rel >= 2, tb_ref.shape[0] - 1, rel)
            unchosen = _dot(sel_ref[...], e_ref[...])
        for pair in range(2):
            rows2 = slice(pair * 2 * t, (pair + 1) * 2 * t)
            s2 = _dot_nt(q4_ref[rows2, :], kv)
            probs, alphas = [], []
            for sub in range(2):
                r = pair * 2 + sub
                h = g * GROUP + r
                rows = slice(r * t, (r + 1) * t)
                m_old = m_ref[rows, :]
                s = s2[sub * t:(sub + 1) * t, :]
                if not windowed:
                    s = s + unchosen
                s = s + tb_ref[rel, h]
                m_new = jnp.maximum(m_old, jnp.max(s, axis=1, keepdims=True))
                p = jnp.exp(s - jnp.tile(m_new, (1, t // LANES)))
                m_ref[rows, :] = m_new
                alphas.append(jnp.exp(m_old - m_new))
                probs.append(p.astype(BF16))
            acc_ref[rows2, :] = (jnp.concatenate(alphas, axis=0) * acc_ref[rows2, :]
                                 + _dot(jnp.concatenate(probs, axis=0), ones_v))

    @pl.when((ki >= 0) if windowed else (ki <= qi))
    def _():
        step()

    @pl.when(ki == qi)
    def _():
        gs = gs_ref[...]
        o = []
        for r in range(GROUP):
            a = acc_ref[r * t:(r + 1) * t, :]
            o.append(a / pltpu.roll(a, HALF, 1) * gs[:, r * 3 + gate_col:r * 3 + gate_col + 1])
        o_ref[:, 0:LANES] = jnp.where(lane < HALF, pltpu.roll(o[0], HALF, 1), o[1])
        o_ref[:, LANES:2 * LANES] = jnp.where(lane < HALF, pltpu.roll(o[2], HALF, 1), o[3])


def _bias_tile_kernel(fv_ref, o_ref, *, t, nrel):
    rel = pl.program_id(0)
    h = pl.program_id(1)
    dist = rel * t + lax.broadcasted_iota(jnp.int32, (t, t), 0) - lax.broadcasted_iota(jnp.int32, (t, t), 1)
    bias = _gather_bias(fv_ref[pl.ds(h, 1), :], jnp.clip(dist, 0, MAX_DISTANCE - 1), t)
    o_ref[...] = jnp.where((dist >= 0) & ((dist < WINDOW) | (rel == nrel)), bias, NEG)


def _bias_tiles(fvec):
    t = T_NSA
    nrel = WINDOW // t + 1
    return pl.pallas_call(
        functools.partial(_bias_tile_kernel, t=t, nrel=nrel),
        grid=(nrel + 1, N_NSA),
        in_specs=[pl.BlockSpec((16, LANES), lambda r, h: (0, 0))],
        out_specs=pl.BlockSpec((None, None, t, t), lambda r, h: (r, h, 0, 0)),
        out_shape=jax.ShapeDtypeStruct((nrel + 1, N_NSA, t, t), F32),
        compiler_params=_cparams(("parallel", "parallel")),
        name="nsa_bias_tiles",
    )(fvec)


def _nsa_flash(z, sel, emat, gsig, tiles, batch, seq, windowed):
    t = T_NSA
    nt = seq // t
    m = batch * seq
    nkk = (WINDOW // t + 1) if windowed else nt
    assert t >= MAX_DISTANCE and 2 * t <= WINDOW and WINDOW % t == 0
    zb_kv = ZB_KVW if windowed else ZB_KVS
    if windowed:
        kidx = lambda qi, kk: jnp.maximum(qi - (nkk - 1) + kk, 0)
    else:
        kidx = lambda qi, kk: jnp.minimum(kk, qi)
    in_specs = [pl.BlockSpec((t, LANES), lambda b, g, qi, kk: (b * nt + qi, ZB_NQ + 2 * g)),
                pl.BlockSpec((t, LANES), lambda b, g, qi, kk: (b * nt + qi, ZB_NQ + 2 * g + 1)),
                pl.BlockSpec((t, LANES), lambda b, g, qi, kk: (b * nt + kidx(qi, kk), zb_kv + g))]
    args = [z, z, z]
    if not windowed:
        in_specs += [pl.BlockSpec((t, LANES), lambda b, g, qi, kk: (b * nt + qi, g)),
                     pl.BlockSpec((LANES, t), lambda b, g, qi, kk: (0, kidx(qi, kk)))]
        args += [sel, emat]
    in_specs += [pl.BlockSpec((t, LANES), lambda b, g, qi, kk: (b * nt + qi, g)),
                 pl.BlockSpec(tiles.shape, lambda b, g, qi, kk: (0, 0, 0, 0))]
    args += [gsig, tiles]
    return pl.pallas_call(
        functools.partial(_nsaflash_kernel, t=t, nkk=nkk, windowed=windowed, gate_col=2 if windowed else 1),
        grid=(batch, N_KV, nt, nkk),
        in_specs=in_specs,
        out_specs=pl.BlockSpec((t, 2 * LANES), lambda b, g, qi, kk: (b * nt + qi, g)),
        out_shape=jax.ShapeDtypeStruct((m, N_NSA * HEAD_DIM), F32),
        scratch_shapes=[pltpu.VMEM((GROUP * t, LANES), BF16), pltpu.VMEM((GROUP * t, LANES), F32),
                        pltpu.VMEM((GROUP * t, LANES), F32)],
        compiler_params=_cparams(("parallel", "parallel", "parallel", "arbitrary")),
        name="nsa_window" if windowed else "nsa_selected",
    )(*args)


def _conv_kernel(a_ref, g_ref, ah_ref, gh_ref, w_ref, b_ref, lg_ref, lb_ref, o_ref, u_ref, *, ts):
    first = pl.program_id(1) == 0
    halo = ah_ref[...].astype(F32) * jax.nn.sigmoid(gh_ref[...].astype(F32))
    u_ref[0:HALO, :] = jnp.where(first, 0.0, halo)
    u_ref[HALO:HALO + ts, :] = a_ref[...].astype(F32) * jax.nn.sigmoid(g_ref[...].astype(F32))
    base = HALO - (CONV_WIDTH - 1)
    y = jnp.zeros((ts, D_CONV), F32)
    for k in range(CONV_WIDTH):
        y = y + u_ref[base + k:base + k + ts, :] * w_ref[k:k + 1, :]
    y = y + b_ref[...]
    mu = jnp.mean(y, axis=-1, keepdims=True)
    yc = y - mu
    var = jnp.mean(yc * yc, axis=-1, keepdims=True)
    yn = yc * lax.rsqrt(var + EPS) * lg_ref[...] + lb_ref[...]
    o_ref[...] = (yn * jax.nn.sigmoid(yn)).astype(BF16)


def _conformer_conv(z, w, b, lg, lb, batch, seq):
    ts = TS_CONV
    ns = seq // ts
    m = batch * seq
    cb = D_CONV // LANES
    hb = ts // HALO
    halo_row = lambda bb, s: jnp.maximum((bb * ns + s) * hb - 1, 0)
    return pl.pallas_call(
        functools.partial(_conv_kernel, ts=ts),
        grid=(batch, ns),
        in_specs=[pl.BlockSpec((ts, D_CONV), lambda bb, s: (bb * ns + s, ZB_CA // cb)),
                  pl.BlockSpec((ts, D_CONV), lambda bb, s: (bb * ns + s, ZB_CG // cb)),
                  pl.BlockSpec((HALO, D_CONV), lambda bb, s: (halo_row(bb, s), ZB_CA // cb)),
                  pl.BlockSpec((HALO, D_CONV), lambda bb, s: (halo_row(bb, s), ZB_CG // cb)),
                  pl.BlockSpec((CONV_WIDTH, D_CONV), lambda bb, s: (0, 0)),
                  pl.BlockSpec((1, D_CONV), lambda bb, s: (0, 0)),
                  pl.BlockSpec((1, D_CONV), lambda bb, s: (0, 0)),
                  pl.BlockSpec((1, D_CONV), lambda bb, s: (0, 0))],
        out_specs=pl.BlockSpec((ts, D_CONV), lambda bb, s: (bb * ns + s, 0)),
        out_shape=jax.ShapeDtypeStruct((m, D_CONV), BF16),
        scratch_shapes=[pltpu.VMEM((HALO + ts, D_CONV), F32)],
        compiler_params=_cparams(("parallel", "parallel")),
        name="conformer_conv",
    )(z, z, z, z, w, b, lg, lb)


def _outproj_kernel(x_ref, mod_ref, fox_ref, oc_ref, os_ref, ow_ref, cv_ref, w_ref, o_ref):
    d_fox = N_FOX * HEAD_DIM
    d_nsa = N_NSA * HEAD_DIM
    nsa = (oc_ref[...] + os_ref[...] + ow_ref[...]).astype(BF16)
    acc = _dot(fox_ref[...], w_ref[0:d_fox, :])
    acc = acc + _dot(nsa, w_ref[d_fox:d_fox + d_nsa, :])
    acc = acc + _dot(cv_ref[...], w_ref[d_fox + d_nsa:, :])
    o_ref[...] = x_ref[...] + mod_ref[2:3, :] * acc


def _outproj(x2, mod_l, fox, oc, osel, owin, conv, w, seq):
    m, d = x2.shape
    tm = TM_OUT
    row = lambda i: (i, 0)
    return pl.pallas_call(
        _outproj_kernel,
        grid=(m // tm,),
        in_specs=[pl.BlockSpec((tm, d), row),
                  pl.BlockSpec((None, 6, d), lambda i: (i * tm // seq, 0, 0)),
                  pl.BlockSpec((tm, N_FOX * HEAD_DIM), row),
                  pl.BlockSpec((tm, N_NSA * HEAD_DIM), row),
                  pl.BlockSpec((tm, N_NSA * HEAD_DIM), row),
                  pl.BlockSpec((tm, N_NSA * HEAD_DIM), row),
                  pl.BlockSpec((tm, D_CONV), row),
                  pl.BlockSpec((d, d), lambda i: (0, 0))],
        out_specs=pl.BlockSpec((tm, d), row),
        out_shape=jax.ShapeDtypeStruct((m, d), F32),
        compiler_params=_cparams(("parallel",)),
        name="outproj",
    )(x2, mod_l, fox, oc, osel, owin, conv, w)


def _mlp_kernel(x_ref, mod_ref, g_ref, w1_ref, w2_ref, fg_ref, o_ref, h_ref, *, tm, final):
    f = pl.program_id(1)

    @pl.when(f == 0)
    def _():
        def chunk(c, carry):
            r = pl.multiple_of(c * ROW_CHUNK, ROW_CHUNK)
            h = _norm_mod(x_ref[pl.ds(r, ROW_CHUNK), :], g_ref[...], mod_ref[4:5, :], mod_ref[3:4, :])
            h_ref[pl.ds(r, ROW_CHUNK), :] = h.astype(BF16)
            return carry
        lax.fori_loop(0, tm // ROW_CHUNK, chunk, 0)
        o_ref[...] = jnp.zeros_like(o_ref)

    a = jnp.maximum(_dot(h_ref[...], w1_ref[...]), 0.0)
    o_ref[...] += _dot((a * a).astype(BF16), w2_ref[...])

    @pl.when(f == pl.num_programs(1) - 1)
    def _():
        def chunk(c, carry):
            r = pl.multiple_of(c * ROW_CHUNK, ROW_CHUNK)
            y = x_ref[pl.ds(r, ROW_CHUNK), :] + mod_ref[5:6, :] * o_ref[pl.ds(r, ROW_CHUNK), :]
            if final:
                y = y * lax.rsqrt(jnp.mean(y * y, axis=-1, keepdims=True) + EPS) * fg_ref[...]
            o_ref[pl.ds(r, ROW_CHUNK), :] = y
            return carry
        lax.fori_loop(0, tm // ROW_CHUNK, chunk, 0)


def _mlp(x2, mod_l, g, w1, w2, fg, seq, final):
    m, d = x2.shape
    tm, tf = TM_MLP, TF_MLP
    dff = w1.shape[1]
    return pl.pallas_call(
        functools.partial(_mlp_kernel, tm=tm, final=final),
        grid=(m // tm, dff // tf),
        in_specs=[pl.BlockSpec((tm, d), lambda i, f: (i, 0)),
                  pl.BlockSpec((None, 6, d), lambda i, f: (i * tm // seq, 0, 0)),
                  pl.BlockSpec((1, d), lambda i, f: (0, 0)),
                  pl.BlockSpec((d, tf), lambda i, f: (0, f)),
                  pl.BlockSpec((tf, d), lambda i, f: (f, 0)),
                  pl.BlockSpec((1, d), lambda i, f: (0, 0))],
        out_specs=pl.BlockSpec((tm, d), lambda i, f: (i, 0)),
        out_shape=jax.ShapeDtypeStruct((m, d), F32),
        scratch_shapes=[pltpu.VMEM((tm, d), BF16)],
        compiler_params=_cparams(("parallel", "arbitrary")),
        name="mlp_final" if final else "mlp",
    )(x2, mod_l, g, w1, w2, fg)


def _prep_inproj_weights(w_in):
    depth, d, _ = w_in.shape
    sl = lambda name, width, off=0: w_in[:, :, _SRC[name] + off:_SRC[name] + off + width]
    scale = HEAD_DIM ** -0.5
    pieces = [sl("fq", 768) * scale, sl("fk", 768), sl("fv", 768), sl("nq", 768) * scale,
              sl("ca", D_CONV), sl("cg", D_CONV)]
    for kname, vname in (("kc", "vc"), ("ks", "vs"), ("kw", "vw")):
        for g in range(N_KV):
            pieces += [sl(kname, HEAD_DIM, g * HEAD_DIM), sl(vname, HEAD_DIM, g * HEAD_DIM)]
    pieces.append(jnp.zeros((depth, d, LANES), w_in.dtype))
    wz = jnp.concatenate(pieces, axis=2).astype(BF16)
    assert wz.shape[2] == NZ
    ws = jnp.concatenate([sl("ff", N_FOX), sl("ng", 3 * N_NSA),
                          jnp.zeros((depth, d, LANES - N_FOX - 3 * N_NSA), w_in.dtype)], axis=2).astype(BF16)
    return wz, ws


def _prep_compress_weights(w_cmp_k, w_cmp_v, pos_cmp):
    depth = w_cmp_k.shape[0]
    slots = 2 * N_KV
    wkv = jnp.stack([w_cmp_k, w_cmp_v], axis=1)
    big = jnp.einsum("zklde,gh,kq->zlgkdhqe", wkv, jnp.eye(N_KV, dtype=F32), jnp.eye(2, dtype=F32))
    big = big.reshape(depth, CMP_LEN, slots * HEAD_DIM, slots * HEAD_DIM)
    half = CMP_LEN // 2
    kdim = half * slots * HEAD_DIM
    wlo = big[:, :half].reshape(depth, kdim, slots * HEAD_DIM).astype(BF16)
    whi = big[:, half:].reshape(depth, kdim, slots * HEAD_DIM).astype(BF16)
    pos = jnp.broadcast_to(pos_cmp[:, :, None, :], (depth, CMP_LEN, slots, HEAD_DIM))
    plo = jnp.zeros((depth, 8, kdim), F32).at[:, 0].set(pos[:, :half].reshape(depth, kdim)).astype(BF16)
    phi = jnp.zeros((depth, 8, kdim), F32).at[:, 0].set(pos[:, half:].reshape(depth, kdim)).astype(BF16)
    return wlo, whi, plo, phi


def _selection_constants(seq):
    nc = seq // CMP_STRIDE
    n_cmp = (seq - CMP_LEN) // CMP_STRIDE + 1
    n = np.arange(nc)[:, None]
    j = np.arange(LANES)[None, :]
    overlap = ((n * CMP_STRIDE < j * SEL_LEN + SEL_LEN) & (n * CMP_STRIDE + CMP_LEN - 1 >= j * SEL_LEN)
               & (n < n_cmp) & (j < seq // SEL_LEN))
    emat = (np.arange(seq)[None, :] // SEL_LEN) == np.arange(LANES)[:, None]
    return jnp.asarray(overlap, F32), jnp.asarray(emat, BF16)


def kernel(x, c, w_mod, b_mod, norm1_g, w_in, b_f, w_cmp_k, w_cmp_v, pos_cmp, conv_w, conv_b, conv_ln_g,
           conv_ln_b, w_out, norm2_g, w_mlp1, w_mlp2, rel_bias, final_g):
    batch, seq, d = x.shape
    depth = w_in.shape[0]
    assert d == D_MODEL and seq % TM_INPROJ == 0 and seq // SEL_LEN <= SEL_LEN

    wz, ws = _prep_inproj_weights(w_in)
    wlo, whi, plo, phi = _prep_compress_weights(w_cmp_k, w_cmp_v, pos_cmp)
    w_out_b = w_out.astype(BF16)
    w1_b = w_mlp1.astype(BF16)
    w2_b = w_mlp2.astype(BF16)
    overlap, emat = _selection_constants(seq)
    bf_rows = jnp.zeros((depth, 1, LANES), F32).at[:, 0, SMALL_FF:SMALL_FF + N_FOX].set(b_f)

    mod = _modulation(c, w_mod, b_mod)
    fvec = _bias_by_distance(rel_bias)
    tiles = _bias_tiles(fvec)
    fg = final_g.reshape(1, d)

    x2 = x.reshape(batch * seq, d)
    kv0 = ZB_KVC * LANES
    for l in range(depth):
        z, small = _inproj(x2, mod[l], norm1_g[l].reshape(1, d), wz[l], ws[l], seq)
        fcol, frow = _fox_cumsum(small, bf_rows[l], batch, seq)
        o_fox = _fox_attention(z, fcol, frow, batch, seq)
        r = z[:, kv0:kv0 + N_KV * LANES].reshape(batch, seq // CMP_STRIDE, CMP_STRIDE * N_KV * LANES)
        kvcmp = _compress(r, wlo[l], whi[l], plo[l], phi[l])
        o_cmp, sel, gsig = _nsa_compressed(z, kvcmp, small, fvec, overlap, batch, seq)
        o_sel = _nsa_flash(z, sel, emat, gsig, tiles, batch, seq, windowed=False)
        o_win = _nsa_flash(z, sel, emat, gsig, tiles, batch, seq, windowed=True)
        o_conv = _conformer_conv(z, conv_w[l], conv_b[l].reshape(1, -1), conv_ln_g[l].reshape(1, -1),
                                 conv_ln_b[l].reshape(1, -1), batch, seq)
        x2 = _outproj(x2, mod[l], o_fox, o_cmp, o_sel, o_win, o_conv, w_out_b[l], seq)
        x2 = _mlp(x2, mod[l], norm2_g[l].reshape(1, d), w1_b[l], w2_b[l], fg, seq, final=(l == depth - 1))
    return x2.reshape(batch, seq, d)
```

```python
import functools
import math

import numpy as np
import jax
import jax.numpy as jnp
from jax import lax
from jax.experimental import pallas as pl
from jax.experimental.pallas import tpu as pltpu

F32 = jnp.float32
BF16 = jnp.bfloat16
HIGHEST = lax.Precision.HIGHEST

D_MODEL = 2048
HEAD_DIM = 64
N_FOX = 12
N_NSA = 12
N_KV = 3
GROUP = 4
D_CONV = 512
D_FF = 4 * D_MODEL
CMP_LEN = 32
CMP_STRIDE = 16
SEL_LEN = 64
N_SEL = 16
WINDOW = 512
CONV_WIDTH = 31
N_BUCKETS = 32
MAX_DISTANCE = 128
EPS = 1e-6
NEG = -1e30
BIG = 1e9

LANES = 128
HALF = LANES // 2

ZB_FQ, ZB_FK, ZB_FV, ZB_NQ = 0, 6, 12, 18
ZB_CA, ZB_CG = 24, 28
ZB_KVC, ZB_KVS, ZB_KVW = 32, 35, 38
NZ_BLOCKS = 42
NZ = NZ_BLOCKS * LANES
SMALL_FF, SMALL_NG = 0, 12

_SRC = dict(fq=0, fk=768, fv=1536, ff=2304, nq=2316, kc=3084, vc=3276, ks=3468, vs=3660,
            kw=3852, vw=4044, ng=4236, ca=4272, cg=4784)

TM_INPROJ = 1024
TN_INPROJ = 768
TM_OUT = 512
TM_MLP = 1024
TF_MLP = 512
ROW_CHUNK = 128
T_FOX = 512
NSUB_FOX = 4
T_NSA = 256
NSUB_NSA = 4
TQ_CMP = 128
TS_CONV = 512
TS_CUM = 512
HALO = 32
VMEM_LIMIT = 56 * 1024 * 1024


def _cparams(sem):
    return pltpu.CompilerParams(dimension_semantics=sem, vmem_limit_bytes=VMEM_LIMIT)


def _dot(a, b):
    return jnp.dot(a, b, preferred_element_type=F32)


def _dot_nt(a, b):
    return lax.dot_general(a, b, (((1,), (1,)), ((), ())), preferred_element_type=F32)


def _bucket_thresholds():
    d = np.arange(MAX_DISTANCE, dtype=np.int32)
    max_exact = N_BUCKETS // 2
    nf = np.maximum(d, 1).astype(np.float32)
    large = max_exact + (np.log(nf / np.float32(max_exact)) / np.float32(math.log(MAX_DISTANCE / max_exact))
                         * np.float32(N_BUCKETS - max_exact)).astype(np.int32)
    large = np.minimum(large, N_BUCKETS - 1)
    bucket = np.where(d < max_exact, d, large)
    assert bucket[-1] == N_BUCKETS - 1 and np.all(np.diff(bucket) >= 0)
    return [int(np.argmax(bucket >= k)) for k in range(N_BUCKETS)]


_T5_THRESH = _bucket_thresholds()


def _fvec_kernel(tbl_ref, out_ref):
    d = lax.broadcasted_iota(jnp.int32, out_ref.shape, 1)
    out = jnp.broadcast_to(tbl_ref[:, 0:1], out_ref.shape)
    for k in range(1, N_BUCKETS):
        out = jnp.where(d >= _T5_THRESH[k], tbl_ref[:, k:k + 1], out)
    out_ref[...] = out


def _bias_by_distance(rel_bias):
    tbl = jnp.zeros((16, LANES), F32).at[:N_NSA, :N_BUCKETS].set(rel_bias.astype(F32).T)
    return pl.pallas_call(
        _fvec_kernel, out_shape=jax.ShapeDtypeStruct((16, LANES), F32), name="t5_bias_by_distance",
    )(tbl)


def _mod_kernel(c_ref, w_ref, b_ref, o_ref):
    c = c_ref[...]
    ca = c * jax.nn.sigmoid(c)
    o_ref[...] = jnp.dot(ca, w_ref[...], precision=HIGHEST, preferred_element_type=F32) + b_ref[...]


def _modulation(c, w_mod, b_mod):
    depth, d, n = w_mod.shape
    b = c.shape[0]
    tn = 1024
    c8 = jnp.zeros((8, d), F32).at[:b].set(c)
    out = pl.pallas_call(
        _mod_kernel,
        grid=(depth, n // tn),
        in_specs=[pl.BlockSpec((8, d), lambda l, j: (0, 0)),
                  pl.BlockSpec((None, d, tn), lambda l, j: (l, 0, j)),
                  pl.BlockSpec((None, 1, tn), lambda l, j: (l, 0, j))],
        out_specs=pl.BlockSpec((None, 8, tn), lambda l, j: (l, 0, j)),
        out_shape=jax.ShapeDtypeStruct((depth, 8, n), F32),
        compiler_params=_cparams(("parallel", "parallel")),
        name="adaln_modulation",
    )(c8, w_mod, b_mod.reshape(depth, 1, n))
    return out[:, :b].reshape(depth, b, 6, d)


def _norm_mod(x, g, sc, sh):
    y = x * lax.rsqrt(jnp.mean(x * x, axis=-1, keepdims=True) + EPS) * g
    return y * (1.0 + sc) + sh


def _inproj_kernel(x_ref, mod_ref, g_ref, w_ref, ws_ref, z_ref, sm_ref, h_ref, *, tm):
    @pl.when(pl.program_id(1) == 0)
    def _():
        def chunk(c, carry):
            r = pl.multiple_of(c * ROW_CHUNK, ROW_CHUNK)
            h = _norm_mod(x_ref[pl.ds(r, ROW_CHUNK), :], g_ref[...], mod_ref[1:2, :], mod_ref[0:1, :])
            hb = h.astype(BF16)
            h_ref[pl.ds(r, ROW_CHUNK), :] = hb
            sm_ref[pl.ds(r, ROW_CHUNK), :] = _dot(hb, ws_ref[...])
            return carry
        lax.fori_loop(0, tm // ROW_CHUNK, chunk, 0)

    z_ref[...] = _dot(h_ref[...], w_ref[...]).astype(BF16)


def _inproj(x2, mod_l, g, wz, ws, seq):
    m, d = x2.shape
    tm, tn = TM_INPROJ, TN_INPROJ
    assert seq % tm == 0 and NZ % tn == 0
    return pl.pallas_call(
        functools.partial(_inproj_kernel, tm=tm),
        grid=(m // tm, NZ // tn),
        in_specs=[pl.BlockSpec((tm, d), lambda i, j: (i, 0)),
                  pl.BlockSpec((None, 6, d), lambda i, j: (i * tm // seq, 0, 0)),
                  pl.BlockSpec((1, d), lambda i, j: (0, 0)),
                  pl.BlockSpec((d, tn), lambda i, j: (0, j)),
                  pl.BlockSpec((d, LANES), lambda i, j: (0, 0))],
        out_specs=[pl.BlockSpec((tm, tn), lambda i, j: (i, j)),
                   pl.BlockSpec((tm, LANES), lambda i, j: (i, 0))],
        out_shape=[jax.ShapeDtypeStruct((m, NZ), BF16), jax.ShapeDtypeStruct((m, LANES), F32)],
        scratch_shapes=[pltpu.VMEM((tm, d), BF16)],
        compiler_params=_cparams(("parallel", "arbitrary")),
        name="inproj",
    )(x2, mod_l, g, wz, ws)


def _foxcum_kernel(sm_ref, bf_ref, fcol_ref, frow_ref, carry_ref, *, ts):
    @pl.when(pl.program_id(1) == 0)
    def _():
        carry_ref[...] = jnp.zeros_like(carry_ref)

    x = sm_ref[...] + bf_ref[...]
    ls = jnp.minimum(x, 0.0) - jnp.log1p(jnp.exp(-jnp.abs(x)))
    r = lax.broadcasted_iota(jnp.int32, (ts, ts), 0)
    c = lax.broadcasted_iota(jnp.int32, (ts, ts), 1)
    tri = jnp.where(r >= c, 1.0, 0.0).astype(F32)
    cum = jnp.dot(tri, ls, precision=HIGHEST, preferred_element_type=F32) + carry_ref[0:1, :]
    carry_ref[0:1, :] = cum[ts - 1:ts, :]
    cum_t = cum.T
    for hp in range(N_FOX // 2):
        fcol_ref[hp] = cum if hp == 0 else pltpu.roll(cum, LANES - 2 * hp, 1)
        frow_ref[hp, 0:2, :] = cum_t[2 * hp:2 * hp + 2, :]
        frow_ref[hp, 2:8, :] = jnp.zeros((6, ts), F32)


def _fox_cumsum(small, bf_row, batch, seq):
    ts = TS_CUM
    ns = seq // ts
    npair = N_FOX // 2
    return pl.pallas_call(
        functools.partial(_foxcum_kernel, ts=ts),
        grid=(batch, ns),
        in_specs=[pl.BlockSpec((ts, LANES), lambda b, s: (b * ns + s, 0)),
                  pl.BlockSpec((1, LANES), lambda b, s: (0, 0))],
        out_specs=[pl.BlockSpec((None, npair, ts, LANES), lambda b, s: (b, 0, s, 0)),
                   pl.BlockSpec((None, npair, 8, ts), lambda b, s: (b, 0, 0, s))],
        out_shape=[jax.ShapeDtypeStruct((batch, npair, seq, LANES), F32),
                   jax.ShapeDtypeStruct((batch, npair, 8, seq), F32)],
        scratch_shapes=[pltpu.VMEM((8, LANES), F32)],
        compiler_params=_cparams(("parallel", "arbitrary")),
        name="fox_cumsum",
    )(small, bf_row)


def _fox_kernel(q_ref, k_ref, v_ref, fc_ref, fr_ref, o_ref, m_ref, fq_ref, acc_ref, *, t, nsub):
    qb = pl.program_id(2)
    kb = pl.program_id(3)
    tb = nsub * t

    @pl.when(kb == 0)
    def _():
        m_ref[...] = jnp.full(m_ref.shape, NEG, F32)
        acc_ref[...] = jnp.zeros_like(acc_ref)
        for h in range(2):
            fq_ref[h] = jnp.broadcast_to(fc_ref[:, h:h + 1], (tb, LANES))

    lane = lax.broadcasted_iota(jnp.int32, (1, LANES), 1)

    def sub_tile(i, j, diagonal):
        rows = slice(i * t, (i + 1) * t)
        cols = slice(j * t, (j + 1) * t)
        q2 = q_ref[rows, :]
        k2 = k_ref[cols, :]
        v2 = v_ref[cols, :]
        if diagonal:
            causal = (lax.broadcasted_iota(jnp.int32, (t, t), 1) <= lax.broadcasted_iota(jnp.int32, (t, t), 0))
        for h in range(2):
            keep = (lane < HALF) if h == 0 else (lane >= HALF)
            qh = jnp.where(keep, q2, jnp.zeros_like(q2))
            vh = jnp.where(keep, v2, jnp.ones_like(v2))
            s = _dot_nt(qh, k2) - fr_ref[h:h + 1, cols]
            if diagonal:
                s = jnp.where(causal, s, NEG)
            fq = fq_ref[h, rows, :]
            m_old = m_ref[h, rows, :]
            m_new = jnp.maximum(m_old, jnp.max(s, axis=1, keepdims=True) + fq)
            shift = jnp.tile(fq - m_new, (1, t // LANES))
            p = jnp.exp(s + shift)
            acc_ref[h, rows, :] = jnp.exp(m_old - m_new) * acc_ref[h, rows, :] + _dot(p.astype(BF16), vh)
            m_ref[h, rows, :] = m_new

    @pl.when(kb < qb)
    def _():
        for i in range(nsub):
            for j in range(nsub):
                sub_tile(i, j, False)

    @pl.when(kb == qb)
    def _():
        for i in range(nsub):
            for j in range(i + 1):
                sub_tile(i, j, i == j)
        a0 = acc_ref[0]
        a1 = acc_ref[1]
        o_ref[...] = jnp.where(lane < HALF, a0 / pltpu.roll(a0, HALF, 1),
                               a1 / pltpu.roll(a1, HALF, 1)).astype(BF16)


def _fox_attention(z, fcol, frow, batch, seq):
    t, nsub = T_FOX, NSUB_FOX
    tb = t * nsub
    nb = seq // tb
    npair = N_FOX // 2
    m = batch * seq
    kv_row = lambda b, hp, qb, kb: b * nb + jnp.minimum(kb, qb)
    return pl.pallas_call(
        functools.partial(_fox_kernel, t=t, nsub=nsub),
        grid=(batch, npair, nb, nb),
        in_specs=[pl.BlockSpec((tb, LANES), lambda b, hp, qb, kb: (b * nb + qb, ZB_FQ + hp)),
                  pl.BlockSpec((tb, LANES), lambda b, hp, qb, kb: (kv_row(b, hp, qb, kb), ZB_FK + hp)),
                  pl.BlockSpec((tb, LANES), lambda b, hp, qb, kb: (kv_row(b, hp, qb, kb), ZB_FV + hp)),
                  pl.BlockSpec((None, None, tb, LANES), lambda b, hp, qb, kb: (b, hp, qb, 0)),
                  pl.BlockSpec((None, None, 8, tb), lambda b, hp, qb, kb: (b, hp, 0, jnp.minimum(kb, qb)))],
        out_specs=pl.BlockSpec((tb, LANES), lambda b, hp, qb, kb: (b * nb + qb, hp)),
        out_shape=jax.ShapeDtypeStruct((m, N_FOX * HEAD_DIM), BF16),
        scratch_shapes=[pltpu.VMEM((2, tb, LANES), F32), pltpu.VMEM((2, tb, LANES), F32),
                        pltpu.VMEM((2, tb, LANES), F32)],
        compiler_params=_cparams(("parallel", "parallel", "parallel", "arbitrary")),
        name="fox_attention",
    )(z, z, z, fcol, frow)


def _compress_kernel(r_ref, wlo_ref, whi_ref, plo_ref, phi_ref, o_ref, *, nc):
    bias = _dot(plo_ref[...], wlo_ref[...]) + _dot(phi_ref[...], whi_ref[...])
    for g in range(N_KV):
        r = r_ref[g]
        out = _dot(r, wlo_ref[...]) + pltpu.roll(_dot(r, whi_ref[...]), nc - 1, 0) + bias[0:1, :]
        o_ref[:, g * LANES:(g + 1) * LANES] = out.astype(BF16)


def _compress(r, wlo, whi, plo, phi):
    batch, ng, nc, k = r.shape
    return pl.pallas_call(
        functools.partial(_compress_kernel, nc=nc),
        grid=(batch,),
        in_specs=[pl.BlockSpec((None, ng, nc, k), lambda b: (b, 0, 0, 0)),
                  pl.BlockSpec((k, LANES), lambda b: (0, 0)),
                  pl.BlockSpec((k, LANES), lambda b: (0, 0)),
                  pl.BlockSpec((8, k), lambda b: (0, 0)),
                  pl.BlockSpec((8, k), lambda b: (0, 0))],
        out_specs=pl.BlockSpec((None, nc, ng * LANES), lambda b: (b, 0, 0)),
        out_shape=jax.ShapeDtypeStruct((batch, nc, ng * LANES), BF16),
        compiler_params=_cparams(("parallel",)),
        name="nsa_compress",
    )(r, wlo, whi, plo, phi)


def _head_queries(qf, lane):
    zero = jnp.zeros_like(qf)
    qa = jnp.where(lane < HALF, qf, zero).astype(BF16)
    qb = jnp.where(lane < HALF, pltpu.roll(qf, HALF, 1), zero).astype(BF16)
    return qa, qb


def _gather_bias(frow, idx, tq):
    tab = jnp.broadcast_to(frow, (tq, LANES))
    parts = [jnp.take_along_axis(tab, idx[:, c * LANES:(c + 1) * LANES], axis=1)
             for c in range(idx.shape[1] // LANES)]
    return parts[0] if len(parts) == 1 else jnp.concatenate(parts, axis=1)


def _nsacmp_kernel(q_ref, kvc_ref, sm_ref, fv_ref, ov_ref, oc_ref, sel_ref, gs_ref, imp_ref, *, tq, nc):
    qi = pl.program_id(1)
    t0 = qi * tq
    lane = lax.broadcasted_iota(jnp.int32, (1, LANES), 1)
    t_col = t0 + lax.broadcasted_iota(jnp.int32, (tq, 1), 0)
    cmp_end = lax.broadcasted_iota(jnp.int32, (1, nc), 1) * CMP_STRIDE + (CMP_LEN - 1)
    dist = t_col - cmp_end
    masked = jnp.where(dist >= 0, 0.0, NEG)
    any_valid = t_col >= CMP_LEN - 1
    idx = jnp.clip(dist, 0, MAX_DISTANCE - 1)

    gs = jax.nn.sigmoid(sm_ref[...])
    n_sel = SEL_LEN
    j_row = lax.broadcasted_iota(jnp.int32, (n_sel, tq), 0)
    t_row = t0 + lax.broadcasted_iota(jnp.int32, (n_sel, tq), 1)
    cur = jnp.right_shift(t_row, SEL_LEN.bit_length() - 1)
    forced = (j_row == 0) | (j_row == cur) | (j_row == cur - 1)
    valid_s = j_row * SEL_LEN <= t_row

    for g in range(N_KV):
        kv = kvc_ref[:, g * LANES:(g + 1) * LANES]
        gs_g = pltpu.roll(gs, LANES - (SMALL_NG + g * GROUP * 3), 1)
        gs_ref[:, g * LANES:(g + 1) * LANES] = gs_g
        psum = jnp.zeros((tq, nc), F32)
        outs = []
        for pair in range(2):
            c0 = (g * 2 + pair) * LANES
            qpair = _head_queries(q_ref[:, c0:c0 + LANES].astype(F32), lane)
            for sub in range(2):
                r = pair * 2 + sub
                h = g * GROUP + r
                s = _dot_nt(qpair[sub], kv) + _gather_bias(fv_ref[h:h + 1, :], idx, tq) + masked
                e = jnp.exp(s - jnp.max(s, axis=1, keepdims=True))
                p = e * jnp.where(any_valid, 1.0 / jnp.sum(e, axis=1, keepdims=True), 0.0)
                psum = psum + p
                o = _dot(p.astype(BF16), kv)
                outs.append(o * gs_g[:, r * 3:r * 3 + 1])
            oc_ref[:, c0:c0 + LANES] = jnp.where(lane < HALF, pltpu.roll(outs[-2], HALF, 1), outs[-1])

        p_hi = psum.astype(BF16)
        p_lo = (psum - p_hi.astype(F32)).astype(BF16)
        imp = _dot(p_hi, ov_ref[...]) + _dot(p_lo, ov_ref[...])
        imp_t = imp.T[0:n_sel, :]
        val = jnp.where(forced, BIG, jnp.where(valid_s, imp_t, NEG))
        imp_ref[...] = val

        def rank(j, cnt):
            row = imp_ref[pl.ds(j, 1), :]
            ahead = (row > val) | ((row == val) & (j < j_row))
            return cnt + jnp.where(ahead, 1.0, 0.0)

        cnt = lax.fori_loop(0, n_sel, rank, jnp.zeros((n_sel, tq), F32), unroll=4)
        unchosen = jnp.where(cnt < float(N_SEL), 0.0, NEG)
        unchosen = jnp.concatenate([unchosen, jnp.full((LANES - n_sel, tq), NEG, F32)], axis=0)
        sel_ref[:, g * LANES:(g + 1) * LANES] = unchosen.T.astype(BF16)


def _nsa_compressed(z, kvcmp, small, fvec, overlap, batch, seq):
    tq = TQ_CMP
    nq = seq // tq
    nc = kvcmp.shape[1]
    m = batch * seq
    w3 = N_KV * LANES
    return pl.pallas_call(
        functools.partial(_nsacmp_kernel, tq=tq, nc=nc),
        grid=(batch, nq),
        in_specs=[pl.BlockSpec((tq, N_NSA * HEAD_DIM), lambda b, i: (b * nq + i, ZB_NQ * LANES // (N_NSA * HEAD_DIM))),
                  pl.BlockSpec((None, nc, w3), lambda b, i: (b, 0, 0)),
                  pl.BlockSpec((tq, LANES), lambda b, i: (b * nq + i, 0)),
                  pl.BlockSpec((16, LANES), lambda b, i: (0, 0)),
                  pl.BlockSpec((nc, LANES), lambda b, i: (0, 0))],
        out_specs=[pl.BlockSpec((tq, N_NSA * HEAD_DIM), lambda b, i: (b * nq + i, 0)),
                   pl.BlockSpec((tq, w3), lambda b, i: (b * nq + i, 0)),
                   pl.BlockSpec((tq, w3), lambda b, i: (b * nq + i, 0))],
        out_shape=[jax.ShapeDtypeStruct((m, N_NSA * HEAD_DIM), F32),
                   jax.ShapeDtypeStruct((m, w3), BF16),
                   jax.ShapeDtypeStruct((m, w3), F32)],
        scratch_shapes=[pltpu.VMEM((SEL_LEN, tq), F32)],
        compiler_params=_cparams(("parallel", "parallel")),
        name="nsa_compressed",
    )(z, kvcmp, small, fvec, overlap)


def _nsaflash_kernel(*refs, t, nsub, nkk, windowed, gate_col):
    if windowed:
        qa_ref, qb_ref, kv_ref, gs_ref, tb_ref, o_ref, q4_ref, m_ref, acc_ref = refs
        sel_ref = e_ref = None
    else:
        qa_ref, qb_ref, kv_ref, sel_ref, e_ref, gs_ref, tb_ref, o_ref, q4_ref, m_ref, acc_ref = refs
    g = pl.program_id(1)
    qb = pl.program_id(2)
    kk = pl.program_id(3)
    far = tb_ref.shape[0] - 1
    lane = lax.broadcasted_iota(jnp.int32, (1, LANES), 1)

    @pl.when(kk == 0)
    def _():
        m_ref[...] = jnp.full(m_ref.shape, NEG, F32)
        acc_ref[...] = jnp.zeros_like(acc_ref)
        for i in range(nsub):
            rows = slice(i * t, (i + 1) * t)
            queries = (_head_queries(qa_ref[rows, :].astype(F32), lane)
                       + _head_queries(qb_ref[rows, :].astype(F32), lane))
            for r in range(GROUP):
                q4_ref[(i * GROUP + r) * t:(i * GROUP + r + 1) * t, :] = queries[r]

    def sub_tile(i, j, rel):
        tidx = rel if (windowed or rel <= 1) else far
        kv = kv_ref[j * t:(j + 1) * t, :]
        ones_v = jnp.where(lane < HALF, jnp.ones_like(kv), kv)
        if not windowed:
            unchosen = _dot(sel_ref[i * t:(i + 1) * t, :], e_ref[:, j * t:(j + 1) * t])
        for pair in range(2):
            base = (i * GROUP + 2 * pair) * t
            rows2 = slice(base, base + 2 * t)
            s2 = _dot_nt(q4_ref[rows2, :], kv)
            probs, alphas = [], []
            for sub in range(2):
                h = g * GROUP + 2 * pair + sub
                rows = slice(base + sub * t, base + (sub + 1) * t)
                m_old = m_ref[rows, :]
                s = s2[sub * t:(sub + 1) * t, :] + tb_ref[tidx, h]
                if not windowed:
                    s = s + unchosen
                m_new = jnp.maximum(m_old, jnp.max(s, axis=1, keepdims=True))
                p = jnp.exp(s - jnp.tile(m_new, (1, t // LANES)))
                m_ref[rows, :] = m_new
                alphas.append(jnp.exp(m_old - m_new))
                probs.append(p.astype(BF16))
            acc_ref[rows2, :] = (jnp.concatenate(alphas, axis=0) * acc_ref[rows2, :]
                                 + _dot(jnp.concatenate(probs, axis=0), ones_v))

    def block(offset):
        max_rel = far - 1 if windowed else None
        for i in range(nsub):
            for j in range(nsub):
                rel = offset * nsub + i - j
                if rel >= 0 and (max_rel is None or rel <= max_rel):
                    sub_tile(i, j, rel)

    if windowed:
        for c in range(nkk):
            @pl.when((kk == c) & (qb - (nkk - 1) + c >= 0))
            def _(c=c):
                block(nkk - 1 - c)
        last = kk == nkk - 1
    else:
        @pl.when(qb - kk == 0)
        def _():
            block(0)

        @pl.when(qb - kk == 1)
        def _():
            block(1)

        @pl.when(qb - kk >= 2)
        def _():
            block(2)
        last = kk == qb

    @pl.when(last)
    def _():
        for i in range(nsub):
            rows = slice(i * t, (i + 1) * t)
            gs = gs_ref[rows, :]
            o = []
            for r in range(GROUP):
                a = acc_ref[(i * GROUP + r) * t:(i * GROUP + r + 1) * t, :]
                o.append(a / pltpu.roll(a, HALF, 1) * gs[:, r * 3 + gate_col:r * 3 + gate_col + 1])
            o_ref[rows, 0:LANES] = jnp.where(lane < HALF, pltpu.roll(o[0], HALF, 1), o[1])
            o_ref[rows, LANES:2 * LANES] = jnp.where(lane < HALF, pltpu.roll(o[2], HALF, 1), o[3])


def _bias_tile_kernel(fv_ref, o_ref, *, t, nrel):
    rel = pl.program_id(0)
    h = pl.program_id(1)
    dist = rel * t + lax.broadcasted_iota(jnp.int32, (t, t), 0) - lax.broadcasted_iota(jnp.int32, (t, t), 1)
    bias = _gather_bias(fv_ref[pl.ds(h, 1), :], jnp.clip(dist, 0, MAX_DISTANCE - 1), t)
    o_ref[...] = jnp.where((dist >= 0) & ((dist < WINDOW) | (rel == nrel)), bias, NEG)


def _bias_tiles(fvec):
    t = T_NSA
    nrel = WINDOW // t + 1
    return pl.pallas_call(
        functools.partial(_bias_tile_kernel, t=t, nrel=nrel),
        grid=(nrel + 1, N_NSA),
        in_specs=[pl.BlockSpec((16, LANES), lambda r, h: (0, 0))],
        out_specs=pl.BlockSpec((None, None, t, t), lambda r, h: (r, h, 0, 0)),
        out_shape=jax.ShapeDtypeStruct((nrel + 1, N_NSA, t, t), F32),
        compiler_params=_cparams(("parallel", "parallel")),
        name="nsa_bias_tiles",
    )(fvec)


def _nsa_flash(z, sel, emat, gsig, tiles, batch, seq, windowed):
    t, nsub = T_NSA, NSUB_NSA
    tb = t * nsub
    nb = seq // tb
    m = batch * seq
    max_rel = WINDOW // t
    nkk = ((max_rel + nsub - 1) // nsub + 1) if windowed else nb
    assert t >= MAX_DISTANCE and 2 * t <= WINDOW and WINDOW % t == 0 and seq % tb == 0
    zb_kv = ZB_KVW if windowed else ZB_KVS
    if windowed:
        kidx = lambda qb, kk: jnp.maximum(qb - (nkk - 1) + kk, 0)
    else:
        kidx = lambda qb, kk: jnp.minimum(kk, qb)
    in_specs = [pl.BlockSpec((tb, LANES), lambda b, g, qb, kk: (b * nb + qb, ZB_NQ + 2 * g)),
                pl.BlockSpec((tb, LANES), lambda b, g, qb, kk: (b * nb + qb, ZB_NQ + 2 * g + 1)),
                pl.BlockSpec((tb, LANES), lambda b, g, qb, kk: (b * nb + kidx(qb, kk), zb_kv + g))]
    args = [z, z, z]
    if not windowed:
        in_specs += [pl.BlockSpec((tb, LANES), lambda b, g, qb, kk: (b * nb + qb, g)),
                     pl.BlockSpec((LANES, tb), lambda b, g, qb, kk: (0, kidx(qb, kk)))]
        args += [sel, emat]
    in_specs += [pl.BlockSpec((tb, LANES), lambda b, g, qb, kk: (b * nb + qb, g)),
                 pl.BlockSpec(tiles.shape, lambda b, g, qb, kk: (0, 0, 0, 0))]
    args += [gsig, tiles]
    return pl.pallas_call(
        functools.partial(_nsaflash_kernel, t=t, nsub=nsub, nkk=nkk, windowed=windowed,
                          gate_col=2 if windowed else 1),
        grid=(batch, N_KV, nb, nkk),
        in_specs=in_specs,
        out_specs=pl.BlockSpec((tb, 2 * LANES), lambda b, g, qb, kk: (b * nb + qb, g)),
        out_shape=jax.ShapeDtypeStruct((m, N_NSA * HEAD_DIM), F32),
        scratch_shapes=[pltpu.VMEM((nsub * GROUP * t, LANES), BF16), pltpu.VMEM((nsub * GROUP * t, LANES), F32),
                        pltpu.VMEM((nsub * GROUP * t, LANES), F32)],
        compiler_params=_cparams(("parallel", "parallel", "parallel", "arbitrary")),
        name="nsa_window" if windowed else "nsa_selected",
    )(*args)


def _conv_kernel(a_ref, g_ref, ah_ref, gh_ref, w_ref, b_ref, lg_ref, lb_ref, o_ref, u_ref, *, ts):
    first = pl.program_id(1) == 0
    halo = ah_ref[...].astype(F32) * jax.nn.sigmoid(gh_ref[...].astype(F32))
    u_ref[0:HALO, :] = jnp.where(first, 0.0, halo)
    u_ref[HALO:HALO + ts, :] = a_ref[...].astype(F32) * jax.nn.sigmoid(g_ref[...].astype(F32))
    base = HALO - (CONV_WIDTH - 1)
    y = jnp.zeros((ts, D_CONV), F32)
    for k in range(CONV_WIDTH):
        y = y + u_ref[base + k:base + k + ts, :] * w_ref[k:k + 1, :]
    y = y + b_ref[...]
    mu = jnp.mean(y, axis=-1, keepdims=True)
    yc = y - mu
    var = jnp.mean(yc * yc, axis=-1, keepdims=True)
    yn = yc * lax.rsqrt(var + EPS) * lg_ref[...] + lb_ref[...]
    o_ref[...] = (yn * jax.nn.sigmoid(yn)).astype(BF16)


def _conformer_conv(z, w, b, lg, lb, batch, seq):
    ts = TS_CONV
    ns = seq // ts
    m = batch * seq
    cb = D_CONV // LANES
    hb = ts // HALO
    halo_row = lambda bb, s: jnp.maximum((bb * ns + s) * hb - 1, 0)
    return pl.pallas_call(
        functools.partial(_conv_kernel, ts=ts),
        grid=(batch, ns),
        in_specs=[pl.BlockSpec((ts, D_CONV), lambda bb, s: (bb * ns + s, ZB_CA // cb)),
                  pl.BlockSpec((ts, D_CONV), lambda bb, s: (bb * ns + s, ZB_CG // cb)),
                  pl.BlockSpec((HALO, D_CONV), lambda bb, s: (halo_row(bb, s), ZB_CA // cb)),
                  pl.BlockSpec((HALO, D_CONV), lambda bb, s: (halo_row(bb, s), ZB_CG // cb)),
                  pl.BlockSpec((CONV_WIDTH, D_CONV), lambda bb, s: (0, 0)),
                  pl.BlockSpec((1, D_CONV), lambda bb, s: (0, 0)),
                  pl.BlockSpec((1, D_CONV), lambda bb, s: (0, 0)),
                  pl.BlockSpec((1, D_CONV), lambda bb, s: (0, 0))],
        out_specs=pl.BlockSpec((ts, D_CONV), lambda bb, s: (bb * ns + s, 0)),
        out_shape=jax.ShapeDtypeStruct((m, D_CONV), BF16),
        scratch_shapes=[pltpu.VMEM((HALO + ts, D_CONV), F32)],
        compiler_params=_cparams(("parallel", "parallel")),
        name="conformer_conv",
    )(z, z, z, z, w, b, lg, lb)


def _outproj_kernel(x_ref, mod_ref, fox_ref, oc_ref, os_ref, ow_ref, cv_ref, w_ref, o_ref):
    d_fox = N_FOX * HEAD_DIM
    d_nsa = N_NSA * HEAD_DIM
    nsa = (oc_ref[...] + os_ref[...] + ow_ref[...]).astype(BF16)
    acc = _dot(fox_ref[...], w_ref[0:d_fox, :])
    acc = acc + _dot(nsa, w_ref[d_fox:d_fox + d_nsa, :])
    acc = acc + _dot(cv_ref[...], w_ref[d_fox + d_nsa:, :])
    o_ref[...] = x_ref[...] + mod_ref[2:3, :] * acc


def _outproj(x2, mod_l, fox, oc, osel, owin, conv, w, seq):
    m, d = x2.shape
    tm = TM_OUT
    row = lambda i: (i, 0)
    return pl.pallas_call(
        _outproj_kernel,
        grid=(m // tm,),
        in_specs=[pl.BlockSpec((tm, d), row),
                  pl.BlockSpec((None, 6, d), lambda i: (i * tm // seq, 0, 0)),
                  pl.BlockSpec((tm, N_FOX * HEAD_DIM), row),
                  pl.BlockSpec((tm, N_NSA * HEAD_DIM), row),
                  pl.BlockSpec((tm, N_NSA * HEAD_DIM), row),
                  pl.BlockSpec((tm, N_NSA * HEAD_DIM), row),
                  pl.BlockSpec((tm, D_CONV), row),
                  pl.BlockSpec((d, d), lambda i: (0, 0))],
        out_specs=pl.BlockSpec((tm, d), row),
        out_shape=jax.ShapeDtypeStruct((m, d), F32),
        compiler_params=_cparams(("parallel",)),
        name="outproj",
    )(x2, mod_l, fox, oc, osel, owin, conv, w)


def _mlp_kernel(x_ref, mod_ref, g_ref, w1_ref, w2_ref, fg_ref, o_ref, h_ref, *, tm, final):
    f = pl.program_id(1)

    @pl.when(f == 0)
    def _():
        def chunk(c, carry):
            r = pl.multiple_of(c * ROW_CHUNK, ROW_CHUNK)
            h = _norm_mod(x_ref[pl.ds(r, ROW_CHUNK), :], g_ref[...], mod_ref[4:5, :], mod_ref[3:4, :])
            h_ref[pl.ds(r, ROW_CHUNK), :] = h.astype(BF16)
            return carry
        lax.fori_loop(0, tm // ROW_CHUNK, chunk, 0)
        o_ref[...] = jnp.zeros_like(o_ref)

    a = jnp.maximum(_dot(h_ref[...], w1_ref[...]), 0.0)
    o_ref[...] += _dot((a * a).astype(BF16), w2_ref[...])

    @pl.when(f == pl.num_programs(1) - 1)
    def _():
        def chunk(c, carry):
            r = pl.multiple_of(c * ROW_CHUNK, ROW_CHUNK)
            y = x_ref[pl.ds(r, ROW_CHUNK), :] + mod_ref[5:6, :] * o_ref[pl.ds(r, ROW_CHUNK), :]
            if final:
                y = y * lax.rsqrt(jnp.mean(y * y, axis=-1, keepdims=True) + EPS) * fg_ref[...]
            o_ref[pl.ds(r, ROW_CHUNK), :] = y
            return carry
        lax.fori_loop(0, tm // ROW_CHUNK, chunk, 0)


def _mlp(x2, mod_l, g, w1, w2, fg, seq, final):
    m, d = x2.shape
    tm, tf = TM_MLP, TF_MLP
    dff = w1.shape[1]
    return pl.pallas_call(
        functools.partial(_mlp_kernel, tm=tm, final=final),
        grid=(m // tm, dff // tf),
        in_specs=[pl.BlockSpec((tm, d), lambda i, f: (i, 0), pipeline_mode=pl.Buffered(1)),
                  pl.BlockSpec((None, 6, d), lambda i, f: (i * tm // seq, 0, 0)),
                  pl.BlockSpec((1, d), lambda i, f: (0, 0)),
                  pl.BlockSpec((d, tf), lambda i, f: (0, f)),
                  pl.BlockSpec((tf, d), lambda i, f: (f, 0)),
                  pl.BlockSpec((1, d), lambda i, f: (0, 0))],
        out_specs=pl.BlockSpec((tm, d), lambda i, f: (i, 0)),
        out_shape=jax.ShapeDtypeStruct((m, d), F32),
        scratch_shapes=[pltpu.VMEM((tm, d), BF16)],
        compiler_params=_cparams(("parallel", "arbitrary")),
        name="mlp_final" if final else "mlp",
    )(x2, mod_l, g, w1, w2, fg)


def _prep_inproj_weights(w_in):
    depth, d, _ = w_in.shape
    sl = lambda name, width, off=0: w_in[:, :, _SRC[name] + off:_SRC[name] + off + width].astype(BF16)
    scale = HEAD_DIM ** -0.5
    pieces = [sl("fq", 768) * scale, sl("fk", 768), sl("fv", 768), sl("nq", 768) * scale,
              sl("ca", D_CONV), sl("cg", D_CONV)]
    for kname, vname in (("kc", "vc"), ("ks", "vs"), ("kw", "vw")):
        for g in range(N_KV):
            pieces += [sl(kname, HEAD_DIM, g * HEAD_DIM), sl(vname, HEAD_DIM, g * HEAD_DIM)]
    pieces.append(jnp.zeros((depth, d, LANES), BF16))
    wz = jnp.concatenate(pieces, axis=2)
    assert wz.shape[2] == NZ
    ws = jnp.concatenate([sl("ff", N_FOX), sl("ng", 3 * N_NSA),
                          jnp.zeros((depth, d, LANES - N_FOX - 3 * N_NSA), BF16)], axis=2)
    return wz, ws


def _prep_compress_weights(w_cmp_k, w_cmp_v, pos_cmp):
    depth = w_cmp_k.shape[0]
    slots = 2
    wkv = jnp.stack([w_cmp_k, w_cmp_v], axis=1)
    big = jnp.einsum("zklde,kq->zlkdqe", wkv, jnp.eye(2, dtype=F32))
    big = big.reshape(depth, CMP_LEN, slots * HEAD_DIM, slots * HEAD_DIM)
    half = CMP_LEN // 2
    kdim = half * slots * HEAD_DIM
    wlo = big[:, :half].reshape(depth, kdim, slots * HEAD_DIM).astype(BF16)
    whi = big[:, half:].reshape(depth, kdim, slots * HEAD_DIM).astype(BF16)
    pos = jnp.broadcast_to(pos_cmp[:, :, None, :], (depth, CMP_LEN, slots, HEAD_DIM))
    plo = jnp.zeros((depth, 8, kdim), F32).at[:, 0].set(pos[:, :half].reshape(depth, kdim)).astype(BF16)
    phi = jnp.zeros((depth, 8, kdim), F32).at[:, 0].set(pos[:, half:].reshape(depth, kdim)).astype(BF16)
    return wlo, whi, plo, phi


def _selection_constants(seq):
    nc = seq // CMP_STRIDE
    n_cmp = (seq - CMP_LEN) // CMP_STRIDE + 1
    n = np.arange(nc)[:, None]
    j = np.arange(LANES)[None, :]
    overlap = ((n * CMP_STRIDE < j * SEL_LEN + SEL_LEN) & (n * CMP_STRIDE + CMP_LEN - 1 >= j * SEL_LEN)
               & (n < n_cmp) & (j < seq // SEL_LEN))
    emat = (np.arange(seq)[None, :] // SEL_LEN) == np.arange(LANES)[:, None]
    return jnp.asarray(overlap, BF16), jnp.asarray(emat, BF16)


def kernel(x, c, w_mod, b_mod, norm1_g, w_in, b_f, w_cmp_k, w_cmp_v, pos_cmp, conv_w, conv_b, conv_ln_g,
           conv_ln_b, w_out, norm2_g, w_mlp1, w_mlp2, rel_bias, final_g):
    batch, seq, d = x.shape
    depth = w_in.shape[0]
    assert d == D_MODEL and seq % TM_INPROJ == 0 and seq // SEL_LEN <= SEL_LEN

    wz, ws = _prep_inproj_weights(w_in)
    wlo, whi, plo, phi = _prep_compress_weights(w_cmp_k, w_cmp_v, pos_cmp)
    w_out_b = w_out.astype(BF16)
    w1_b = w_mlp1.astype(BF16)
    w2_b = w_mlp2.astype(BF16)
    overlap, emat = _selection_constants(seq)
    bf_rows = jnp.zeros((depth, 1, LANES), F32).at[:, 0, SMALL_FF:SMALL_FF + N_FOX].set(b_f)

    mod = _modulation(c, w_mod, b_mod)
    fvec = _bias_by_distance(rel_bias)
    tiles = _bias_tiles(fvec)
    fg = final_g.reshape(1, d)

    x2 = x.reshape(batch * seq, d)
    kv0 = ZB_KVC * LANES
    for l in range(depth):
        z, small = _inproj(x2, mod[l], norm1_g[l].reshape(1, d), wz[l], ws[l], seq)
        fcol, frow = _fox_cumsum(small, bf_rows[l], batch, seq)
        o_fox = _fox_attention(z, fcol, frow, batch, seq)
        r = z[:, kv0:kv0 + N_KV * LANES].reshape(batch, seq // CMP_STRIDE, CMP_STRIDE, N_KV, LANES)
        r = r.transpose(0, 3, 1, 2, 4).reshape(batch, N_KV, seq // CMP_STRIDE, CMP_STRIDE * LANES)
        kvcmp = _compress(r, wlo[l], whi[l], plo[l], phi[l])
        o_cmp, sel, gsig = _nsa_compressed(z, kvcmp, small, fvec, overlap, batch, seq)
        o_sel = _nsa_flash(z, sel, emat, gsig, tiles, batch, seq, windowed=False)
        o_win = _nsa_flash(z, sel, emat, gsig, tiles, batch, seq, windowed=True)
        o_conv = _conformer_conv(z, conv_w[l], conv_b[l].reshape(1, -1), conv_ln_g[l].reshape(1, -1),
                                 conv_ln_b[l].reshape(1, -1), batch, seq)
        x2 = _outproj(x2, mod[l], o_fox, o_cmp, o_sel, o_win, o_conv, w_out_b[l], seq)
        x2 = _mlp(x2, mod[l], norm2_g[l].reshape(1, d), w1_b[l], w2_b[l], fg, seq, final=(l == depth - 1))
    return x2.reshape(batch, seq, d)
```

```python
import functools
import math

import numpy as np
import jax
import jax.numpy as jnp
from jax import lax
from jax.experimental import pallas as pl
from jax.experimental.pallas import tpu as pltpu

F32 = jnp.float32
BF16 = jnp.bfloat16
HIGHEST = lax.Precision.HIGHEST

D_MODEL = 2048
HEAD_DIM = 64
N_FOX = 12
N_NSA = 12
N_KV = 3
GROUP = 4
D_CONV = 512
D_FF = 4 * D_MODEL
CMP_LEN = 32
CMP_STRIDE = 16
SEL_LEN = 64
N_SEL = 16
WINDOW = 512
CONV_WIDTH = 31
N_BUCKETS = 32
MAX_DISTANCE = 128
EPS = 1e-6
NEG = -1e30
BIG = 1e9
LOG2E = math.log2(math.e)

LANES = 128
SUBLANES = 8
HALF = LANES // 2

ZB_FQ, ZB_FK, ZB_FV, ZB_NQ = 0, 6, 12, 18
ZB_CA, ZB_CG = 24, 28
ZB_KVC, ZB_KVS, ZB_KVW = 32, 35, 38
NZ_BLOCKS = 42
NZ = NZ_BLOCKS * LANES
SMALL_FF, SMALL_NG = 0, 12

_SRC = dict(fq=0, fk=768, fv=1536, ff=2304, nq=2316, kc=3084, vc=3276, ks=3468, vs=3660,
            kw=3852, vw=4044, ng=4236, ca=4272, cg=4784)

TM_INPROJ = 1024
TN_INPROJ = 768
TM_OUT = 512
TM_MLP = 1024
TF_MLP = 512
ROW_CHUNK = 128
T_FOX = 512
NSUB_FOX = 4
T_NSA = 256
NSUB_NSA = 4
TQ_CMP = 512
TS_CONV = 512
TS_CUM = 512
HALO = 32
VMEM_LIMIT = 56 * 1024 * 1024


def _cparams(sem):
    return pltpu.CompilerParams(dimension_semantics=sem, vmem_limit_bytes=VMEM_LIMIT)


def _dot(a, b):
    return jnp.dot(a, b, preferred_element_type=F32)


def _dot_nt(a, b):
    return lax.dot_general(a, b, (((1,), (1,)), ((), ())), preferred_element_type=F32)


def _bucket_thresholds():
    d = np.arange(MAX_DISTANCE, dtype=np.int32)
    max_exact = N_BUCKETS // 2
    nf = np.maximum(d, 1).astype(np.float32)
    large = max_exact + (np.log(nf / np.float32(max_exact)) / np.float32(math.log(MAX_DISTANCE / max_exact))
                         * np.float32(N_BUCKETS - max_exact)).astype(np.int32)
    large = np.minimum(large, N_BUCKETS - 1)
    bucket = np.where(d < max_exact, d, large)
    assert bucket[-1] == N_BUCKETS - 1 and np.all(np.diff(bucket) >= 0)
    return [int(np.argmax(bucket >= k)) for k in range(N_BUCKETS)]


_T5_THRESH = _bucket_thresholds()


def _fvec_kernel(tbl_ref, out_ref):
    d = lax.broadcasted_iota(jnp.int32, out_ref.shape, 1)
    out = jnp.broadcast_to(tbl_ref[:, 0:1], out_ref.shape)
    for k in range(1, N_BUCKETS):
        out = jnp.where(d >= _T5_THRESH[k], tbl_ref[:, k:k + 1], out)
    out_ref[...] = out * LOG2E


def _bias_by_distance(rel_bias):
    tbl = jnp.zeros((16, LANES), F32).at[:N_NSA, :N_BUCKETS].set(rel_bias.astype(F32).T)
    return pl.pallas_call(
        _fvec_kernel, out_shape=jax.ShapeDtypeStruct((16, LANES), F32), name="t5_bias_by_distance",
    )(tbl)


def _mod_kernel(c_ref, w_ref, b_ref, o_ref):
    c = c_ref[...]
    ca = c * jax.nn.sigmoid(c)
    o_ref[...] = jnp.dot(ca, w_ref[...], precision=HIGHEST, preferred_element_type=F32) + b_ref[...]


def _modulation(c, w_mod, b_mod):
    depth, d, n = w_mod.shape
    b = c.shape[0]
    tn = 1024
    c8 = jnp.zeros((8, d), F32).at[:b].set(c)
    out = pl.pallas_call(
        _mod_kernel,
        grid=(depth, n // tn),
        in_specs=[pl.BlockSpec((8, d), lambda l, j: (0, 0)),
                  pl.BlockSpec((None, d, tn), lambda l, j: (l, 0, j)),
                  pl.BlockSpec((None, 1, tn), lambda l, j: (l, 0, j))],
        out_specs=pl.BlockSpec((None, 8, tn), lambda l, j: (l, 0, j)),
        out_shape=jax.ShapeDtypeStruct((depth, 8, n), F32),
        compiler_params=_cparams(("parallel", "parallel")),
        name="adaln_modulation",
    )(c8, w_mod, b_mod.reshape(depth, 1, n))
    return out[:, :b].reshape(depth, b, 6, d)


def _norm_mod(x, g, sc, sh):
    y = x * lax.rsqrt(jnp.mean(x * x, axis=-1, keepdims=True) + EPS) * g
    return y * (1.0 + sc) + sh


def _inproj_kernel(x_ref, mod_ref, g_ref, w_ref, ws_ref, z_ref, sm_ref, h_ref, *, tm):
    @pl.when(pl.program_id(1) == 0)
    def _():
        def chunk(c, carry):
            r = pl.multiple_of(c * ROW_CHUNK, ROW_CHUNK)
            h = _norm_mod(x_ref[pl.ds(r, ROW_CHUNK), :], g_ref[...], mod_ref[1:2, :], mod_ref[0:1, :])
            hb = h.astype(BF16)
            h_ref[pl.ds(r, ROW_CHUNK), :] = hb
            sm_ref[pl.ds(r, ROW_CHUNK), :] = _dot(hb, ws_ref[...])
            return carry
        lax.fori_loop(0, tm // ROW_CHUNK, chunk, 0)

    z_ref[...] = _dot(h_ref[...], w_ref[...]).astype(BF16)


def _inproj(x2, mod_l, g, wz, ws, layer, seq):
    m, d = x2.shape
    tm, tn = TM_INPROJ, TN_INPROJ
    assert seq % tm == 0 and NZ % tn == 0
    return pl.pallas_call(
        functools.partial(_inproj_kernel, tm=tm),
        grid=(m // tm, NZ // tn),
        in_specs=[pl.BlockSpec((tm, d), lambda i, j: (i, 0)),
                  pl.BlockSpec((None, 6, d), lambda i, j: (i * tm // seq, 0, 0)),
                  pl.BlockSpec((1, d), lambda i, j: (0, 0)),
                  pl.BlockSpec((None, d, tn), lambda i, j: (layer, 0, j)),
                  pl.BlockSpec((None, d, LANES), lambda i, j: (layer, 0, 0))],
        out_specs=[pl.BlockSpec((tm, tn), lambda i, j: (i, j)),
                   pl.BlockSpec((tm, LANES), lambda i, j: (i, 0))],
        out_shape=[jax.ShapeDtypeStruct((m, NZ), BF16), jax.ShapeDtypeStruct((m, LANES), F32)],
        scratch_shapes=[pltpu.VMEM((tm, d), BF16)],
        compiler_params=_cparams(("parallel", "arbitrary")),
        name="inproj",
    )(x2, mod_l, g, wz, ws)


def _foxcum_kernel(sm_ref, bf_ref, fcol_ref, frow_ref, carry_ref, *, ts):
    @pl.when(pl.program_id(1) == 0)
    def _():
        carry_ref[...] = jnp.zeros_like(carry_ref)

    x = sm_ref[...] + bf_ref[...]
    ls = (jnp.minimum(x, 0.0) - jnp.log1p(jnp.exp(-jnp.abs(x)))) * LOG2E
    r = lax.broadcasted_iota(jnp.int32, (ts, ts), 0)
    c = lax.broadcasted_iota(jnp.int32, (ts, ts), 1)
    tri = jnp.where(r >= c, 1.0, 0.0).astype(F32)
    cum = jnp.dot(tri, ls, precision=HIGHEST, preferred_element_type=F32) + carry_ref[0:1, :]
    carry_ref[0:1, :] = cum[ts - 1:ts, :]
    cum_t = cum.T
    for hp in range(N_FOX // 2):
        fcol_ref[hp] = cum if hp == 0 else pltpu.roll(cum, LANES - 2 * hp, 1)
        frow_ref[hp, 0:2, :] = cum_t[2 * hp:2 * hp + 2, :]
        frow_ref[hp, 2:8, :] = jnp.zeros((6, ts), F32)


def _fox_cumsum(small, bf_row, batch, seq):
    ts = TS_CUM
    ns = seq // ts
    npair = N_FOX // 2
    return pl.pallas_call(
        functools.partial(_foxcum_kernel, ts=ts),
        grid=(batch, ns),
        in_specs=[pl.BlockSpec((ts, LANES), lambda b, s: (b * ns + s, 0)),
                  pl.BlockSpec((1, LANES), lambda b, s: (0, 0))],
        out_specs=[pl.BlockSpec((None, npair, ts, LANES), lambda b, s: (b, 0, s, 0)),
                   pl.BlockSpec((None, npair, 8, ts), lambda b, s: (b, 0, 0, s))],
        out_shape=[jax.ShapeDtypeStruct((batch, npair, seq, LANES), F32),
                   jax.ShapeDtypeStruct((batch, npair, 8, seq), F32)],
        scratch_shapes=[pltpu.VMEM((8, LANES), F32)],
        compiler_params=_cparams(("parallel", "arbitrary")),
        name="fox_cumsum",
    )(small, bf_row)


def _fox_kernel(q_ref, k_ref, v_ref, fc_ref, fr_ref, o_ref, m_ref, fq_ref, acc_ref, *, t, nsub):
    qb = pl.program_id(2)
    kb = pl.program_id(3)
    tb = nsub * t

    @pl.when(kb == 0)
    def _():
        m_ref[...] = jnp.full(m_ref.shape, NEG, F32)
        acc_ref[...] = jnp.zeros_like(acc_ref)
        for h in range(2):
            fq_ref[h] = jnp.broadcast_to(fc_ref[:, h:h + 1], (tb, LANES))

    lane = lax.broadcasted_iota(jnp.int32, (1, LANES), 1)

    def sub_tile(i, j, diagonal):
        rows = slice(i * t, (i + 1) * t)
        cols = slice(j * t, (j + 1) * t)
        q2 = q_ref[rows, :]
        k2 = k_ref[cols, :]
        v2 = v_ref[cols, :]
        if diagonal:
            causal = (lax.broadcasted_iota(jnp.int32, (t, t), 1) <= lax.broadcasted_iota(jnp.int32, (t, t), 0))
        for h in range(2):
            keep = (lane < HALF) if h == 0 else (lane >= HALF)
            qh = jnp.where(keep, q2, jnp.zeros_like(q2))
            vh = jnp.where(keep, v2, jnp.ones_like(v2))
            s = _dot_nt(qh, k2) - fr_ref[h:h + 1, cols]
            if diagonal:
                s = jnp.where(causal, s, NEG)
            fq = fq_ref[h, rows, :]
            m_old = m_ref[h, rows, :]
            m_new = jnp.maximum(m_old, jnp.max(s, axis=1, keepdims=True) + fq)
            shift = jnp.tile(fq - m_new, (1, t // LANES))
            p = jnp.exp2(s + shift)
            acc_ref[h, rows, :] = jnp.exp2(m_old - m_new) * acc_ref[h, rows, :] + _dot(p.astype(BF16), vh)
            m_ref[h, rows, :] = m_new

    @pl.when(kb < qb)
    def _():
        for i in range(nsub):
            for j in range(nsub):
                sub_tile(i, j, False)

    @pl.when(kb == qb)
    def _():
        for i in range(nsub):
            for j in range(i + 1):
                sub_tile(i, j, i == j)
        a0 = acc_ref[0]
        a1 = acc_ref[1]
        o_ref[...] = jnp.where(lane < HALF, a0 / pltpu.roll(a0, HALF, 1),
                               a1 / pltpu.roll(a1, HALF, 1)).astype(BF16)


def _fox_attention(z, fcol, frow, batch, seq):
    t, nsub = T_FOX, NSUB_FOX
    tb = t * nsub
    nb = seq // tb
    npair = N_FOX // 2
    m = batch * seq
    kv_row = lambda b, hp, qb, kb: b * nb + jnp.minimum(kb, qb)
    return pl.pallas_call(
        functools.partial(_fox_kernel, t=t, nsub=nsub),
        grid=(batch, npair, nb, nb),
        in_specs=[pl.BlockSpec((tb, LANES), lambda b, hp, qb, kb: (b * nb + qb, ZB_FQ + hp)),
                  pl.BlockSpec((tb, LANES), lambda b, hp, qb, kb: (kv_row(b, hp, qb, kb), ZB_FK + hp)),
                  pl.BlockSpec((tb, LANES), lambda b, hp, qb, kb: (kv_row(b, hp, qb, kb), ZB_FV + hp)),
                  pl.BlockSpec((None, None, tb, LANES), lambda b, hp, qb, kb: (b, hp, qb, 0)),
                  pl.BlockSpec((None, None, 8, tb), lambda b, hp, qb, kb: (b, hp, 0, jnp.minimum(kb, qb)))],
        out_specs=pl.BlockSpec((tb, LANES), lambda b, hp, qb, kb: (b * nb + qb, hp)),
        out_shape=jax.ShapeDtypeStruct((m, N_FOX * HEAD_DIM), BF16),
        scratch_shapes=[pltpu.VMEM((2, tb, LANES), F32), pltpu.VMEM((2, tb, LANES), F32),
                        pltpu.VMEM((2, tb, LANES), F32)],
        compiler_params=_cparams(("parallel", "parallel", "parallel", "arbitrary")),
        name="fox_attention",
    )(z, z, z, fcol, frow)


def _compress_kernel(r_ref, wlo_ref, whi_ref, plo_ref, phi_ref, o_ref, *, nc):
    bias = _dot(plo_ref[...], wlo_ref[...]) + _dot(phi_ref[...], whi_ref[...])
    for g in range(N_KV):
        r = r_ref[g]
        out = _dot(r, wlo_ref[...]) + pltpu.roll(_dot(r, whi_ref[...]), nc - 1, 0) + bias[0:1, :]
        o_ref[:, g * LANES:(g + 1) * LANES] = out.astype(BF16)


def _compress(r, wlo, whi, plo, phi):
    batch, ng, nc, k = r.shape
    return pl.pallas_call(
        functools.partial(_compress_kernel, nc=nc),
        grid=(batch,),
        in_specs=[pl.BlockSpec((None, ng, nc, k), lambda b: (b, 0, 0, 0)),
                  pl.BlockSpec((k, LANES), lambda b: (0, 0)),
                  pl.BlockSpec((k, LANES), lambda b: (0, 0)),
                  pl.BlockSpec((8, k), lambda b: (0, 0)),
                  pl.BlockSpec((8, k), lambda b: (0, 0))],
        out_specs=pl.BlockSpec((None, nc, ng * LANES), lambda b: (b, 0, 0)),
        out_shape=jax.ShapeDtypeStruct((batch, nc, ng * LANES), BF16),
        compiler_params=_cparams(("parallel",)),
        name="nsa_compress",
    )(r, wlo, whi, plo, phi)


def _head_queries(qf, lane):
    zero = jnp.zeros_like(qf)
    qa = jnp.where(lane < HALF, qf, zero).astype(BF16)
    qb = jnp.where(lane < HALF, pltpu.roll(qf, HALF, 1), zero).astype(BF16)
    return qa, qb


def _gather_bias(frow, idx, tq):
    tab = jnp.broadcast_to(frow, (tq, LANES))
    parts = [jnp.take_along_axis(tab, idx[:, c * LANES:(c + 1) * LANES], axis=1)
             for c in range(idx.shape[1] // LANES)]
    return parts[0] if len(parts) == 1 else jnp.concatenate(parts, axis=1)


def _nsacmp_kernel(q_ref, kvc_ref, sm_ref, fv_ref, ov_ref, oc_ref, sel_ref, gs_ref, *, tq, nc):
    qi = pl.program_id(1)
    t0 = qi * tq
    lane = lax.broadcasted_iota(jnp.int32, (1, LANES), 1)
    t_col = t0 + lax.broadcasted_iota(jnp.int32, (tq, 1), 0)
    cmp_end = lax.broadcasted_iota(jnp.int32, (1, nc), 1) * CMP_STRIDE + (CMP_LEN - 1)
    dist = t_col - cmp_end
    masked = jnp.where(dist >= 0, 0.0, NEG)
    any_valid = t_col >= CMP_LEN - 1
    idx = jnp.clip(dist, 0, MAX_DISTANCE - 1)

    gs = jax.nn.sigmoid(sm_ref[...])
    n_sel = SEL_LEN
    j_row = lax.broadcasted_iota(jnp.int32, (n_sel, tq), 0)
    sub_iota = lax.broadcasted_iota(jnp.int32, (SUBLANES, tq), 0)
    t_row = t0 + lax.broadcasted_iota(jnp.int32, (n_sel, tq), 1)
    cur = jnp.right_shift(t_row, SEL_LEN.bit_length() - 1)
    forced = (j_row == 0) | (j_row == cur) | (j_row == cur - 1)
    valid_s = j_row * SEL_LEN <= t_row

    for g in range(N_KV):
        kv = kvc_ref[:, g * LANES:(g + 1) * LANES]
        gs_g = pltpu.roll(gs, LANES - (SMALL_NG + g * GROUP * 3), 1)
        gs_ref[:, g * LANES:(g + 1) * LANES] = gs_g
        psum = jnp.zeros((tq, nc), F32)
        outs = []
        for pair in range(2):
            c0 = (g * 2 + pair) * LANES
            qpair = _head_queries(q_ref[:, c0:c0 + LANES].astype(F32), lane)
            for sub in range(2):
                r = pair * 2 + sub
                h = g * GROUP + r
                s = _dot_nt(qpair[sub], kv) + _gather_bias(fv_ref[h:h + 1, :], idx, tq) + masked
                e = jnp.exp2(s - jnp.max(s, axis=1, keepdims=True))
                p = e * jnp.where(any_valid, 1.0 / jnp.sum(e, axis=1, keepdims=True), 0.0)
                psum = psum + p
                o = _dot(p.astype(BF16), kv)
                outs.append(o * gs_g[:, r * 3:r * 3 + 1])
            oc_ref[:, c0:c0 + LANES] = jnp.where(lane < HALF, pltpu.roll(outs[-2], HALF, 1), outs[-1])

        p_hi = psum.astype(BF16)
        p_lo = (psum - p_hi.astype(F32)).astype(BF16)
        imp = _dot(p_hi, ov_ref[...]) + _dot(p_lo, ov_ref[...])
        imp_t = imp.T[0:n_sel, :]
        val = jnp.where(forced, BIG, jnp.where(valid_s, imp_t, NEG))
        groups = [val[SUBLANES * b:SUBLANES * (b + 1), :] for b in range(n_sel // SUBLANES)]
        counts = [jnp.zeros((SUBLANES, tq), F32) for _ in groups]
        for j in range(n_sel):
            jb, js = divmod(j, SUBLANES)
            row = groups[jb][js:js + 1, :]
            for b, grp in enumerate(groups):
                if b < jb:
                    ahead = row > grp
                elif b > jb:
                    ahead = row >= grp
                else:
                    ahead = (row > grp) | ((row == grp) & (js < sub_iota))
                counts[b] = counts[b] + jnp.where(ahead, 1.0, 0.0)
        cnt = jnp.concatenate(counts, axis=0)
        unchosen = jnp.where(cnt < float(N_SEL), 0.0, NEG)
        unchosen = jnp.concatenate([unchosen, jnp.full((LANES - n_sel, tq), NEG, F32)], axis=0)
        sel_ref[:, g * LANES:(g + 1) * LANES] = unchosen.T.astype(BF16)


def _nsa_compressed(z, kvcmp, small, fvec, overlap, batch, seq):
    tq = TQ_CMP
    nq = seq // tq
    nc = kvcmp.shape[1]
    m = batch * seq
    w3 = N_KV * LANES
    return pl.pallas_call(
        functools.partial(_nsacmp_kernel, tq=tq, nc=nc),
        grid=(batch, nq),
        in_specs=[pl.BlockSpec((tq, N_NSA * HEAD_DIM), lambda b, i: (b * nq + i, ZB_NQ * LANES // (N_NSA * HEAD_DIM))),
                  pl.BlockSpec((None, nc, w3), lambda b, i: (b, 0, 0)),
                  pl.BlockSpec((tq, LANES), lambda b, i: (b * nq + i, 0)),
                  pl.BlockSpec((16, LANES), lambda b, i: (0, 0)),
                  pl.BlockSpec((nc, LANES), lambda b, i: (0, 0))],
        out_specs=[pl.BlockSpec((tq, N_NSA * HEAD_DIM), lambda b, i: (b * nq + i, 0)),
                   pl.BlockSpec((tq, w3), lambda b, i: (b * nq + i, 0)),
                   pl.BlockSpec((tq, w3), lambda b, i: (b * nq + i, 0))],
        out_shape=[jax.ShapeDtypeStruct((m, N_NSA * HEAD_DIM), F32),
                   jax.ShapeDtypeStruct((m, w3), BF16),
                   jax.ShapeDtypeStruct((m, w3), F32)],
        compiler_params=_cparams(("parallel", "parallel")),
        name="nsa_compressed",
    )(z, kvcmp, small, fvec, overlap)


def _nsaflash_kernel(*refs, t, nsub, nkk, windowed, gate_col):
    if windowed:
        qa_ref, qb_ref, kv_ref, gs_ref, tb_ref, o_ref, q4_ref, m_ref, acc_ref = refs
        sel_ref = e_ref = None
    else:
        qa_ref, qb_ref, kv_ref, sel_ref, e_ref, gs_ref, tb_ref, o_ref, q4_ref, m_ref, acc_ref = refs
    g = pl.program_id(1)
    qb = pl.program_id(2)
    kk = pl.program_id(3)
    far = tb_ref.shape[0] - 1
    lane = lax.broadcasted_iota(jnp.int32, (1, LANES), 1)

    @pl.when(kk == 0)
    def _():
        m_ref[...] = jnp.full(m_ref.shape, NEG, F32)
        acc_ref[...] = jnp.zeros_like(acc_ref)
        for i in range(nsub):
            rows = slice(i * t, (i + 1) * t)
            queries = (_head_queries(qa_ref[rows, :].astype(F32), lane)
                       + _head_queries(qb_ref[rows, :].astype(F32), lane))
            for r in range(GROUP):
                q4_ref[(i * GROUP + r) * t:(i * GROUP + r + 1) * t, :] = queries[r]

    def sub_tile(i, j, rel):
        tidx = rel if (windowed or rel <= 1) else far
        kv = kv_ref[j * t:(j + 1) * t, :]
        ones_v = jnp.where(lane < HALF, jnp.ones_like(kv), kv)
        if not windowed:
            unchosen = _dot(sel_ref[i * t:(i + 1) * t, :], e_ref[:, j * t:(j + 1) * t])
        for pair in range(2):
            base = (i * GROUP + 2 * pair) * t
            rows2 = slice(base, base + 2 * t)
            s2 = _dot_nt(q4_ref[rows2, :], kv)
            probs, alphas = [], []
            for sub in range(2):
                h = g * GROUP + 2 * pair + sub
                rows = slice(base + sub * t, base + (sub + 1) * t)
                m_old = m_ref[rows, :]
                s = s2[sub * t:(sub + 1) * t, :] + tb_ref[tidx, h]
                if not windowed:
                    s = s + unchosen
                m_new = jnp.maximum(m_old, jnp.max(s, axis=1, keepdims=True))
                p = jnp.exp2(s - jnp.tile(m_new, (1, t // LANES)))
                m_ref[rows, :] = m_new
                alphas.append(jnp.exp2(m_old - m_new))
                probs.append(p.astype(BF16))
            acc_ref[rows2, :] = (jnp.concatenate(alphas, axis=0) * acc_ref[rows2, :]
                                 + _dot(jnp.concatenate(probs, axis=0), ones_v))

    def block(offset):
        max_rel = far - 1 if windowed else None
        for i in range(nsub):
            for j in range(nsub):
                rel = offset * nsub + i - j
                if rel >= 0 and (max_rel is None or rel <= max_rel):
                    sub_tile(i, j, rel)

    if windowed:
        for c in range(nkk):
            @pl.when((kk == c) & (qb - (nkk - 1) + c >= 0))
            def _(c=c):
                block(nkk - 1 - c)
        last = kk == nkk - 1
    else:
        @pl.when(qb - kk == 0)
        def _():
            block(0)

        @pl.when(qb - kk == 1)
        def _():
            block(1)

        @pl.when(qb - kk >= 2)
        def _():
            block(2)
        last = kk == qb

    @pl.when(last)
    def _():
        for i in range(nsub):
            rows = slice(i * t, (i + 1) * t)
            gs = gs_ref[rows, :]
            o = []
            for r in range(GROUP):
                a = acc_ref[(i * GROUP + r) * t:(i * GROUP + r + 1) * t, :]
                o.append(a / pltpu.roll(a, HALF, 1) * gs[:, r * 3 + gate_col:r * 3 + gate_col + 1])
            o_ref[rows, 0:LANES] = jnp.where(lane < HALF, pltpu.roll(o[0], HALF, 1), o[1])
            o_ref[rows, LANES:2 * LANES] = jnp.where(lane < HALF, pltpu.roll(o[2], HALF, 1), o[3])


def _bias_tile_kernel(fv_ref, o_ref, *, t, nrel):
    rel = pl.program_id(0)
    h = pl.program_id(1)
    dist = rel * t + lax.broadcasted_iota(jnp.int32, (t, t), 0) - lax.broadcasted_iota(jnp.int32, (t, t), 1)
    bias = _gather_bias(fv_ref[pl.ds(h, 1), :], jnp.clip(dist, 0, MAX_DISTANCE - 1), t)
    o_ref[...] = jnp.where((dist >= 0) & ((dist < WINDOW) | (rel == nrel)), bias, NEG)


def _bias_tiles(fvec):
    t = T_NSA
    nrel = WINDOW // t + 1
    return pl.pallas_call(
        functools.partial(_bias_tile_kernel, t=t, nrel=nrel),
        grid=(nrel + 1, N_NSA),
        in_specs=[pl.BlockSpec((16, LANES), lambda r, h: (0, 0))],
        out_specs=pl.BlockSpec((None, None, t, t), lambda r, h: (r, h, 0, 0)),
        out_shape=jax.ShapeDtypeStruct((nrel + 1, N_NSA, t, t), F32),
        compiler_params=_cparams(("parallel", "parallel")),
        name="nsa_bias_tiles",
    )(fvec)


def _nsa_flash(z, sel, emat, gsig, tiles, batch, seq, windowed):
    t, nsub = T_NSA, NSUB_NSA
    tb = t * nsub
    nb = seq // tb
    m = batch * seq
    max_rel = WINDOW // t
    nkk = ((max_rel + nsub - 1) // nsub + 1) if windowed else nb
    assert t >= MAX_DISTANCE and 2 * t <= WINDOW and WINDOW % t == 0 and seq % tb == 0
    zb_kv = ZB_KVW if windowed else ZB_KVS
    if windowed:
        kidx = lambda qb, kk: jnp.maximum(qb - (nkk - 1) + kk, 0)
    else:
        kidx = lambda qb, kk: jnp.minimum(kk, qb)
    in_specs = [pl.BlockSpec((tb, LANES), lambda b, g, qb, kk: (b * nb + qb, ZB_NQ + 2 * g)),
                pl.BlockSpec((tb, LANES), lambda b, g, qb, kk: (b * nb + qb, ZB_NQ + 2 * g + 1)),
                pl.BlockSpec((tb, LANES), lambda b, g, qb, kk: (b * nb + kidx(qb, kk), zb_kv + g))]
    args = [z, z, z]
    if not windowed:
        in_specs += [pl.BlockSpec((tb, LANES), lambda b, g, qb, kk: (b * nb + qb, g)),
                     pl.BlockSpec((LANES, tb), lambda b, g, qb, kk: (0, kidx(qb, kk)))]
        args += [sel, emat]
    in_specs += [pl.BlockSpec((tb, LANES), lambda b, g, qb, kk: (b * nb + qb, g)),
                 pl.BlockSpec(tiles.shape, lambda b, g, qb, kk: (0, 0, 0, 0))]
    args += [gsig, tiles]
    return pl.pallas_call(
        functools.partial(_nsaflash_kernel, t=t, nsub=nsub, nkk=nkk, windowed=windowed,
                          gate_col=2 if windowed else 1),
        grid=(batch, N_KV, nb, nkk),
        in_specs=in_specs,
        out_specs=pl.BlockSpec((tb, 2 * LANES), lambda b, g, qb, kk: (b * nb + qb, g)),
        out_shape=jax.ShapeDtypeStruct((m, N_NSA * HEAD_DIM), F32),
        scratch_shapes=[pltpu.VMEM((nsub * GROUP * t, LANES), BF16), pltpu.VMEM((nsub * GROUP * t, LANES), F32),
                        pltpu.VMEM((nsub * GROUP * t, LANES), F32)],
        compiler_params=_cparams(("parallel", "parallel", "parallel", "arbitrary")),
        name="nsa_window" if windowed else "nsa_selected",
    )(*args)


def _conv_kernel(a_ref, g_ref, ah_ref, gh_ref, w_ref, b_ref, lg_ref, lb_ref, o_ref, u_ref, *, ts):
    first = pl.program_id(1) == 0
    halo = ah_ref[...].astype(F32) * jax.nn.sigmoid(gh_ref[...].astype(F32))
    u_ref[0:HALO, :] = jnp.where(first, 0.0, halo)
    u_ref[HALO:HALO + ts, :] = a_ref[...].astype(F32) * jax.nn.sigmoid(g_ref[...].astype(F32))
    base = HALO - (CONV_WIDTH - 1)
    y = jnp.zeros((ts, D_CONV), F32)
    for k in range(CONV_WIDTH):
        y = y + u_ref[base + k:base + k + ts, :] * w_ref[k:k + 1, :]
    y = y + b_ref[...]
    mu = jnp.mean(y, axis=-1, keepdims=True)
    yc = y - mu
    var = jnp.mean(yc * yc, axis=-1, keepdims=True)
    yn = yc * lax.rsqrt(var + EPS) * lg_ref[...] + lb_ref[...]
    o_ref[...] = (yn * jax.nn.sigmoid(yn)).astype(BF16)


def _conformer_conv(z, w, b, lg, lb, batch, seq):
    ts = TS_CONV
    ns = seq // ts
    m = batch * seq
    cb = D_CONV // LANES
    hb = ts // HALO
    halo_row = lambda bb, s: jnp.maximum((bb * ns + s) * hb - 1, 0)
    return pl.pallas_call(
        functools.partial(_conv_kernel, ts=ts),
        grid=(batch, ns),
        in_specs=[pl.BlockSpec((ts, D_CONV), lambda bb, s: (bb * ns + s, ZB_CA // cb)),
                  pl.BlockSpec((ts, D_CONV), lambda bb, s: (bb * ns + s, ZB_CG // cb)),
                  pl.BlockSpec((HALO, D_CONV), lambda bb, s: (halo_row(bb, s), ZB_CA // cb)),
                  pl.BlockSpec((HALO, D_CONV), lambda bb, s: (halo_row(bb, s), ZB_CG // cb)),
                  pl.BlockSpec((CONV_WIDTH, D_CONV), lambda bb, s: (0, 0)),
                  pl.BlockSpec((1, D_CONV), lambda bb, s: (0, 0)),
                  pl.BlockSpec((1, D_CONV), lambda bb, s: (0, 0)),
                  pl.BlockSpec((1, D_CONV), lambda bb, s: (0, 0))],
        out_specs=pl.BlockSpec((ts, D_CONV), lambda bb, s: (bb * ns + s, 0)),
        out_shape=jax.ShapeDtypeStruct((m, D_CONV), BF16),
        scratch_shapes=[pltpu.VMEM((HALO + ts, D_CONV), F32)],
        compiler_params=_cparams(("parallel", "parallel")),
        name="conformer_conv",
    )(z, z, z, z, w, b, lg, lb)


def _outproj_kernel(x_ref, mod_ref, fox_ref, oc_ref, os_ref, ow_ref, cv_ref, w_ref, o_ref):
    d_fox = N_FOX * HEAD_DIM
    d_nsa = N_NSA * HEAD_DIM
    nsa = (oc_ref[...] + os_ref[...] + ow_ref[...]).astype(BF16)
    acc = _dot(fox_ref[...], w_ref[0:d_fox, :])
    acc = acc + _dot(nsa, w_ref[d_fox:d_fox + d_nsa, :])
    acc = acc + _dot(cv_ref[...], w_ref[d_fox + d_nsa:, :])
    o_ref[...] = x_ref[...] + mod_ref[2:3, :] * acc


def _outproj(x2, mod_l, fox, oc, osel, owin, conv, w, layer, seq):
    m, d = x2.shape
    tm = TM_OUT
    row = lambda i: (i, 0)
    return pl.pallas_call(
        _outproj_kernel,
        grid=(m // tm,),
        in_specs=[pl.BlockSpec((tm, d), row),
                  pl.BlockSpec((None, 6, d), lambda i: (i * tm // seq, 0, 0)),
                  pl.BlockSpec((tm, N_FOX * HEAD_DIM), row),
                  pl.BlockSpec((tm, N_NSA * HEAD_DIM), row),
                  pl.BlockSpec((tm, N_NSA * HEAD_DIM), row),
                  pl.BlockSpec((tm, N_NSA * HEAD_DIM), row),
                  pl.BlockSpec((tm, D_CONV), row),
                  pl.BlockSpec((None, d, d), lambda i: (layer, 0, 0))],
        out_specs=pl.BlockSpec((tm, d), row),
        out_shape=jax.ShapeDtypeStruct((m, d), F32),
        compiler_params=_cparams(("parallel",)),
        name="outproj",
    )(x2, mod_l, fox, oc, osel, owin, conv, w)


def _mlp_kernel(x_ref, mod_ref, g_ref, w1_ref, w2_ref, fg_ref, o_ref, h_ref, *, tm, final):
    f = pl.program_id(1)

    @pl.when(f == 0)
    def _():
        def chunk(c, carry):
            r = pl.multiple_of(c * ROW_CHUNK, ROW_CHUNK)
            h = _norm_mod(x_ref[pl.ds(r, ROW_CHUNK), :], g_ref[...], mod_ref[4:5, :], mod_ref[3:4, :])
            h_ref[pl.ds(r, ROW_CHUNK), :] = h.astype(BF16)
            return carry
        lax.fori_loop(0, tm // ROW_CHUNK, chunk, 0)
        o_ref[...] = jnp.zeros_like(o_ref)

    a = jnp.maximum(_dot(h_ref[...], w1_ref[...]), 0.0)
    o_ref[...] += _dot((a * a).astype(BF16), w2_ref[...])

    @pl.when(f == pl.num_programs(1) - 1)
    def _():
        def chunk(c, carry):
            r = pl.multiple_of(c * ROW_CHUNK, ROW_CHUNK)
            y = x_ref[pl.ds(r, ROW_CHUNK), :] + mod_ref[5:6, :] * o_ref[pl.ds(r, ROW_CHUNK), :]
            if final:
                y = y * lax.rsqrt(jnp.mean(y * y, axis=-1, keepdims=True) + EPS) * fg_ref[...]
            o_ref[pl.ds(r, ROW_CHUNK), :] = y
            return carry
        lax.fori_loop(0, tm // ROW_CHUNK, chunk, 0)


def _mlp(x2, mod_l, g, w1, w2, fg, layer, seq, final):
    m, d = x2.shape
    tm, tf = TM_MLP, TF_MLP
    dff = w1.shape[2]
    return pl.pallas_call(
        functools.partial(_mlp_kernel, tm=tm, final=final),
        grid=(m // tm, dff // tf),
        in_specs=[pl.BlockSpec((tm, d), lambda i, f: (i, 0), pipeline_mode=pl.Buffered(1)),
                  pl.BlockSpec((None, 6, d), lambda i, f: (i * tm // seq, 0, 0)),
                  pl.BlockSpec((1, d), lambda i, f: (0, 0)),
                  pl.BlockSpec((None, d, tf), lambda i, f: (layer, 0, f)),
                  pl.BlockSpec((None, tf, d), lambda i, f: (layer, f, 0)),
                  pl.BlockSpec((1, d), lambda i, f: (0, 0))],
        out_specs=pl.BlockSpec((tm, d), lambda i, f: (i, 0)),
        out_shape=jax.ShapeDtypeStruct((m, d), F32),
        scratch_shapes=[pltpu.VMEM((tm, d), BF16)],
        compiler_params=_cparams(("parallel", "arbitrary")),
        name="mlp_final" if final else "mlp",
    )(x2, mod_l, g, w1, w2, fg)


def _prep_inproj_weights(w_in):
    depth, d, _ = w_in.shape
    def sl(name, width, off=0, scale=None):
        piece = w_in[:, :, _SRC[name] + off:_SRC[name] + off + width]
        return (piece if scale is None else piece * scale).astype(BF16)
    qs = HEAD_DIM ** -0.5 * LOG2E
    pieces = [sl("fq", 768, scale=qs), sl("fk", 768), sl("fv", 768), sl("nq", 768, scale=qs),
              sl("ca", D_CONV), sl("cg", D_CONV)]
    for kname, vname in (("kc", "vc"), ("ks", "vs"), ("kw", "vw")):
        for g in range(N_KV):
            pieces += [sl(kname, HEAD_DIM, g * HEAD_DIM), sl(vname, HEAD_DIM, g * HEAD_DIM)]
    pieces.append(jnp.zeros((depth, d, LANES), BF16))
    wz = jnp.concatenate(pieces, axis=2)
    assert wz.shape[2] == NZ
    ws = jnp.concatenate([sl("ff", N_FOX), sl("ng", 3 * N_NSA),
                          jnp.zeros((depth, d, LANES - N_FOX - 3 * N_NSA), BF16)], axis=2)
    return wz, ws


def _prep_compress_weights(w_cmp_k, w_cmp_v, pos_cmp):
    depth = w_cmp_k.shape[0]
    slots = 2
    wkv = jnp.stack([w_cmp_k, w_cmp_v], axis=1)
    big = jnp.einsum("zklde,kq->zlkdqe", wkv, jnp.eye(2, dtype=F32))
    big = big.reshape(depth, CMP_LEN, slots * HEAD_DIM, slots * HEAD_DIM)
    half = CMP_LEN // 2
    kdim = half * slots * HEAD_DIM
    wlo = big[:, :half].reshape(depth, kdim, slots * HEAD_DIM).astype(BF16)
    whi = big[:, half:].reshape(depth, kdim, slots * HEAD_DIM).astype(BF16)
    pos = jnp.broadcast_to(pos_cmp[:, :, None, :], (depth, CMP_LEN, slots, HEAD_DIM))
    plo = jnp.zeros((depth, 8, kdim), F32).at[:, 0].set(pos[:, :half].reshape(depth, kdim)).astype(BF16)
    phi = jnp.zeros((depth, 8, kdim), F32).at[:, 0].set(pos[:, half:].reshape(depth, kdim)).astype(BF16)
    return wlo, whi, plo, phi


def _selection_constants(seq):
    nc = seq // CMP_STRIDE
    n_cmp = (seq - CMP_LEN) // CMP_STRIDE + 1
    n = np.arange(nc)[:, None]
    j = np.arange(LANES)[None, :]
    overlap = ((n * CMP_STRIDE < j * SEL_LEN + SEL_LEN) & (n * CMP_STRIDE + CMP_LEN - 1 >= j * SEL_LEN)
               & (n < n_cmp) & (j < seq // SEL_LEN))
    emat = (np.arange(seq)[None, :] // SEL_LEN) == np.arange(LANES)[:, None]
    return jnp.asarray(overlap, BF16), jnp.asarray(emat, BF16)


def kernel(x, c, w_mod, b_mod, norm1_g, w_in, b_f, w_cmp_k, w_cmp_v, pos_cmp, conv_w, conv_b, conv_ln_g,
           conv_ln_b, w_out, norm2_g, w_mlp1, w_mlp2, rel_bias, final_g):
    batch, seq, d = x.shape
    depth = w_in.shape[0]
    assert d == D_MODEL and seq % TM_INPROJ == 0 and seq // SEL_LEN <= SEL_LEN

    wz, ws = _prep_inproj_weights(w_in)
    wlo, whi, plo, phi = _prep_compress_weights(w_cmp_k, w_cmp_v, pos_cmp)
    w_out_b = w_out.astype(BF16)
    w1_b = w_mlp1.astype(BF16)
    w2_b = w_mlp2.astype(BF16)
    overlap, emat = _selection_constants(seq)
    bf_rows = jnp.zeros((depth, 1, LANES), F32).at[:, 0, SMALL_FF:SMALL_FF + N_FOX].set(b_f)

    mod = _modulation(c, w_mod, b_mod)
    fvec = _bias_by_distance(rel_bias)
    tiles = _bias_tiles(fvec)
    fg = final_g.reshape(1, d)

    x2 = x.reshape(batch * seq, d)
    kv0 = ZB_KVC * LANES
    for l in range(depth):
        z, small = _inproj(x2, mod[l], norm1_g[l].reshape(1, d), wz, ws, l, seq)
        fcol, frow = _fox_cumsum(small, bf_rows[l], batch, seq)
        o_fox = _fox_attention(z, fcol, frow, batch, seq)
        r = z[:, kv0:kv0 + N_KV * LANES].reshape(batch, seq // CMP_STRIDE, CMP_STRIDE, N_KV, LANES)
        r = r.transpose(0, 3, 1, 2, 4).reshape(batch, N_KV, seq // CMP_STRIDE, CMP_STRIDE * LANES)
        kvcmp = _compress(r, wlo[l], whi[l], plo[l], phi[l])
        o_cmp, sel, gsig = _nsa_compressed(z, kvcmp, small, fvec, overlap, batch, seq)
        o_sel = _nsa_flash(z, sel, emat, gsig, tiles, batch, seq, windowed=False)
        o_win = _nsa_flash(z, sel, emat, gsig, tiles, batch, seq, windowed=True)
        o_conv = _conformer_conv(z, conv_w[l], conv_b[l].reshape(1, -1), conv_ln_g[l].reshape(1, -1),
                                 conv_ln_b[l].reshape(1, -1), batch, seq)
        x2 = _outproj(x2, mod[l], o_fox, o_cmp, o_sel, o_win, o_conv, w_out_b, l, seq)
        x2 = _mlp(x2, mod[l], norm2_g[l].reshape(1, d), w1_b, w2_b, fg, l, seq, final=(l == depth - 1))
    return x2.reshape(batch, seq, d)
```

```python
import functools
import math

import numpy as np
import jax
import jax.numpy as jnp
from jax import lax
from jax.experimental import pallas as pl
from jax.experimental.pallas import tpu as pltpu

F32 = jnp.float32
BF16 = jnp.bfloat16
HIGHEST = lax.Precision.HIGHEST

D_MODEL = 2048
HEAD_DIM = 64
N_FOX = 12
N_NSA = 12
N_KV = 3
GROUP = 4
D_CONV = 512
D_FF = 4 * D_MODEL
CMP_LEN = 32
CMP_STRIDE = 16
SEL_LEN = 64
N_SEL = 16
WINDOW = 512
CONV_WIDTH = 31
N_BUCKETS = 32
MAX_DISTANCE = 128
EPS = 1e-6
NEG = -1e30
BIG = 1e9
LOG2E = math.log2(math.e)

LANES = 128
SUBLANES = 8
HALF = LANES // 2

ZB_FQ, ZB_FK, ZB_FV, ZB_NQ = 0, 6, 12, 18
ZB_CA, ZB_CG = 24, 28
ZB_KVC, ZB_KVS, ZB_KVW = 32, 35, 38
NZ_BLOCKS = 42
NZ = NZ_BLOCKS * LANES
SMALL_FF, SMALL_NG = 0, 12

_SRC = dict(fq=0, fk=768, fv=1536, ff=2304, nq=2316, kc=3084, vc=3276, ks=3468, vs=3660,
            kw=3852, vw=4044, ng=4236, ca=4272, cg=4784)

TM_INPROJ = 1024
TN_INPROJ = 768
TM_OUT = 512
TM_MLP = 1024
TF_MLP = 512
ROW_CHUNK = 128
T_FOX = 512
NSUB_FOX = 4
T_NSA = 256
NSUB_NSA = 4
TQ_CMP = 512
TS_CONV = 512
CONV_ROWS = 64
TS_CUM = 512
HALO = 32
VMEM_LIMIT = 56 * 1024 * 1024


def _cparams(sem):
    return pltpu.CompilerParams(dimension_semantics=sem, vmem_limit_bytes=VMEM_LIMIT)


def _dot(a, b):
    return jnp.dot(a, b, preferred_element_type=F32)


def _dot_nt(a, b):
    return lax.dot_general(a, b, (((1,), (1,)), ((), ())), preferred_element_type=F32)


def _bucket_thresholds():
    d = np.arange(MAX_DISTANCE, dtype=np.int32)
    max_exact = N_BUCKETS // 2
    nf = np.maximum(d, 1).astype(np.float32)
    large = max_exact + (np.log(nf / np.float32(max_exact)) / np.float32(math.log(MAX_DISTANCE / max_exact))
                         * np.float32(N_BUCKETS - max_exact)).astype(np.int32)
    large = np.minimum(large, N_BUCKETS - 1)
    bucket = np.where(d < max_exact, d, large)
    assert bucket[-1] == N_BUCKETS - 1 and np.all(np.diff(bucket) >= 0)
    return [int(np.argmax(bucket >= k)) for k in range(N_BUCKETS)]


_T5_THRESH = _bucket_thresholds()


def _fvec_kernel(tbl_ref, out_ref):
    d = lax.broadcasted_iota(jnp.int32, out_ref.shape, 1)
    out = jnp.broadcast_to(tbl_ref[:, 0:1], out_ref.shape)
    for k in range(1, N_BUCKETS):
        out = jnp.where(d >= _T5_THRESH[k], tbl_ref[:, k:k + 1], out)
    out_ref[...] = out * LOG2E


def _bias_by_distance(rel_bias):
    tbl = jnp.zeros((16, LANES), F32).at[:N_NSA, :N_BUCKETS].set(rel_bias.astype(F32).T)
    return pl.pallas_call(
        _fvec_kernel, out_shape=jax.ShapeDtypeStruct((16, LANES), F32), name="t5_bias_by_distance",
    )(tbl)


def _mod_kernel(ct_ref, w_ref, b_ref, o_ref, *, nb):
    c = ct_ref[...]
    ca = c * jax.nn.sigmoid(c)
    w = w_ref[...]
    rows = [jnp.sum(w * ca[:, b:b + 1], axis=0, keepdims=True) for b in range(nb)]
    rows.append(jnp.zeros((SUBLANES - nb, w.shape[1]), F32))
    o_ref[...] = jnp.concatenate(rows, axis=0) + b_ref[...]


def _modulation(c, w_mod, b_mod):
    depth, d, n = w_mod.shape
    b = c.shape[0]
    assert b < SUBLANES
    tn = 1024
    ct = jnp.zeros((d, LANES), F32).at[:, :b].set(c.T)
    out = pl.pallas_call(
        functools.partial(_mod_kernel, nb=b),
        grid=(depth, n // tn),
        in_specs=[pl.BlockSpec((d, LANES), lambda l, j: (0, 0)),
                  pl.BlockSpec((None, d, tn), lambda l, j: (l, 0, j)),
                  pl.BlockSpec((None, 1, tn), lambda l, j: (l, 0, j))],
        out_specs=pl.BlockSpec((None, SUBLANES, tn), lambda l, j: (l, 0, j)),
        out_shape=jax.ShapeDtypeStruct((depth, SUBLANES, n), F32),
        compiler_params=_cparams(("parallel", "parallel")),
        name="adaln_modulation",
    )(ct, w_mod, b_mod.reshape(depth, 1, n))
    return out[:, :b].reshape(depth, b, 6, d)


def _norm_mod(x, g, sc, sh):
    y = x * lax.rsqrt(jnp.mean(x * x, axis=-1, keepdims=True) + EPS) * g
    return y * (1.0 + sc) + sh


def _inproj_kernel(x_ref, mod_ref, g_ref, w_ref, ws_ref, z_ref, sm_ref, h_ref, *, tm):
    @pl.when(pl.program_id(1) == 0)
    def _():
        def chunk(c, carry):
            r = pl.multiple_of(c * ROW_CHUNK, ROW_CHUNK)
            h = _norm_mod(x_ref[pl.ds(r, ROW_CHUNK), :], g_ref[...], mod_ref[1:2, :], mod_ref[0:1, :])
            hb = h.astype(BF16)
            h_ref[pl.ds(r, ROW_CHUNK), :] = hb
            sm_ref[pl.ds(r, ROW_CHUNK), :] = _dot(hb, ws_ref[...])
            return carry
        lax.fori_loop(0, tm // ROW_CHUNK, chunk, 0)

    z_ref[...] = _dot(h_ref[...], w_ref[...]).astype(BF16)


def _inproj(x2, mod_l, g, wz, ws, layer, seq):
    m, d = x2.shape
    tm, tn = TM_INPROJ, TN_INPROJ
    assert seq % tm == 0 and NZ % tn == 0
    return pl.pallas_call(
        functools.partial(_inproj_kernel, tm=tm),
        grid=(m // tm, NZ // tn),
        in_specs=[pl.BlockSpec((tm, d), lambda i, j: (i, 0)),
                  pl.BlockSpec((None, 6, d), lambda i, j: (i * tm // seq, 0, 0)),
                  pl.BlockSpec((1, d), lambda i, j: (0, 0)),
                  pl.BlockSpec((None, d, tn), lambda i, j: (layer, 0, j)),
                  pl.BlockSpec((None, d, LANES), lambda i, j: (layer, 0, 0))],
        out_specs=[pl.BlockSpec((tm, tn), lambda i, j: (i, j)),
                   pl.BlockSpec((tm, LANES), lambda i, j: (i, 0))],
        out_shape=[jax.ShapeDtypeStruct((m, NZ), BF16), jax.ShapeDtypeStruct((m, LANES), F32)],
        scratch_shapes=[pltpu.VMEM((tm, d), BF16)],
        compiler_params=_cparams(("parallel", "arbitrary")),
        name="inproj",
    )(x2, mod_l, g, wz, ws)


def _foxcum_kernel(sm_ref, bf_ref, fcol_ref, frow_ref, carry_ref, *, ts):
    @pl.when(pl.program_id(1) == 0)
    def _():
        carry_ref[...] = jnp.zeros_like(carry_ref)

    x = sm_ref[...] + bf_ref[...]
    ls = (jnp.minimum(x, 0.0) - jnp.log1p(jnp.exp(-jnp.abs(x)))) * LOG2E
    r = lax.broadcasted_iota(jnp.int32, (ts, ts), 0)
    c = lax.broadcasted_iota(jnp.int32, (ts, ts), 1)
    tri = jnp.where(r >= c, 1.0, 0.0).astype(F32)
    cum = jnp.dot(tri, ls, precision=HIGHEST, preferred_element_type=F32) + carry_ref[0:1, :]
    carry_ref[0:1, :] = cum[ts - 1:ts, :]
    cum_t = cum.T
    for hp in range(N_FOX // 2):
        fcol_ref[hp] = cum if hp == 0 else pltpu.roll(cum, LANES - 2 * hp, 1)
        frow_ref[hp, 0:2, :] = cum_t[2 * hp:2 * hp + 2, :]
        frow_ref[hp, 2:8, :] = jnp.zeros((6, ts), F32)


def _fox_cumsum(small, bf_row, batch, seq):
    ts = TS_CUM
    ns = seq // ts
    npair = N_FOX // 2
    return pl.pallas_call(
        functools.partial(_foxcum_kernel, ts=ts),
        grid=(batch, ns),
        in_specs=[pl.BlockSpec((ts, LANES), lambda b, s: (b * ns + s, 0)),
                  pl.BlockSpec((1, LANES), lambda b, s: (0, 0))],
        out_specs=[pl.BlockSpec((None, npair, ts, LANES), lambda b, s: (b, 0, s, 0)),
                   pl.BlockSpec((None, npair, 8, ts), lambda b, s: (b, 0, 0, s))],
        out_shape=[jax.ShapeDtypeStruct((batch, npair, seq, LANES), F32),
                   jax.ShapeDtypeStruct((batch, npair, 8, seq), F32)],
        scratch_shapes=[pltpu.VMEM((8, LANES), F32)],
        compiler_params=_cparams(("parallel", "arbitrary")),
        name="fox_cumsum",
    )(small, bf_row)


def _fox_kernel(q_ref, k_ref, v_ref, fc_ref, fr_ref, o_ref, m_ref, fq_ref, acc_ref, *, t, nsub):
    qb = pl.program_id(2)
    kb = pl.program_id(3)
    tb = nsub * t

    @pl.when(kb == 0)
    def _():
        m_ref[...] = jnp.full(m_ref.shape, NEG, F32)
        acc_ref[...] = jnp.zeros_like(acc_ref)
        for h in range(2):
            fq_ref[h] = jnp.broadcast_to(fc_ref[:, h:h + 1], (tb, LANES))

    lane = lax.broadcasted_iota(jnp.int32, (1, LANES), 1)

    def sub_tile(i, j, diagonal):
        rows = slice(i * t, (i + 1) * t)
        cols = slice(j * t, (j + 1) * t)
        q2 = q_ref[rows, :]
        k2 = k_ref[cols, :]
        v2 = v_ref[cols, :]
        if diagonal:
            causal = (lax.broadcasted_iota(jnp.int32, (t, t), 1) <= lax.broadcasted_iota(jnp.int32, (t, t), 0))
        for h in range(2):
            keep = (lane < HALF) if h == 0 else (lane >= HALF)
            qh = jnp.where(keep, q2, jnp.zeros_like(q2))
            vh = jnp.where(keep, v2, jnp.ones_like(v2))
            s = _dot_nt(qh, k2) - fr_ref[h:h + 1, cols]
            if diagonal:
                s = jnp.where(causal, s, NEG)
            fq = fq_ref[h, rows, :]
            m_old = m_ref[h, rows, :]
            m_new = jnp.maximum(m_old, jnp.max(s, axis=1, keepdims=True) + fq)
            shift = jnp.tile(fq - m_new, (1, t // LANES))
            p = jnp.exp2(s + shift)
            acc_ref[h, rows, :] = jnp.exp2(m_old - m_new) * acc_ref[h, rows, :] + _dot(p.astype(BF16), vh)
            m_ref[h, rows, :] = m_new

    @pl.when(kb < qb)
    def _():
        for i in range(nsub):
            for j in range(nsub):
                sub_tile(i, j, False)

    @pl.when(kb == qb)
    def _():
        for i in range(nsub):
            for j in range(i + 1):
                sub_tile(i, j, i == j)
        a0 = acc_ref[0]
        a1 = acc_ref[1]
        o_ref[...] = jnp.where(lane < HALF, a0 / pltpu.roll(a0, HALF, 1),
                               a1 / pltpu.roll(a1, HALF, 1)).astype(BF16)


def _fox_attention(z, fcol, frow, batch, seq):
    t, nsub = T_FOX, NSUB_FOX
    tb = t * nsub
    nb = seq // tb
    npair = N_FOX // 2
    m = batch * seq
    kv_row = lambda b, hp, qb, kb: b * nb + jnp.minimum(kb, qb)
    return pl.pallas_call(
        functools.partial(_fox_kernel, t=t, nsub=nsub),
        grid=(batch, npair, nb, nb),
        in_specs=[pl.BlockSpec((tb, LANES), lambda b, hp, qb, kb: (b * nb + qb, ZB_FQ + hp)),
                  pl.BlockSpec((tb, LANES), lambda b, hp, qb, kb: (kv_row(b, hp, qb, kb), ZB_FK + hp)),
                  pl.BlockSpec((tb, LANES), lambda b, hp, qb, kb: (kv_row(b, hp, qb, kb), ZB_FV + hp)),
                  pl.BlockSpec((None, None, tb, LANES), lambda b, hp, qb, kb: (b, hp, qb, 0)),
                  pl.BlockSpec((None, None, 8, tb), lambda b, hp, qb, kb: (b, hp, 0, jnp.minimum(kb, qb)))],
        out_specs=pl.BlockSpec((tb, LANES), lambda b, hp, qb, kb: (b * nb + qb, hp)),
        out_shape=jax.ShapeDtypeStruct((m, N_FOX * HEAD_DIM), BF16),
        scratch_shapes=[pltpu.VMEM((2, tb, LANES), F32), pltpu.VMEM((2, tb, LANES), F32),
                        pltpu.VMEM((2, tb, LANES), F32)],
        compiler_params=_cparams(("parallel", "parallel", "parallel", "arbitrary")),
        name="fox_attention",
    )(z, z, z, fcol, frow)


def _compress_kernel(r_ref, wlo_ref, whi_ref, plo_ref, phi_ref, o_ref, *, nc):
    bias = _dot(plo_ref[...], wlo_ref[...]) + _dot(phi_ref[...], whi_ref[...])
    for g in range(N_KV):
        r = r_ref[g]
        out = _dot(r, wlo_ref[...]) + pltpu.roll(_dot(r, whi_ref[...]), nc - 1, 0) + bias[0:1, :]
        o_ref[:, g * LANES:(g + 1) * LANES] = out.astype(BF16)


def _compress(r, wlo, whi, plo, phi):
    batch, ng, nc, k = r.shape
    return pl.pallas_call(
        functools.partial(_compress_kernel, nc=nc),
        grid=(batch,),
        in_specs=[pl.BlockSpec((None, ng, nc, k), lambda b: (b, 0, 0, 0)),
                  pl.BlockSpec((k, LANES), lambda b: (0, 0)),
                  pl.BlockSpec((k, LANES), lambda b: (0, 0)),
                  pl.BlockSpec((8, k), lambda b: (0, 0)),
                  pl.BlockSpec((8, k), lambda b: (0, 0))],
        out_specs=pl.BlockSpec((None, nc, ng * LANES), lambda b: (b, 0, 0)),
        out_shape=jax.ShapeDtypeStruct((batch, nc, ng * LANES), BF16),
        compiler_params=_cparams(("parallel",)),
        name="nsa_compress",
    )(r, wlo, whi, plo, phi)


def _head_queries(qf, lane):
    zero = jnp.zeros_like(qf)
    qa = jnp.where(lane < HALF, qf, zero).astype(BF16)
    qb = jnp.where(lane < HALF, pltpu.roll(qf, HALF, 1), zero).astype(BF16)
    return qa, qb


def _gather_bias(frow, idx, tq):
    tab = jnp.broadcast_to(frow, (tq, LANES))
    parts = [jnp.take_along_axis(tab, idx[:, c * LANES:(c + 1) * LANES], axis=1)
             for c in range(idx.shape[1] // LANES)]
    return parts[0] if len(parts) == 1 else jnp.concatenate(parts, axis=1)


def _nsacmp_kernel(q_ref, kvc_ref, sm_ref, fv_ref, ov_ref, oc_ref, sel_ref, gs_ref, *, tq, nc):
    qi = pl.program_id(1)
    t0 = qi * tq
    lane = lax.broadcasted_iota(jnp.int32, (1, LANES), 1)
    t_col = t0 + lax.broadcasted_iota(jnp.int32, (tq, 1), 0)
    cmp_end = lax.broadcasted_iota(jnp.int32, (1, nc), 1) * CMP_STRIDE + (CMP_LEN - 1)
    dist = t_col - cmp_end
    masked = jnp.where(dist >= 0, 0.0, NEG)
    any_valid = t_col >= CMP_LEN - 1
    idx = jnp.clip(dist, 0, MAX_DISTANCE - 1)

    gs = jax.nn.sigmoid(sm_ref[...])
    n_sel = SEL_LEN
    j_row = lax.broadcasted_iota(jnp.int32, (n_sel, tq), 0)
    sub_iota = lax.broadcasted_iota(jnp.int32, (SUBLANES, tq), 0)
    t_row = t0 + lax.broadcasted_iota(jnp.int32, (n_sel, tq), 1)
    cur = jnp.right_shift(t_row, SEL_LEN.bit_length() - 1)
    forced = (j_row == 0) | (j_row == cur) | (j_row == cur - 1)
    valid_s = j_row * SEL_LEN <= t_row

    for g in range(N_KV):
        kv = kvc_ref[:, g * LANES:(g + 1) * LANES]
        gs_g = pltpu.roll(gs, LANES - (SMALL_NG + g * GROUP * 3), 1)
        gs_ref[:, g * LANES:(g + 1) * LANES] = gs_g
        psum = jnp.zeros((tq, nc), F32)
        outs = []
        for pair in range(2):
            c0 = (g * 2 + pair) * LANES
            qpair = _head_queries(q_ref[:, c0:c0 + LANES].astype(F32), lane)
            for sub in range(2):
                r = pair * 2 + sub
                h = g * GROUP + r
                s = _dot_nt(qpair[sub], kv) + _gather_bias(fv_ref[h:h + 1, :], idx, tq) + masked
                e = jnp.exp2(s - jnp.max(s, axis=1, keepdims=True))
                p = e * jnp.where(any_valid, 1.0 / jnp.sum(e, axis=1, keepdims=True), 0.0)
                psum = psum + p
                o = _dot(p.astype(BF16), kv)
                outs.append(o * gs_g[:, r * 3:r * 3 + 1])
            oc_ref[:, c0:c0 + LANES] = jnp.where(lane < HALF, pltpu.roll(outs[-2], HALF, 1), outs[-1])

        p_hi = psum.astype(BF16)
        p_lo = (psum - p_hi.astype(F32)).astype(BF16)
        imp = _dot(p_hi, ov_ref[...]) + _dot(p_lo, ov_ref[...])
        imp_t = imp.T[0:n_sel, :]
        val = jnp.where(forced, BIG, jnp.where(valid_s, imp_t, NEG))
        groups = [val[SUBLANES * b:SUBLANES * (b + 1), :] for b in range(n_sel // SUBLANES)]
        counts = [jnp.zeros((SUBLANES, tq), F32) for _ in groups]
        for j in range(n_sel):
            jb, js = divmod(j, SUBLANES)
            row = groups[jb][js:js + 1, :]
            for b, grp in enumerate(groups):
                if b < jb:
                    ahead = row > grp
                elif b > jb:
                    ahead = row >= grp
                else:
                    ahead = (row > grp) | ((row == grp) & (js < sub_iota))
                counts[b] = counts[b] + jnp.where(ahead, 1.0, 0.0)
        cnt = jnp.concatenate(counts, axis=0)
        unchosen = jnp.where(cnt < float(N_SEL), 0.0, NEG)
        unchosen = jnp.concatenate([jnp.zeros((LANES - n_sel, tq), F32), unchosen], axis=0)
        sel_ref[:, g * LANES:(g + 1) * LANES] = unchosen.T.astype(BF16)


def _nsa_compressed(z, kvcmp, small, fvec, overlap, batch, seq):
    tq = TQ_CMP
    nq = seq // tq
    nc = kvcmp.shape[1]
    m = batch * seq
    w3 = N_KV * LANES
    return pl.pallas_call(
        functools.partial(_nsacmp_kernel, tq=tq, nc=nc),
        grid=(batch, nq),
        in_specs=[pl.BlockSpec((tq, N_NSA * HEAD_DIM), lambda b, i: (b * nq + i, ZB_NQ * LANES // (N_NSA * HEAD_DIM))),
                  pl.BlockSpec((None, nc, w3), lambda b, i: (b, 0, 0)),
                  pl.BlockSpec((tq, LANES), lambda b, i: (b * nq + i, 0)),
                  pl.BlockSpec((16, LANES), lambda b, i: (0, 0)),
                  pl.BlockSpec((nc, LANES), lambda b, i: (0, 0))],
        out_specs=[pl.BlockSpec((tq, N_NSA * HEAD_DIM), lambda b, i: (b * nq + i, 0)),
                   pl.BlockSpec((tq, w3), lambda b, i: (b * nq + i, 0)),
                   pl.BlockSpec((tq, w3), lambda b, i: (b * nq + i, 0))],
        out_shape=[jax.ShapeDtypeStruct((m, N_NSA * HEAD_DIM), F32),
                   jax.ShapeDtypeStruct((m, w3), BF16),
                   jax.ShapeDtypeStruct((m, w3), F32)],
        compiler_params=_cparams(("parallel", "parallel")),
        name="nsa_compressed",
    )(z, kvcmp, small, fvec, overlap)


def _nsaflash_kernel(*refs, t, nsub, nkk, windowed, gate_col):
    if windowed:
        qa_ref, qb_ref, kv_ref, gs_ref, tb_ref, o_ref, q4_ref, m_ref, acc_ref = refs
        sel_ref = e_ref = None
    else:
        qa_ref, qb_ref, kv_ref, sel_ref, e_ref, gs_ref, tb_ref, o_ref, q4_ref, m_ref, acc_ref = refs
    g = pl.program_id(1)
    qb = pl.program_id(2)
    kk = pl.program_id(3)
    far = tb_ref.shape[0] - 1
    lane = lax.broadcasted_iota(jnp.int32, (1, LANES), 1)

    @pl.when(kk == 0)
    def _():
        m_ref[...] = jnp.full(m_ref.shape, NEG, F32)
        acc_ref[...] = jnp.zeros_like(acc_ref)
        for i in range(nsub):
            rows = slice(i * t, (i + 1) * t)
            queries = (_head_queries(qa_ref[rows, :].astype(F32), lane)
                       + _head_queries(qb_ref[rows, :].astype(F32), lane))
            for r in range(GROUP):
                q = queries[r] if windowed else jnp.where(lane < HALF, queries[r], sel_ref[rows, :])
                q4_ref[(i * GROUP + r) * t:(i * GROUP + r + 1) * t, :] = q

    def sub_tile(i, j, rel):
        tidx = rel if (windowed or rel <= 1) else far
        kv = kv_ref[j * t:(j + 1) * t, :]
        ones_v = jnp.where(lane < HALF, jnp.ones_like(kv), kv)
        keys = kv if windowed else jnp.where(lane < HALF, kv, e_ref[j * t:(j + 1) * t, :])
        for pair in range(2):
            base = (i * GROUP + 2 * pair) * t
            rows2 = slice(base, base + 2 * t)
            s2 = _dot_nt(q4_ref[rows2, :], keys)
            probs, alphas = [], []
            for sub in range(2):
                h = g * GROUP + 2 * pair + sub
                rows = slice(base + sub * t, base + (sub + 1) * t)
                m_old = m_ref[rows, :]
                s = s2[sub * t:(sub + 1) * t, :] + tb_ref[tidx, h]
                m_new = jnp.maximum(m_old, jnp.max(s, axis=1, keepdims=True))
                p = jnp.exp2(s - jnp.tile(m_new, (1, t // LANES)))
                m_ref[rows, :] = m_new
                alphas.append(jnp.exp2(m_old - m_new))
                probs.append(p.astype(BF16))
            acc_ref[rows2, :] = (jnp.concatenate(alphas, axis=0) * acc_ref[rows2, :]
                                 + _dot(jnp.concatenate(probs, axis=0), ones_v))

    def block(offset):
        max_rel = far - 1 if windowed else None
        for j in range(nsub):
            for i in range(nsub):
                rel = offset * nsub + i - j
                if rel >= 0 and (max_rel is None or rel <= max_rel):
                    sub_tile(i, j, rel)

    if windowed:
        for c in range(nkk):
            @pl.when((kk == c) & (qb - (nkk - 1) + c >= 0))
            def _(c=c):
                block(nkk - 1 - c)
        last = kk == nkk - 1
    else:
        @pl.when(qb - kk == 0)
        def _():
            block(0)

        @pl.when(qb - kk == 1)
        def _():
            block(1)

        @pl.when(qb - kk >= 2)
        def _():
            block(2)
        last = kk == qb

    @pl.when(last)
    def _():
        for i in range(nsub):
            rows = slice(i * t, (i + 1) * t)
            gs = gs_ref[rows, :]
            o = []
            for r in range(GROUP):
                a = acc_ref[(i * GROUP + r) * t:(i * GROUP + r + 1) * t, :]
                o.append(a / pltpu.roll(a, HALF, 1) * gs[:, r * 3 + gate_col:r * 3 + gate_col + 1])
            o_ref[rows, 0:LANES] = jnp.where(lane < HALF, pltpu.roll(o[0], HALF, 1), o[1])
            o_ref[rows, LANES:2 * LANES] = jnp.where(lane < HALF, pltpu.roll(o[2], HALF, 1), o[3])


def _bias_tile_kernel(fv_ref, o_ref, *, t, nrel):
    rel = pl.program_id(0)
    h = pl.program_id(1)
    dist = rel * t + lax.broadcasted_iota(jnp.int32, (t, t), 0) - lax.broadcasted_iota(jnp.int32, (t, t), 1)
    bias = _gather_bias(fv_ref[pl.ds(h, 1), :], jnp.clip(dist, 0, MAX_DISTANCE - 1), t)
    o_ref[...] = jnp.where((dist >= 0) & ((dist < WINDOW) | (rel == nrel)), bias, NEG)


def _bias_tiles(fvec):
    t = T_NSA
    nrel = WINDOW // t + 1
    return pl.pallas_call(
        functools.partial(_bias_tile_kernel, t=t, nrel=nrel),
        grid=(nrel + 1, N_NSA),
        in_specs=[pl.BlockSpec((16, LANES), lambda r, h: (0, 0))],
        out_specs=pl.BlockSpec((None, None, t, t), lambda r, h: (r, h, 0, 0)),
        out_shape=jax.ShapeDtypeStruct((nrel + 1, N_NSA, t, t), F32),
        compiler_params=_cparams(("parallel", "parallel")),
        name="nsa_bias_tiles",
    )(fvec)


def _nsa_flash(z, sel, emat, gsig, tiles, batch, seq, windowed):
    t, nsub = T_NSA, NSUB_NSA
    tb = t * nsub
    nb = seq // tb
    m = batch * seq
    max_rel = WINDOW // t
    nkk = ((max_rel + nsub - 1) // nsub + 1) if windowed else nb
    assert t >= MAX_DISTANCE and 2 * t <= WINDOW and WINDOW % t == 0 and seq % tb == 0
    zb_kv = ZB_KVW if windowed else ZB_KVS
    if windowed:
        kidx = lambda qb, kk: jnp.maximum(qb - (nkk - 1) + kk, 0)
    else:
        kidx = lambda qb, kk: jnp.minimum(kk, qb)
    in_specs = [pl.BlockSpec((tb, LANES), lambda b, g, qb, kk: (b * nb + qb, ZB_NQ + 2 * g)),
                pl.BlockSpec((tb, LANES), lambda b, g, qb, kk: (b * nb + qb, ZB_NQ + 2 * g + 1)),
                pl.BlockSpec((tb, LANES), lambda b, g, qb, kk: (b * nb + kidx(qb, kk), zb_kv + g))]
    args = [z, z, z]
    if not windowed:
        in_specs += [pl.BlockSpec((tb, LANES), lambda b, g, qb, kk: (b * nb + qb, g)),
                     pl.BlockSpec((tb, LANES), lambda b, g, qb, kk: (kidx(qb, kk), 0))]
        args += [sel, emat]
    in_specs += [pl.BlockSpec((tb, LANES), lambda b, g, qb, kk: (b * nb + qb, g)),
                 pl.BlockSpec(tiles.shape, lambda b, g, qb, kk: (0, 0, 0, 0))]
    args += [gsig, tiles]
    return pl.pallas_call(
        functools.partial(_nsaflash_kernel, t=t, nsub=nsub, nkk=nkk, windowed=windowed,
                          gate_col=2 if windowed else 1),
        grid=(batch, N_KV, nb, nkk),
        in_specs=in_specs,
        out_specs=pl.BlockSpec((tb, 2 * LANES), lambda b, g, qb, kk: (b * nb + qb, g)),
        out_shape=jax.ShapeDtypeStruct((m, N_NSA * HEAD_DIM), F32),
        scratch_shapes=[pltpu.VMEM((nsub * GROUP * t, LANES), BF16), pltpu.VMEM((nsub * GROUP * t, LANES), F32),
                        pltpu.VMEM((nsub * GROUP * t, LANES), F32)],
        compiler_params=_cparams(("parallel", "parallel", "parallel", "arbitrary")),
        name="nsa_window" if windowed else "nsa_selected",
    )(*args)


def _conv_kernel(a_ref, g_ref, ah_ref, gh_ref, w_ref, b_ref, lg_ref, lb_ref, o_ref, u_ref, us_ref, *, ts):
    first = pl.program_id(1) == 0
    halo = ah_ref[...].astype(F32) * jax.nn.sigmoid(gh_ref[...].astype(F32))
    u_ref[0:HALO, :] = jnp.where(first, 0.0, halo)
    u_ref[HALO:HALO + ts, :] = a_ref[...].astype(F32) * jax.nn.sigmoid(g_ref[...].astype(F32))
    n_shift = HALO + ts - SUBLANES
    for s in range(1, SUBLANES):
        us_ref[s - 1, 0:n_shift, :] = u_ref[s:s + n_shift, :]
    base = HALO - (CONV_WIDTH - 1)
    for c in range(ts // CONV_ROWS):
        r0 = c * CONV_ROWS
        y = jnp.zeros((CONV_ROWS, D_CONV), F32)
        for k in range(CONV_WIDTH):
            start, s = divmod(base + k + r0, SUBLANES)
            start *= SUBLANES
            window = (u_ref[start:start + CONV_ROWS, :] if s == 0
                      else us_ref[s - 1, start:start + CONV_ROWS, :])
            y = y + window * w_ref[k:k + 1, :]
        y = y + b_ref[...]
        mu = jnp.mean(y, axis=-1, keepdims=True)
        yc = y - mu
        var = jnp.mean(yc * yc, axis=-1, keepdims=True)
        yn = yc * lax.rsqrt(var + EPS) * lg_ref[...] + lb_ref[...]
        o_ref[r0:r0 + CONV_ROWS, :] = (yn * jax.nn.sigmoid(yn)).astype(BF16)


def _conformer_conv(z, w, b, lg, lb, batch, seq):
    ts = TS_CONV
    ns = seq // ts
    m = batch * seq
    cb = D_CONV // LANES
    hb = ts // HALO
    halo_row = lambda bb, s: jnp.maximum((bb * ns + s) * hb - 1, 0)
    return pl.pallas_call(
        functools.partial(_conv_kernel, ts=ts),
        grid=(batch, ns),
        in_specs=[pl.BlockSpec((ts, D_CONV), lambda bb, s: (bb * ns + s, ZB_CA // cb)),
                  pl.BlockSpec((ts, D_CONV), lambda bb, s: (bb * ns + s, ZB_CG // cb)),
                  pl.BlockSpec((HALO, D_CONV), lambda bb, s: (halo_row(bb, s), ZB_CA // cb)),
                  pl.BlockSpec((HALO, D_CONV), lambda bb, s: (halo_row(bb, s), ZB_CG // cb)),
                  pl.BlockSpec((CONV_WIDTH, D_CONV), lambda bb, s: (0, 0)),
                  pl.BlockSpec((1, D_CONV), lambda bb, s: (0, 0)),
                  pl.BlockSpec((1, D_CONV), lambda bb, s: (0, 0)),
                  pl.BlockSpec((1, D_CONV), lambda bb, s: (0, 0))],
        out_specs=pl.BlockSpec((ts, D_CONV), lambda bb, s: (bb * ns + s, 0)),
        out_shape=jax.ShapeDtypeStruct((m, D_CONV), BF16),
        scratch_shapes=[pltpu.VMEM((HALO + ts, D_CONV), F32),
                        pltpu.VMEM((SUBLANES - 1, HALO + ts, D_CONV), F32)],
        compiler_params=_cparams(("parallel", "parallel")),
        name="conformer_conv",
    )(z, z, z, z, w, b, lg, lb)


def _outproj_kernel(x_ref, mod_ref, fox_ref, oc_ref, os_ref, ow_ref, cv_ref, w_ref, o_ref):
    d_fox = N_FOX * HEAD_DIM
    d_nsa = N_NSA * HEAD_DIM
    nsa = (oc_ref[...] + os_ref[...] + ow_ref[...]).astype(BF16)
    acc = _dot(fox_ref[...], w_ref[0:d_fox, :])
    acc = acc + _dot(nsa, w_ref[d_fox:d_fox + d_nsa, :])
    acc = acc + _dot(cv_ref[...], w_ref[d_fox + d_nsa:, :])
    o_ref[...] = x_ref[...] + mod_ref[2:3, :] * acc


def _outproj(x2, mod_l, fox, oc, osel, owin, conv, w, layer, seq):
    m, d = x2.shape
    tm = TM_OUT
    row = lambda i: (i, 0)
    return pl.pallas_call(
        _outproj_kernel,
        grid=(m // tm,),
        in_specs=[pl.BlockSpec((tm, d), row),
                  pl.BlockSpec((None, 6, d), lambda i: (i * tm // seq, 0, 0)),
                  pl.BlockSpec((tm, N_FOX * HEAD_DIM), row),
                  pl.BlockSpec((tm, N_NSA * HEAD_DIM), row),
                  pl.BlockSpec((tm, N_NSA * HEAD_DIM), row),
                  pl.BlockSpec((tm, N_NSA * HEAD_DIM), row),
                  pl.BlockSpec((tm, D_CONV), row),
                  pl.BlockSpec((None, d, d), lambda i: (layer, 0, 0))],
        out_specs=pl.BlockSpec((tm, d), row),
        out_shape=jax.ShapeDtypeStruct((m, d), F32),
        compiler_params=_cparams(("parallel",)),
        name="outproj",
    )(x2, mod_l, fox, oc, osel, owin, conv, w)


def _mlp_kernel(x_ref, mod_ref, g_ref, w1_ref, w2_ref, fg_ref, o_ref, h_ref, *, tm, final):
    f = pl.program_id(1)

    @pl.when(f == 0)
    def _():
        def chunk(c, carry):
            r = pl.multiple_of(c * ROW_CHUNK, ROW_CHUNK)
            h = _norm_mod(x_ref[pl.ds(r, ROW_CHUNK), :], g_ref[...], mod_ref[4:5, :], mod_ref[3:4, :])
            h_ref[pl.ds(r, ROW_CHUNK), :] = h.astype(BF16)
            return carry
        lax.fori_loop(0, tm // ROW_CHUNK, chunk, 0)
        o_ref[...] = jnp.zeros_like(o_ref)

    a = jnp.maximum(_dot(h_ref[...], w1_ref[...]), 0.0)
    o_ref[...] += _dot((a * a).astype(BF16), w2_ref[...])

    @pl.when(f == pl.num_programs(1) - 1)
    def _():
        def chunk(c, carry):
            r = pl.multiple_of(c * ROW_CHUNK, ROW_CHUNK)
            y = x_ref[pl.ds(r, ROW_CHUNK), :] + mod_ref[5:6, :] * o_ref[pl.ds(r, ROW_CHUNK), :]
            if final:
                y = y * lax.rsqrt(jnp.mean(y * y, axis=-1, keepdims=True) + EPS) * fg_ref[...]
            o_ref[pl.ds(r, ROW_CHUNK), :] = y
            return carry
        lax.fori_loop(0, tm // ROW_CHUNK, chunk, 0)


def _mlp(x2, mod_l, g, w1, w2, fg, layer, seq, final):
    m, d = x2.shape
    tm, tf = TM_MLP, TF_MLP
    dff = w1.shape[2]
    return pl.pallas_call(
        functools.partial(_mlp_kernel, tm=tm, final=final),
        grid=(m // tm, dff // tf),
        in_specs=[pl.BlockSpec((tm, d), lambda i, f: (i, 0), pipeline_mode=pl.Buffered(1)),
                  pl.BlockSpec((None, 6, d), lambda i, f: (i * tm // seq, 0, 0)),
                  pl.BlockSpec((1, d), lambda i, f: (0, 0)),
                  pl.BlockSpec((None, d, tf), lambda i, f: (layer, 0, f)),
                  pl.BlockSpec((None, tf, d), lambda i, f: (layer, f, 0)),
                  pl.BlockSpec((1, d), lambda i, f: (0, 0))],
        out_specs=pl.BlockSpec((tm, d), lambda i, f: (i, 0)),
        out_shape=jax.ShapeDtypeStruct((m, d), F32),
        scratch_shapes=[pltpu.VMEM((tm, d), BF16)],
        compiler_params=_cparams(("parallel", "arbitrary")),
        name="mlp_final" if final else "mlp",
    )(x2, mod_l, g, w1, w2, fg)


def _prep_inproj_weights(w_in):
    depth, d, _ = w_in.shape
    def sl(name, width, off=0, scale=None):
        piece = w_in[:, :, _SRC[name] + off:_SRC[name] + off + width]
        return (piece if scale is None else piece * scale).astype(BF16)
    qs = HEAD_DIM ** -0.5 * LOG2E
    pieces = [sl("fq", 768, scale=qs), sl("fk", 768), sl("fv", 768), sl("nq", 768, scale=qs),
              sl("ca", D_CONV), sl("cg", D_CONV)]
    for kname, vname in (("kc", "vc"), ("ks", "vs"), ("kw", "vw")):
        for g in range(N_KV):
            pieces += [sl(kname, HEAD_DIM, g * HEAD_DIM), sl(vname, HEAD_DIM, g * HEAD_DIM)]
    pieces.append(jnp.zeros((depth, d, LANES), BF16))
    wz = jnp.concatenate(pieces, axis=2)
    assert wz.shape[2] == NZ
    ws = jnp.concatenate([sl("ff", N_FOX), sl("ng", 3 * N_NSA),
                          jnp.zeros((depth, d, LANES - N_FOX - 3 * N_NSA), BF16)], axis=2)
    return wz, ws


def _prep_compress_weights(w_cmp_k, w_cmp_v, pos_cmp):
    depth = w_cmp_k.shape[0]
    slots = 2
    wkv = jnp.stack([w_cmp_k, w_cmp_v], axis=1)
    big = jnp.einsum("zklde,kq->zlkdqe", wkv, jnp.eye(2, dtype=F32))
    big = big.reshape(depth, CMP_LEN, slots * HEAD_DIM, slots * HEAD_DIM)
    half = CMP_LEN // 2
    kdim = half * slots * HEAD_DIM
    wlo = big[:, :half].reshape(depth, kdim, slots * HEAD_DIM).astype(BF16)
    whi = big[:, half:].reshape(depth, kdim, slots * HEAD_DIM).astype(BF16)
    pos = jnp.broadcast_to(pos_cmp[:, :, None, :], (depth, CMP_LEN, slots, HEAD_DIM))
    plo = jnp.zeros((depth, 8, kdim), F32).at[:, 0].set(pos[:, :half].reshape(depth, kdim)).astype(BF16)
    phi = jnp.zeros((depth, 8, kdim), F32).at[:, 0].set(pos[:, half:].reshape(depth, kdim)).astype(BF16)
    return wlo, whi, plo, phi


def _selection_constants(seq):
    nc = seq // CMP_STRIDE
    n_cmp = (seq - CMP_LEN) // CMP_STRIDE + 1
    n = np.arange(nc)[:, None]
    j = np.arange(LANES)[None, :]
    overlap = ((n * CMP_STRIDE < j * SEL_LEN + SEL_LEN) & (n * CMP_STRIDE + CMP_LEN - 1 >= j * SEL_LEN)
               & (n < n_cmp) & (j < seq // SEL_LEN))
    emat = (np.arange(seq)[:, None] // SEL_LEN + HALF) == np.arange(LANES)[None, :]
    return jnp.asarray(overlap, BF16), jnp.asarray(emat, BF16)


def kernel(x, c, w_mod, b_mod, norm1_g, w_in, b_f, w_cmp_k, w_cmp_v, pos_cmp, conv_w, conv_b, conv_ln_g,
           conv_ln_b, w_out, norm2_g, w_mlp1, w_mlp2, rel_bias, final_g):
    batch, seq, d = x.shape
    depth = w_in.shape[0]
    assert d == D_MODEL and seq % TM_INPROJ == 0 and seq // SEL_LEN <= SEL_LEN

    wz, ws = _prep_inproj_weights(w_in)
    wlo, whi, plo, phi = _prep_compress_weights(w_cmp_k, w_cmp_v, pos_cmp)
    w_out_b = w_out.astype(BF16)
    w1_b = w_mlp1.astype(BF16)
    w2_b = w_mlp2.astype(BF16)
    overlap, emat = _selection_constants(seq)
    bf_rows = jnp.zeros((depth, 1, LANES), F32).at[:, 0, SMALL_FF:SMALL_FF + N_FOX].set(b_f)

    mod = _modulation(c, w_mod, b_mod)
    fvec = _bias_by_distance(rel_bias)
    tiles = _bias_tiles(fvec)
    fg = final_g.reshape(1, d)

    x2 = x.reshape(batch * seq, d)
    kv0 = ZB_KVC * LANES
    for l in range(depth):
        z, small = _inproj(x2, mod[l], norm1_g[l].reshape(1, d), wz, ws, l, seq)
        fcol, frow = _fox_cumsum(small, bf_rows[l], batch, seq)
        o_fox = _fox_attention(z, fcol, frow, batch, seq)
        r = z[:, kv0:kv0 + N_KV * LANES].reshape(batch, seq // CMP_STRIDE, CMP_STRIDE, N_KV, LANES)
        r = r.transpose(0, 3, 1, 2, 4).reshape(batch, N_KV, seq // CMP_STRIDE, CMP_STRIDE * LANES)
        kvcmp = _compress(r, wlo[l], whi[l], plo[l], phi[l])
        o_cmp, sel, gsig = _nsa_compressed(z, kvcmp, small, fvec, overlap, batch, seq)
        o_sel = _nsa_flash(z, sel, emat, gsig, tiles, batch, seq, windowed=False)
        o_win = _nsa_flash(z, sel, emat, gsig, tiles, batch, seq, windowed=True)
        o_conv = _conformer_conv(z, conv_w[l], conv_b[l].reshape(1, -1), conv_ln_g[l].reshape(1, -1),
                                 conv_ln_b[l].reshape(1, -1), batch, seq)
        x2 = _outproj(x2, mod[l], o_fox, o_cmp, o_sel, o_win, o_conv, w_out_b, l, seq)
        x2 = _mlp(x2, mod[l], norm2_g[l].reshape(1, d), w1_b, w2_b, fg, l, seq, final=(l == depth - 1))
    return x2.reshape(batch, seq, d)
```

```python
import functools
import math

import numpy as np
import jax
import jax.numpy as jnp
from jax import lax
from jax.experimental import pallas as pl
from jax.experimental.pallas import tpu as pltpu

F32 = jnp.float32
BF16 = jnp.bfloat16
HIGHEST = lax.Precision.HIGHEST

D_MODEL = 2048
HEAD_DIM = 64
N_FOX = 12
N_NSA = 12
N_KV = 3
GROUP = 4
D_CONV = 512
D_FF = 4 * D_MODEL
CMP_LEN = 32
CMP_STRIDE = 16
SEL_LEN = 64
N_SEL = 16
WINDOW = 512
CONV_WIDTH = 31
N_BUCKETS = 32
MAX_DISTANCE = 128
EPS = 1e-6
NEG = -1e30
BIG = 1e9
LOG2E = math.log2(math.e)

LANES = 128
SUBLANES = 8
HALF = LANES // 2

ZB_FQ, ZB_FK, ZB_FV, ZB_NQ = 0, 6, 12, 18
ZB_CA, ZB_CG = 24, 28
ZB_KVC, ZB_KVS, ZB_KVW = 32, 35, 38
NZ_BLOCKS = 42
NZ = NZ_BLOCKS * LANES
SMALL_FF, SMALL_NG = 0, 12

_SRC = dict(fq=0, fk=768, fv=1536, ff=2304, nq=2316, kc=3084, vc=3276, ks=3468, vs=3660,
            kw=3852, vw=4044, ng=4236, ca=4272, cg=4784)

TM_INPROJ = 1024
TN_INPROJ = 768
TM_OUT = 512
TM_MLP = 1024
TF_MLP = 512
ROW_CHUNK = 128
T_FOX = 512
NSUB_FOX = 4
T_NSA = 256
NSUB_NSA = 4
TQ_CMP = 512
TS_CONV = 512
CONV_ROWS = 64
TS_CUM = 512
HALO = 32
VMEM_LIMIT = 56 * 1024 * 1024


def _cparams(sem):
    return pltpu.CompilerParams(dimension_semantics=sem, vmem_limit_bytes=VMEM_LIMIT)


def _dot(a, b):
    return jnp.dot(a, b, preferred_element_type=F32)


def _dot_nt(a, b):
    return lax.dot_general(a, b, (((1,), (1,)), ((), ())), preferred_element_type=F32)


def _bucket_thresholds():
    d = np.arange(MAX_DISTANCE, dtype=np.int32)
    max_exact = N_BUCKETS // 2
    nf = np.maximum(d, 1).astype(np.float32)
    large = max_exact + (np.log(nf / np.float32(max_exact)) / np.float32(math.log(MAX_DISTANCE / max_exact))
                         * np.float32(N_BUCKETS - max_exact)).astype(np.int32)
    large = np.minimum(large, N_BUCKETS - 1)
    bucket = np.where(d < max_exact, d, large)
    assert bucket[-1] == N_BUCKETS - 1 and np.all(np.diff(bucket) >= 0)
    return [int(np.argmax(bucket >= k)) for k in range(N_BUCKETS)]


_T5_THRESH = _bucket_thresholds()


def _fvec_kernel(tbl_ref, out_ref):
    d = lax.broadcasted_iota(jnp.int32, out_ref.shape, 1)
    out = jnp.broadcast_to(tbl_ref[:, 0:1], out_ref.shape)
    for k in range(1, N_BUCKETS):
        out = jnp.where(d >= _T5_THRESH[k], tbl_ref[:, k:k + 1], out)
    out_ref[...] = out * LOG2E


def _bias_by_distance(rel_bias):
    tbl = jnp.zeros((16, LANES), F32).at[:N_NSA, :N_BUCKETS].set(rel_bias.astype(F32).T)
    return pl.pallas_call(
        _fvec_kernel, out_shape=jax.ShapeDtypeStruct((16, LANES), F32), name="t5_bias_by_distance",
    )(tbl)


def _mod_kernel(ct_ref, w_ref, b_ref, o_ref, *, nb):
    c = ct_ref[...]
    ca = c * jax.nn.sigmoid(c)
    w = w_ref[...]
    rows = [jnp.sum(w * ca[:, b:b + 1], axis=0, keepdims=True) for b in range(nb)]
    rows.append(jnp.zeros((SUBLANES - nb, w.shape[1]), F32))
    o_ref[...] = jnp.concatenate(rows, axis=0) + b_ref[...]


def _modulation(c, w_mod, b_mod):
    depth, d, n = w_mod.shape
    b = c.shape[0]
    assert b < SUBLANES
    tn = 1024
    ct = jnp.zeros((d, LANES), F32).at[:, :b].set(c.T)
    out = pl.pallas_call(
        functools.partial(_mod_kernel, nb=b),
        grid=(depth, n // tn),
        in_specs=[pl.BlockSpec((d, LANES), lambda l, j: (0, 0)),
                  pl.BlockSpec((None, d, tn), lambda l, j: (l, 0, j)),
                  pl.BlockSpec((None, 1, tn), lambda l, j: (l, 0, j))],
        out_specs=pl.BlockSpec((None, SUBLANES, tn), lambda l, j: (l, 0, j)),
        out_shape=jax.ShapeDtypeStruct((depth, SUBLANES, n), F32),
        compiler_params=_cparams(("parallel", "parallel")),
        name="adaln_modulation",
    )(ct, w_mod, b_mod.reshape(depth, 1, n))
    return out[:, :b].reshape(depth, b, 6, d)


def _norm_mod(x, g, sc, sh):
    y = x * lax.rsqrt(jnp.mean(x * x, axis=-1, keepdims=True) + EPS) * g
    return y * (1.0 + sc) + sh


def _inproj_kernel(x_ref, mod_ref, g_ref, w_ref, ws_ref, z_ref, sm_ref, h_ref, *, tm):
    @pl.when(pl.program_id(1) == 0)
    def _():
        def chunk(c, carry):
            r = pl.multiple_of(c * ROW_CHUNK, ROW_CHUNK)
            h = _norm_mod(x_ref[pl.ds(r, ROW_CHUNK), :], g_ref[...], mod_ref[1:2, :], mod_ref[0:1, :])
            hb = h.astype(BF16)
            h_ref[pl.ds(r, ROW_CHUNK), :] = hb
            sm_ref[pl.ds(r, ROW_CHUNK), :] = _dot(hb, ws_ref[...])
            return carry
        lax.fori_loop(0, tm // ROW_CHUNK, chunk, 0)

    z_ref[...] = _dot(h_ref[...], w_ref[...]).astype(BF16)


def _inproj(x2, mod_l, g, wz, ws, layer, seq):
    m, d = x2.shape
    tm, tn = TM_INPROJ, TN_INPROJ
    assert seq % tm == 0 and NZ % tn == 0
    return pl.pallas_call(
        functools.partial(_inproj_kernel, tm=tm),
        grid=(m // tm, NZ // tn),
        in_specs=[pl.BlockSpec((tm, d), lambda i, j: (i, 0)),
                  pl.BlockSpec((None, 6, d), lambda i, j: (i * tm // seq, 0, 0)),
                  pl.BlockSpec((1, d), lambda i, j: (0, 0)),
                  pl.BlockSpec((None, d, tn), lambda i, j: (layer, 0, j)),
                  pl.BlockSpec((None, d, LANES), lambda i, j: (layer, 0, 0))],
        out_specs=[pl.BlockSpec((tm, tn), lambda i, j: (i, j)),
                   pl.BlockSpec((tm, LANES), lambda i, j: (i, 0))],
        out_shape=[jax.ShapeDtypeStruct((m, NZ), BF16), jax.ShapeDtypeStruct((m, LANES), F32)],
        scratch_shapes=[pltpu.VMEM((tm, d), BF16)],
        compiler_params=_cparams(("parallel", "arbitrary")),
        name="inproj",
    )(x2, mod_l, g, wz, ws)


def _foxcum_kernel(sm_ref, bf_ref, fcol_ref, frow_ref, carry_ref, *, ts):
    @pl.when(pl.program_id(1) == 0)
    def _():
        carry_ref[...] = jnp.zeros_like(carry_ref)

    x = sm_ref[...] + bf_ref[...]
    ls = (jnp.minimum(x, 0.0) - jnp.log1p(jnp.exp(-jnp.abs(x)))) * LOG2E
    r = lax.broadcasted_iota(jnp.int32, (ts, ts), 0)
    c = lax.broadcasted_iota(jnp.int32, (ts, ts), 1)
    tri = jnp.where(r >= c, 1.0, 0.0).astype(F32)
    cum = jnp.dot(tri, ls, precision=HIGHEST, preferred_element_type=F32) + carry_ref[0:1, :]
    carry_ref[0:1, :] = cum[ts - 1:ts, :]
    cum_t = cum.T
    for hp in range(N_FOX // 2):
        fcol_ref[hp] = cum if hp == 0 else pltpu.roll(cum, LANES - 2 * hp, 1)
        frow_ref[hp, 0:2, :] = cum_t[2 * hp:2 * hp + 2, :]
        frow_ref[hp, 2:8, :] = jnp.zeros((6, ts), F32)


def _fox_cumsum(small, bf_row, batch, seq):
    ts = TS_CUM
    ns = seq // ts
    npair = N_FOX // 2
    return pl.pallas_call(
        functools.partial(_foxcum_kernel, ts=ts),
        grid=(batch, ns),
        in_specs=[pl.BlockSpec((ts, LANES), lambda b, s: (b * ns + s, 0)),
                  pl.BlockSpec((1, LANES), lambda b, s: (0, 0))],
        out_specs=[pl.BlockSpec((None, npair, ts, LANES), lambda b, s: (b, 0, s, 0)),
                   pl.BlockSpec((None, npair, 8, ts), lambda b, s: (b, 0, 0, s))],
        out_shape=[jax.ShapeDtypeStruct((batch, npair, seq, LANES), F32),
                   jax.ShapeDtypeStruct((batch, npair, 8, seq), F32)],
        scratch_shapes=[pltpu.VMEM((8, LANES), F32)],
        compiler_params=_cparams(("parallel", "arbitrary")),
        name="fox_cumsum",
    )(small, bf_row)


def _fox_kernel(q_ref, k_ref, v_ref, fc_ref, fr_ref, o_ref, m_ref, fq_ref, acc_ref, *, t, nsub):
    qb = pl.program_id(2)
    kb = pl.program_id(3)
    tb = nsub * t

    @pl.when(kb == 0)
    def _():
        m_ref[...] = jnp.full(m_ref.shape, NEG, F32)
        acc_ref[...] = jnp.zeros_like(acc_ref)
        for h in range(2):
            fq_ref[h] = jnp.broadcast_to(fc_ref[:, h:h + 1], (tb, LANES))

    lane = lax.broadcasted_iota(jnp.int32, (1, LANES), 1)

    def sub_tile(i, j, diagonal):
        rows = slice(i * t, (i + 1) * t)
        cols = slice(j * t, (j + 1) * t)
        q2 = q_ref[rows, :]
        k2 = k_ref[cols, :]
        v2 = v_ref[cols, :]
        if diagonal:
            causal = (lax.broadcasted_iota(jnp.int32, (t, t), 1) <= lax.broadcasted_iota(jnp.int32, (t, t), 0))
        zero = jnp.zeros_like(q2)
        s2 = _dot_nt(jnp.concatenate([jnp.where(lane < HALF, q2, zero), jnp.where(lane >= HALF, q2, zero)], axis=0), k2)
        for h in range(2):
            keep = (lane < HALF) if h == 0 else (lane >= HALF)
            vh = jnp.where(keep, v2, jnp.ones_like(v2))
            s = s2[h * t:(h + 1) * t, :] - fr_ref[h:h + 1, cols]
            if diagonal:
                s = jnp.where(causal, s, NEG)
            fq = fq_ref[h, rows, :]
            m_old = m_ref[h, rows, :]
            m_new = jnp.maximum(m_old, jnp.max(s, axis=1, keepdims=True) + fq)
            shift = jnp.tile(fq - m_new, (1, t // LANES))
            p = jnp.exp2(s + shift)
            acc_ref[h, rows, :] = jnp.exp2(m_old - m_new) * acc_ref[h, rows, :] + _dot(p.astype(BF16), vh)
            m_ref[h, rows, :] = m_new

    @pl.when(kb < qb)
    def _():
        for i in range(nsub):
            for j in range(nsub):
                sub_tile(i, j, False)

    @pl.when(kb == qb)
    def _():
        for i in range(nsub):
            for j in range(i + 1):
                sub_tile(i, j, i == j)
        a0 = acc_ref[0]
        a1 = acc_ref[1]
        o_ref[...] = jnp.where(lane < HALF, a0 / pltpu.roll(a0, HALF, 1),
                               a1 / pltpu.roll(a1, HALF, 1)).astype(BF16)


def _fox_attention(z, fcol, frow, batch, seq):
    t, nsub = T_FOX, NSUB_FOX
    tb = t * nsub
    nb = seq // tb
    npair = N_FOX // 2
    m = batch * seq
    kv_row = lambda b, hp, qb, kb: b * nb + jnp.minimum(kb, qb)
    return pl.pallas_call(
        functools.partial(_fox_kernel, t=t, nsub=nsub),
        grid=(batch, npair, nb, nb),
        in_specs=[pl.BlockSpec((tb, LANES), lambda b, hp, qb, kb: (b * nb + qb, ZB_FQ + hp)),
                  pl.BlockSpec((tb, LANES), lambda b, hp, qb, kb: (kv_row(b, hp, qb, kb), ZB_FK + hp)),
                  pl.BlockSpec((tb, LANES), lambda b, hp, qb, kb: (kv_row(b, hp, qb, kb), ZB_FV + hp)),
                  pl.BlockSpec((None, None, tb, LANES), lambda b, hp, qb, kb: (b, hp, qb, 0)),
                  pl.BlockSpec((None, None, 8, tb), lambda b, hp, qb, kb: (b, hp, 0, jnp.minimum(kb, qb)))],
        out_specs=pl.BlockSpec((tb, LANES), lambda b, hp, qb, kb: (b * nb + qb, hp)),
        out_shape=jax.ShapeDtypeStruct((m, N_FOX * HEAD_DIM), BF16),
        scratch_shapes=[pltpu.VMEM((2, tb, LANES), F32), pltpu.VMEM((2, tb, LANES), F32),
                        pltpu.VMEM((2, tb, LANES), F32)],
        compiler_params=_cparams(("parallel", "parallel", "parallel", "arbitrary")),
        name="fox_attention",
    )(z, z, z, fcol, frow)


def _compress_kernel(r_ref, wlo_ref, whi_ref, plo_ref, phi_ref, o_ref, *, nc):
    bias = _dot(plo_ref[...], wlo_ref[...]) + _dot(phi_ref[...], whi_ref[...])
    for g in range(N_KV):
        r = r_ref[g]
        out = _dot(r, wlo_ref[...]) + pltpu.roll(_dot(r, whi_ref[...]), nc - 1, 0) + bias[0:1, :]
        o_ref[:, g * LANES:(g + 1) * LANES] = out.astype(BF16)


def _compress(r, wlo, whi, plo, phi):
    batch, ng, nc, k = r.shape
    return pl.pallas_call(
        functools.partial(_compress_kernel, nc=nc),
        grid=(batch,),
        in_specs=[pl.BlockSpec((None, ng, nc, k), lambda b: (b, 0, 0, 0)),
                  pl.BlockSpec((k, LANES), lambda b: (0, 0)),
                  pl.BlockSpec((k, LANES), lambda b: (0, 0)),
                  pl.BlockSpec((8, k), lambda b: (0, 0)),
                  pl.BlockSpec((8, k), lambda b: (0, 0))],
        out_specs=pl.BlockSpec((None, nc, ng * LANES), lambda b: (b, 0, 0)),
        out_shape=jax.ShapeDtypeStruct((batch, nc, ng * LANES), BF16),
        compiler_params=_cparams(("parallel",)),
        name="nsa_compress",
    )(r, wlo, whi, plo, phi)


def _head_queries(qf, lane):
    zero = jnp.zeros_like(qf)
    qa = jnp.where(lane < HALF, qf, zero).astype(BF16)
    qb = jnp.where(lane < HALF, pltpu.roll(qf, HALF, 1), zero).astype(BF16)
    return qa, qb


def _gather_bias(frow, idx, tq):
    tab = jnp.broadcast_to(frow, (tq, LANES))
    parts = [jnp.take_along_axis(tab, idx[:, c * LANES:(c + 1) * LANES], axis=1)
             for c in range(idx.shape[1] // LANES)]
    return parts[0] if len(parts) == 1 else jnp.concatenate(parts, axis=1)


def _nsacmp_kernel(q_ref, kvc_ref, sm_ref, fv_ref, ov_ref, oc_ref, sel_ref, gs_ref, *, tq, nc):
    qi = pl.program_id(1)
    t0 = qi * tq
    lane = lax.broadcasted_iota(jnp.int32, (1, LANES), 1)
    t_col = t0 + lax.broadcasted_iota(jnp.int32, (tq, 1), 0)
    cmp_end = lax.broadcasted_iota(jnp.int32, (1, nc), 1) * CMP_STRIDE + (CMP_LEN - 1)
    dist = t_col - cmp_end
    masked = jnp.where(dist >= 0, 0.0, NEG)
    any_valid = t_col >= CMP_LEN - 1
    idx = jnp.clip(dist, 0, MAX_DISTANCE - 1)

    gs = jax.nn.sigmoid(sm_ref[...])
    n_sel = SEL_LEN
    j_row = lax.broadcasted_iota(jnp.int32, (n_sel, tq), 0)
    sub_iota = lax.broadcasted_iota(jnp.int32, (SUBLANES, tq), 0)
    t_row = t0 + lax.broadcasted_iota(jnp.int32, (n_sel, tq), 1)
    cur = jnp.right_shift(t_row, SEL_LEN.bit_length() - 1)
    forced = (j_row == 0) | (j_row == cur) | (j_row == cur - 1)
    valid_s = j_row * SEL_LEN <= t_row

    for g in range(N_KV):
        kv = kvc_ref[:, g * LANES:(g + 1) * LANES]
        gs_g = pltpu.roll(gs, LANES - (SMALL_NG + g * GROUP * 3), 1)
        gs_ref[:, g * LANES:(g + 1) * LANES] = gs_g
        psum = jnp.zeros((tq, nc), F32)
        outs = []
        for pair in range(2):
            c0 = (g * 2 + pair) * LANES
            qpair = _head_queries(q_ref[:, c0:c0 + LANES].astype(F32), lane)
            for sub in range(2):
                r = pair * 2 + sub
                h = g * GROUP + r
                s = _dot_nt(qpair[sub], kv) + _gather_bias(fv_ref[h:h + 1, :], idx, tq) + masked
                e = jnp.exp2(s - jnp.max(s, axis=1, keepdims=True))
                p = e * jnp.where(any_valid, 1.0 / jnp.sum(e, axis=1, keepdims=True), 0.0)
                psum = psum + p
                o = _dot(p.astype(BF16), kv)
                outs.append(o * gs_g[:, r * 3:r * 3 + 1])
            oc_ref[:, c0:c0 + LANES] = jnp.where(lane < HALF, pltpu.roll(outs[-2], HALF, 1), outs[-1])

        p_hi = psum.astype(BF16)
        p_lo = (psum - p_hi.astype(F32)).astype(BF16)
        imp = _dot(p_hi, ov_ref[...]) + _dot(p_lo, ov_ref[...])
        imp_t = imp.T[0:n_sel, :]
        val = jnp.where(forced, BIG, jnp.where(valid_s, imp_t, NEG))
        groups = [val[SUBLANES * b:SUBLANES * (b + 1), :] for b in range(n_sel // SUBLANES)]
        counts = [jnp.zeros((SUBLANES, tq), F32) for _ in groups]
        for j in range(n_sel):
            jb, js = divmod(j, SUBLANES)
            row = groups[jb][js:js + 1, :]
            for b, grp in enumerate(groups):
                if b < jb:
                    ahead = row > grp
                elif b > jb:
                    ahead = row >= grp
                else:
                    ahead = (row > grp) | ((row == grp) & (js < sub_iota))
                counts[b] = counts[b] + jnp.where(ahead, 1.0, 0.0)
        cnt = jnp.concatenate(counts, axis=0)
        unchosen = jnp.where(cnt < float(N_SEL), 0.0, NEG)
        unchosen = jnp.concatenate([jnp.zeros((LANES - n_sel, tq), F32), unchosen], axis=0)
        sel_ref[:, g * LANES:(g + 1) * LANES] = unchosen.T.astype(BF16)


def _nsa_compressed(z, kvcmp, small, fvec, overlap, batch, seq):
    tq = TQ_CMP
    nq = seq // tq
    nc = kvcmp.shape[1]
    m = batch * seq
    w3 = N_KV * LANES
    return pl.pallas_call(
        functools.partial(_nsacmp_kernel, tq=tq, nc=nc),
        grid=(batch, nq),
        in_specs=[pl.BlockSpec((tq, N_NSA * HEAD_DIM), lambda b, i: (b * nq + i, ZB_NQ * LANES // (N_NSA * HEAD_DIM))),
                  pl.BlockSpec((None, nc, w3), lambda b, i: (b, 0, 0)),
                  pl.BlockSpec((tq, LANES), lambda b, i: (b * nq + i, 0)),
                  pl.BlockSpec((16, LANES), lambda b, i: (0, 0)),
                  pl.BlockSpec((nc, LANES), lambda b, i: (0, 0))],
        out_specs=[pl.BlockSpec((tq, N_NSA * HEAD_DIM), lambda b, i: (b * nq + i, 0)),
                   pl.BlockSpec((tq, w3), lambda b, i: (b * nq + i, 0)),
                   pl.BlockSpec((tq, w3), lambda b, i: (b * nq + i, 0))],
        out_shape=[jax.ShapeDtypeStruct((m, N_NSA * HEAD_DIM), F32),
                   jax.ShapeDtypeStruct((m, w3), BF16),
                   jax.ShapeDtypeStruct((m, w3), F32)],
        compiler_params=_cparams(("parallel", "parallel")),
        name="nsa_compressed",
    )(z, kvcmp, small, fvec, overlap)


def _nsaflash_kernel(*refs, t, nsub, nkk, windowed, gate_col):
    if windowed:
        qa_ref, qb_ref, kv_ref, gs_ref, tb_ref, o_ref, q4_ref, m_ref, acc_ref = refs
        sel_ref = e_ref = None
    else:
        qa_ref, qb_ref, kv_ref, sel_ref, e_ref, gs_ref, tb_ref, o_ref, q4_ref, m_ref, acc_ref = refs
    g = pl.program_id(1)
    qb = pl.program_id(2)
    kk = pl.program_id(3)
    far = tb_ref.shape[0] - 1
    lane = lax.broadcasted_iota(jnp.int32, (1, LANES), 1)

    @pl.when(kk == 0)
    def _():
        m_ref[...] = jnp.full(m_ref.shape, NEG, F32)
        acc_ref[...] = jnp.zeros_like(acc_ref)
        for i in range(nsub):
            rows = slice(i * t, (i + 1) * t)
            queries = (_head_queries(qa_ref[rows, :].astype(F32), lane)
                       + _head_queries(qb_ref[rows, :].astype(F32), lane))
            for r in range(GROUP):
                q = queries[r] if windowed else jnp.where(lane < HALF, queries[r], sel_ref[rows, :])
                q4_ref[(i * GROUP + r) * t:(i * GROUP + r + 1) * t, :] = q

    def sub_tile(i, j, rel):
        tidx = rel if (windowed or rel <= 1) else far
        kv = kv_ref[j * t:(j + 1) * t, :]
        ones_v = jnp.where(lane < HALF, jnp.ones_like(kv), kv)
        keys = kv if windowed else jnp.where(lane < HALF, kv, e_ref[j * t:(j + 1) * t, :])
        s4 = _dot_nt(q4_ref[i * GROUP * t:(i + 1) * GROUP * t, :], keys)
        for pair in range(2):
            base = (i * GROUP + 2 * pair) * t
            rows2 = slice(base, base + 2 * t)
            probs, alphas = [], []
            for sub in range(2):
                r = 2 * pair + sub
                h = g * GROUP + r
                rows = slice(base + sub * t, base + (sub + 1) * t)
                m_old = m_ref[rows, :]
                s = s4[r * t:(r + 1) * t, :] + tb_ref[tidx, h]
                m_new = jnp.maximum(m_old, jnp.max(s, axis=1, keepdims=True))
                p = jnp.exp2(s - jnp.tile(m_new, (1, t // LANES)))
                m_ref[rows, :] = m_new
                alphas.append(jnp.exp2(m_old - m_new))
                probs.append(p.astype(BF16))
            acc_ref[rows2, :] = (jnp.concatenate(alphas, axis=0) * acc_ref[rows2, :]
                                 + _dot(jnp.concatenate(probs, axis=0), ones_v))

    def block(offset):
        max_rel = far - 1 if windowed else None
        for j in range(nsub):
            for i in range(nsub):
                rel = offset * nsub + i - j
                if rel >= 0 and (max_rel is None or rel <= max_rel):
                    sub_tile(i, j, rel)

    if windowed:
        for c in range(nkk):
            @pl.when((kk == c) & (qb - (nkk - 1) + c >= 0))
            def _(c=c):
                block(nkk - 1 - c)
        last = kk == nkk - 1
    else:
        @pl.when(qb - kk == 0)
        def _():
            block(0)

        @pl.when(qb - kk == 1)
        def _():
            block(1)

        @pl.when(qb - kk >= 2)
        def _():
            block(2)
        last = kk == qb

    @pl.when(last)
    def _():
        for i in range(nsub):
            rows = slice(i * t, (i + 1) * t)
            gs = gs_ref[rows, :]
            o = []
            for r in range(GROUP):
                a = acc_ref[(i * GROUP + r) * t:(i * GROUP + r + 1) * t, :]
                o.append(a / pltpu.roll(a, HALF, 1) * gs[:, r * 3 + gate_col:r * 3 + gate_col + 1])
            o_ref[rows, 0:LANES] = jnp.where(lane < HALF, pltpu.roll(o[0], HALF, 1), o[1])
            o_ref[rows, LANES:2 * LANES] = jnp.where(lane < HALF, pltpu.roll(o[2], HALF, 1), o[3])


def _bias_tile_kernel(fv_ref, o_ref, *, t, nrel):
    rel = pl.program_id(0)
    h = pl.program_id(1)
    dist = rel * t + lax.broadcasted_iota(jnp.int32, (t, t), 0) - lax.broadcasted_iota(jnp.int32, (t, t), 1)
    bias = _gather_bias(fv_ref[pl.ds(h, 1), :], jnp.clip(dist, 0, MAX_DISTANCE - 1), t)
    o_ref[...] = jnp.where((dist >= 0) & ((dist < WINDOW) | (rel == nrel)), bias, NEG)


def _bias_tiles(fvec):
    t = T_NSA
    nrel = WINDOW // t + 1
    return pl.pallas_call(
        functools.partial(_bias_tile_kernel, t=t, nrel=nrel),
        grid=(nrel + 1, N_NSA),
        in_specs=[pl.BlockSpec((16, LANES), lambda r, h: (0, 0))],
        out_specs=pl.BlockSpec((None, None, t, t), lambda r, h: (r, h, 0, 0)),
        out_shape=jax.ShapeDtypeStruct((nrel + 1, N_NSA, t, t), F32),
        compiler_params=_cparams(("parallel", "parallel")),
        name="nsa_bias_tiles",
    )(fvec)


def _nsa_flash(z, sel, emat, gsig, tiles, batch, seq, windowed):
    t, nsub = T_NSA, NSUB_NSA
    tb = t * nsub
    nb = seq // tb
    m = batch * seq
    max_rel = WINDOW // t
    nkk = ((max_rel + nsub - 1) // nsub + 1) if windowed else nb
    assert t >= MAX_DISTANCE and 2 * t <= WINDOW and WINDOW % t == 0 and seq % tb == 0
    zb_kv = ZB_KVW if windowed else ZB_KVS
    if windowed:
        kidx = lambda qb, kk: jnp.maximum(qb - (nkk - 1) + kk, 0)
    else:
        kidx = lambda qb, kk: jnp.minimum(kk, qb)
    in_specs = [pl.BlockSpec((tb, LANES), lambda b, g, qb, kk: (b * nb + qb, ZB_NQ + 2 * g)),
                pl.BlockSpec((tb, LANES), lambda b, g, qb, kk: (b * nb + qb, ZB_NQ + 2 * g + 1)),
                pl.BlockSpec((tb, LANES), lambda b, g, qb, kk: (b * nb + kidx(qb, kk), zb_kv + g))]
    args = [z, z, z]
    if not windowed:
        in_specs += [pl.BlockSpec((tb, LANES), lambda b, g, qb, kk: (b * nb + qb, g)),
                     pl.BlockSpec((tb, LANES), lambda b, g, qb, kk: (kidx(qb, kk), 0))]
        args += [sel, emat]
    in_specs += [pl.BlockSpec((tb, LANES), lambda b, g, qb, kk: (b * nb + qb, g)),
                 pl.BlockSpec(tiles.shape, lambda b, g, qb, kk: (0, 0, 0, 0))]
    args += [gsig, tiles]
    return pl.pallas_call(
        functools.partial(_nsaflash_kernel, t=t, nsub=nsub, nkk=nkk, windowed=windowed,
                          gate_col=2 if windowed else 1),
        grid=(batch, N_KV, nb, nkk),
        in_specs=in_specs,
        out_specs=pl.BlockSpec((tb, 2 * LANES), lambda b, g, qb, kk: (b * nb + qb, g)),
        out_shape=jax.ShapeDtypeStruct((m, N_NSA * HEAD_DIM), F32),
        scratch_shapes=[pltpu.VMEM((nsub * GROUP * t, LANES), BF16), pltpu.VMEM((nsub * GROUP * t, LANES), F32),
                        pltpu.VMEM((nsub * GROUP * t, LANES), F32)],
        compiler_params=_cparams(("parallel", "parallel", "parallel", "arbitrary")),
        name="nsa_window" if windowed else "nsa_selected",
    )(*args)


def _conv_kernel(a_ref, g_ref, ah_ref, gh_ref, w_ref, b_ref, lg_ref, lb_ref, o_ref, u_ref, us_ref, *, ts):
    first = pl.program_id(1) == 0
    halo = ah_ref[...].astype(F32) * jax.nn.sigmoid(gh_ref[...].astype(F32))
    u_ref[0:HALO, :] = jnp.where(first, 0.0, halo)
    u_ref[HALO:HALO + ts, :] = a_ref[...].astype(F32) * jax.nn.sigmoid(g_ref[...].astype(F32))
    n_shift = HALO + ts - SUBLANES
    for s in range(1, SUBLANES):
        us_ref[s - 1, 0:n_shift, :] = u_ref[s:s + n_shift, :]
    base = HALO - (CONV_WIDTH - 1)
    for c in range(ts // CONV_ROWS):
        r0 = c * CONV_ROWS
        y = jnp.zeros((CONV_ROWS, D_CONV), F32)
        for k in range(CONV_WIDTH):
            start, s = divmod(base + k + r0, SUBLANES)
            start *= SUBLANES
            window = (u_ref[start:start + CONV_ROWS, :] if s == 0
                      else us_ref[s - 1, start:start + CONV_ROWS, :])
            y = y + window * w_ref[k:k + 1, :]
        y = y + b_ref[...]
        mu = jnp.mean(y, axis=-1, keepdims=True)
        yc = y - mu
        var = jnp.mean(yc * yc, axis=-1, keepdims=True)
        yn = yc * lax.rsqrt(var + EPS) * lg_ref[...] + lb_ref[...]
        o_ref[r0:r0 + CONV_ROWS, :] = (yn * jax.nn.sigmoid(yn)).astype(BF16)


def _conformer_conv(z, w, b, lg, lb, batch, seq):
    ts = TS_CONV
    ns = seq // ts
    m = batch * seq
    cb = D_CONV // LANES
    hb = ts // HALO
    halo_row = lambda bb, s: jnp.maximum((bb * ns + s) * hb - 1, 0)
    return pl.pallas_call(
        functools.partial(_conv_kernel, ts=ts),
        grid=(batch, ns),
        in_specs=[pl.BlockSpec((ts, D_CONV), lambda bb, s: (bb * ns + s, ZB_CA // cb)),
                  pl.BlockSpec((ts, D_CONV), lambda bb, s: (bb * ns + s, ZB_CG // cb)),
                  pl.BlockSpec((HALO, D_CONV), lambda bb, s: (halo_row(bb, s), ZB_CA // cb)),
                  pl.BlockSpec((HALO, D_CONV), lambda bb, s: (halo_row(bb, s), ZB_CG // cb)),
                  pl.BlockSpec((CONV_WIDTH, D_CONV), lambda bb, s: (0, 0)),
                  pl.BlockSpec((1, D_CONV), lambda bb, s: (0, 0)),
                  pl.BlockSpec((1, D_CONV), lambda bb, s: (0, 0)),
                  pl.BlockSpec((1, D_CONV), lambda bb, s: (0, 0))],
        out_specs=pl.BlockSpec((ts, D_CONV), lambda bb, s: (bb * ns + s, 0)),
        out_shape=jax.ShapeDtypeStruct((m, D_CONV), BF16),
        scratch_shapes=[pltpu.VMEM((HALO + ts, D_CONV), F32),
                        pltpu.VMEM((SUBLANES - 1, HALO + ts, D_CONV), F32)],
        compiler_params=_cparams(("parallel", "parallel")),
        name="conformer_conv",
    )(z, z, z, z, w, b, lg, lb)


def _outproj_kernel(x_ref, mod_ref, fox_ref, oc_ref, os_ref, ow_ref, cv_ref, w_ref, o_ref):
    d_fox = N_FOX * HEAD_DIM
    d_nsa = N_NSA * HEAD_DIM
    nsa = (oc_ref[...] + os_ref[...] + ow_ref[...]).astype(BF16)
    acc = _dot(fox_ref[...], w_ref[0:d_fox, :])
    acc = acc + _dot(nsa, w_ref[d_fox:d_fox + d_nsa, :])
    acc = acc + _dot(cv_ref[...], w_ref[d_fox + d_nsa:, :])
    o_ref[...] = x_ref[...] + mod_ref[2:3, :] * acc


def _outproj(x2, mod_l, fox, oc, osel, owin, conv, w, layer, seq):
    m, d = x2.shape
    tm = TM_OUT
    row = lambda i: (i, 0)
    return pl.pallas_call(
        _outproj_kernel,
        grid=(m // tm,),
        in_specs=[pl.BlockSpec((tm, d), row),
                  pl.BlockSpec((None, 6, d), lambda i: (i * tm // seq, 0, 0)),
                  pl.BlockSpec((tm, N_FOX * HEAD_DIM), row),
                  pl.BlockSpec((tm, N_NSA * HEAD_DIM), row),
                  pl.BlockSpec((tm, N_NSA * HEAD_DIM), row),
                  pl.BlockSpec((tm, N_NSA * HEAD_DIM), row),
                  pl.BlockSpec((tm, D_CONV), row),
                  pl.BlockSpec((None, d, d), lambda i: (layer, 0, 0))],
        out_specs=pl.BlockSpec((tm, d), row),
        out_shape=jax.ShapeDtypeStruct((m, d), F32),
        compiler_params=_cparams(("parallel",)),
        name="outproj",
    )(x2, mod_l, fox, oc, osel, owin, conv, w)


def _mlp_kernel(x_ref, mod_ref, g_ref, w1_ref, w2_ref, fg_ref, o_ref, h_ref, *, tm, final):
    f = pl.program_id(1)

    @pl.when(f == 0)
    def _():
        def chunk(c, carry):
            r = pl.multiple_of(c * ROW_CHUNK, ROW_CHUNK)
            h = _norm_mod(x_ref[pl.ds(r, ROW_CHUNK), :], g_ref[...], mod_ref[4:5, :], mod_ref[3:4, :])
            h_ref[pl.ds(r, ROW_CHUNK), :] = h.astype(BF16)
            return carry
        lax.fori_loop(0, tm // ROW_CHUNK, chunk, 0)
        o_ref[...] = jnp.zeros_like(o_ref)

    a = jnp.maximum(_dot(h_ref[...], w1_ref[...]), 0.0)
    o_ref[...] += _dot((a * a).astype(BF16), w2_ref[...])

    @pl.when(f == pl.num_programs(1) - 1)
    def _():
        def chunk(c, carry):
            r = pl.multiple_of(c * ROW_CHUNK, ROW_CHUNK)
            y = x_ref[pl.ds(r, ROW_CHUNK), :] + mod_ref[5:6, :] * o_ref[pl.ds(r, ROW_CHUNK), :]
            if final:
                y = y * lax.rsqrt(jnp.mean(y * y, axis=-1, keepdims=True) + EPS) * fg_ref[...]
            o_ref[pl.ds(r, ROW_CHUNK), :] = y
            return carry
        lax.fori_loop(0, tm // ROW_CHUNK, chunk, 0)


def _mlp(x2, mod_l, g, w1, w2, fg, layer, seq, final):
    m, d = x2.shape
    tm, tf = TM_MLP, TF_MLP
    dff = w1.shape[2]
    return pl.pallas_call(
        functools.partial(_mlp_kernel, tm=tm, final=final),
        grid=(m // tm, dff // tf),
        in_specs=[pl.BlockSpec((tm, d), lambda i, f: (i, 0), pipeline_mode=pl.Buffered(1)),
                  pl.BlockSpec((None, 6, d), lambda i, f: (i * tm // seq, 0, 0)),
                  pl.BlockSpec((1, d), lambda i, f: (0, 0)),
                  pl.BlockSpec((None, d, tf), lambda i, f: (layer, 0, f)),
                  pl.BlockSpec((None, tf, d), lambda i, f: (layer, f, 0)),
                  pl.BlockSpec((1, d), lambda i, f: (0, 0))],
        out_specs=pl.BlockSpec((tm, d), lambda i, f: (i, 0)),
        out_shape=jax.ShapeDtypeStruct((m, d), F32),
        scratch_shapes=[pltpu.VMEM((tm, d), BF16)],
        compiler_params=_cparams(("parallel", "arbitrary")),
        name="mlp_final" if final else "mlp",
    )(x2, mod_l, g, w1, w2, fg)


def _prep_inproj_weights(w_in):
    depth, d, _ = w_in.shape
    def sl(name, width, off=0, scale=None):
        piece = w_in[:, :, _SRC[name] + off:_SRC[name] + off + width]
        return (piece if scale is None else piece * scale).astype(BF16)
    qs = HEAD_DIM ** -0.5 * LOG2E
    pieces = [sl("fq", 768, scale=qs), sl("fk", 768), sl("fv", 768), sl("nq", 768, scale=qs),
              sl("ca", D_CONV), sl("cg", D_CONV)]
    for kname, vname in (("kc", "vc"), ("ks", "vs"), ("kw", "vw")):
        for g in range(N_KV):
            pieces += [sl(kname, HEAD_DIM, g * HEAD_DIM), sl(vname, HEAD_DIM, g * HEAD_DIM)]
    pieces.append(jnp.zeros((depth, d, LANES), BF16))
    wz = jnp.concatenate(pieces, axis=2)
    assert wz.shape[2] == NZ
    ws = jnp.concatenate([sl("ff", N_FOX), sl("ng", 3 * N_NSA),
                          jnp.zeros((depth, d, LANES - N_FOX - 3 * N_NSA), BF16)], axis=2)
    return wz, ws


def _prep_compress_weights(w_cmp_k, w_cmp_v, pos_cmp):
    depth = w_cmp_k.shape[0]
    slots = 2
    wkv = jnp.stack([w_cmp_k, w_cmp_v], axis=1)
    big = jnp.einsum("zklde,kq->zlkdqe", wkv, jnp.eye(2, dtype=F32))
    big = big.reshape(depth, CMP_LEN, slots * HEAD_DIM, slots * HEAD_DIM)
    half = CMP_LEN // 2
    kdim = half * slots * HEAD_DIM
    wlo = big[:, :half].reshape(depth, kdim, slots * HEAD_DIM).astype(BF16)
    whi = big[:, half:].reshape(depth, kdim, slots * HEAD_DIM).astype(BF16)
    pos = jnp.broadcast_to(pos_cmp[:, :, None, :], (depth, CMP_LEN, slots, HEAD_DIM))
    plo = jnp.zeros((depth, 8, kdim), F32).at[:, 0].set(pos[:, :half].reshape(depth, kdim)).astype(BF16)
    phi = jnp.zeros((depth, 8, kdim), F32).at[:, 0].set(pos[:, half:].reshape(depth, kdim)).astype(BF16)
    return wlo, whi, plo, phi


def _selection_constants(seq):
    nc = seq // CMP_STRIDE
    n_cmp = (seq - CMP_LEN) // CMP_STRIDE + 1
    n = np.arange(nc)[:, None]
    j = np.arange(LANES)[None, :]
    overlap = ((n * CMP_STRIDE < j * SEL_LEN + SEL_LEN) & (n * CMP_STRIDE + CMP_LEN - 1 >= j * SEL_LEN)
               & (n < n_cmp) & (j < seq // SEL_LEN))
    emat = (np.arange(seq)[:, None] // SEL_LEN + HALF) == np.arange(LANES)[None, :]
    return jnp.asarray(overlap, BF16), jnp.asarray(emat, BF16)


def kernel(x, c, w_mod, b_mod, norm1_g, w_in, b_f, w_cmp_k, w_cmp_v, pos_cmp, conv_w, conv_b, conv_ln_g,
           conv_ln_b, w_out, norm2_g, w_mlp1, w_mlp2, rel_bias, final_g):
    batch, seq, d = x.shape
    depth = w_in.shape[0]
    assert d == D_MODEL and seq % TM_INPROJ == 0 and seq // SEL_LEN <= SEL_LEN

    wz, ws = _prep_inproj_weights(w_in)
    wlo, whi, plo, phi = _prep_compress_weights(w_cmp_k, w_cmp_v, pos_cmp)
    w_out_b = w_out.astype(BF16)
    w1_b = w_mlp1.astype(BF16)
    w2_b = w_mlp2.astype(BF16)
    overlap, emat = _selection_constants(seq)
    bf_rows = jnp.zeros((depth, 1, LANES), F32).at[:, 0, SMALL_FF:SMALL_FF + N_FOX].set(b_f)

    mod = _modulation(c, w_mod, b_mod)
    fvec = _bias_by_distance(rel_bias)
    tiles = _bias_tiles(fvec)
    fg = final_g.reshape(1, d)

    x2 = x.reshape(batch * seq, d)
    kv0 = ZB_KVC * LANES
    for l in range(depth):
        z, small = _inproj(x2, mod[l], norm1_g[l].reshape(1, d), wz, ws, l, seq)
        fcol, frow = _fox_cumsum(small, bf_rows[l], batch, seq)
        o_fox = _fox_attention(z, fcol, frow, batch, seq)
        r = z[:, kv0:kv0 + N_KV * LANES].reshape(batch, seq // CMP_STRIDE, CMP_STRIDE, N_KV, LANES)
        r = r.transpose(0, 3, 1, 2, 4).reshape(batch, N_KV, seq // CMP_STRIDE, CMP_STRIDE * LANES)
        kvcmp = _compress(r, wlo[l], whi[l], plo[l], phi[l])
        o_cmp, sel, gsig = _nsa_compressed(z, kvcmp, small, fvec, overlap, batch, seq)
        o_sel = _nsa_flash(z, sel, emat, gsig, tiles, batch, seq, windowed=False)
        o_win = _nsa_flash(z, sel, emat, gsig, tiles, batch, seq, windowed=True)
        o_conv = _conformer_conv(z, conv_w[l], conv_b[l].reshape(1, -1), conv_ln_g[l].reshape(1, -1),
                                 conv_ln_b[l].reshape(1, -1), batch, seq)
        x2 = _outproj(x2, mod[l], o_fox, o_cmp, o_sel, o_win, o_conv, w_out_b, l, seq)
        x2 = _mlp(x2, mod[l], norm2_g[l].reshape(1, d), w1_b, w2_b, fg, l, seq, final=(l == depth - 1))
    return x2.reshape(batch, seq, d)
```

```python
import functools
import math

import numpy as np
import jax
import jax.numpy as jnp
from jax import lax
from jax.experimental import pallas as pl
from jax.experimental.pallas import tpu as pltpu

F32 = jnp.float32
BF16 = jnp.bfloat16
HIGHEST = lax.Precision.HIGHEST

D_MODEL = 2048
HEAD_DIM = 64
N_FOX = 12
N_NSA = 12
N_KV = 3
GROUP = 4
D_CONV = 512
D_FF = 4 * D_MODEL
CMP_LEN = 32
CMP_STRIDE = 16
SEL_LEN = 64
N_SEL = 16
WINDOW = 512
CONV_WIDTH = 31
N_BUCKETS = 32
MAX_DISTANCE = 128
EPS = 1e-6
NEG = -1e30
BIG = 1e9
LOG2E = math.log2(math.e)

LANES = 128
SUBLANES = 8
HALF = LANES // 2

ZB_FQ, ZB_FK, ZB_FV, ZB_NQ = 0, 6, 12, 18
ZB_CA, ZB_CG = 24, 28
ZB_KVC, ZB_KVS, ZB_KVW = 32, 35, 38
NZ_BLOCKS = 42
NZ = NZ_BLOCKS * LANES
SMALL_FF, SMALL_NG = 0, 12

_SRC = dict(fq=0, fk=768, fv=1536, ff=2304, nq=2316, kc=3084, vc=3276, ks=3468, vs=3660,
            kw=3852, vw=4044, ng=4236, ca=4272, cg=4784)

TM_INPROJ = 1024
TN_INPROJ = 768
TM_OUT = 512
TM_MLP = 1024
TF_MLP = 512
ROW_CHUNK = 128
T_FOX = 512
NSUB_FOX = 4
T_NSA = 256
NSUB_NSA = 4
TQ_CMP = 512
TS_CONV = 512
CONV_ROWS = 64
TS_CUM = 512
HALO = 32
VMEM_LIMIT = 56 * 1024 * 1024


def _cparams(sem):
    return pltpu.CompilerParams(dimension_semantics=sem, vmem_limit_bytes=VMEM_LIMIT)


def _dot(a, b):
    return jnp.dot(a, b, preferred_element_type=F32)


def _dot_nt(a, b):
    return lax.dot_general(a, b, (((1,), (1,)), ((), ())), preferred_element_type=F32)


def _bucket_thresholds():
    d = np.arange(MAX_DISTANCE, dtype=np.int32)
    max_exact = N_BUCKETS // 2
    nf = np.maximum(d, 1).astype(np.float32)
    large = max_exact + (np.log(nf / np.float32(max_exact)) / np.float32(math.log(MAX_DISTANCE / max_exact))
                         * np.float32(N_BUCKETS - max_exact)).astype(np.int32)
    large = np.minimum(large, N_BUCKETS - 1)
    bucket = np.where(d < max_exact, d, large)
    assert bucket[-1] == N_BUCKETS - 1 and np.all(np.diff(bucket) >= 0)
    return [int(np.argmax(bucket >= k)) for k in range(N_BUCKETS)]


_T5_THRESH = _bucket_thresholds()


def _fvec_kernel(tbl_ref, out_ref):
    d = lax.broadcasted_iota(jnp.int32, out_ref.shape, 1)
    out = jnp.broadcast_to(tbl_ref[:, 0:1], out_ref.shape)
    for k in range(1, N_BUCKETS):
        out = jnp.where(d >= _T5_THRESH[k], tbl_ref[:, k:k + 1], out)
    out_ref[...] = out * LOG2E


def _bias_by_distance(rel_bias):
    tbl = jnp.zeros((16, LANES), F32).at[:N_NSA, :N_BUCKETS].set(rel_bias.astype(F32).T)
    return pl.pallas_call(
        _fvec_kernel, out_shape=jax.ShapeDtypeStruct((16, LANES), F32), name="t5_bias_by_distance",
    )(tbl)


def _mod_kernel(ct_ref, w_ref, b_ref, o_ref, *, nb):
    c = ct_ref[...]
    ca = c * jax.nn.sigmoid(c)
    w = w_ref[...]
    rows = [jnp.sum(w * ca[:, b:b + 1], axis=0, keepdims=True) for b in range(nb)]
    rows.append(jnp.zeros((SUBLANES - nb, w.shape[1]), F32))
    o_ref[...] = jnp.concatenate(rows, axis=0) + b_ref[...]


def _modulation(c, w_mod, b_mod):
    depth, d, n = w_mod.shape
    b = c.shape[0]
    assert b < SUBLANES
    tn = 1024
    ct = jnp.zeros((d, LANES), F32).at[:, :b].set(c.T)
    out = pl.pallas_call(
        functools.partial(_mod_kernel, nb=b),
        grid=(depth, n // tn),
        in_specs=[pl.BlockSpec((d, LANES), lambda l, j: (0, 0)),
                  pl.BlockSpec((None, d, tn), lambda l, j: (l, 0, j)),
                  pl.BlockSpec((None, 1, tn), lambda l, j: (l, 0, j))],
        out_specs=pl.BlockSpec((None, SUBLANES, tn), lambda l, j: (l, 0, j)),
        out_shape=jax.ShapeDtypeStruct((depth, SUBLANES, n), F32),
        compiler_params=_cparams(("parallel", "parallel")),
        name="adaln_modulation",
    )(ct, w_mod, b_mod.reshape(depth, 1, n))
    return out[:, :b].reshape(depth, b, 6, d)


def _norm_mod(x, g, sc, sh):
    y = x * lax.rsqrt(jnp.mean(x * x, axis=-1, keepdims=True) + EPS) * g
    return y * (1.0 + sc) + sh


def _inproj_kernel(x_ref, mod_ref, g_ref, w_ref, ws_ref, z_ref, sm_ref, h_ref, *, tm):
    @pl.when(pl.program_id(1) == 0)
    def _():
        def chunk(c, carry):
            r = pl.multiple_of(c * ROW_CHUNK, ROW_CHUNK)
            h = _norm_mod(x_ref[pl.ds(r, ROW_CHUNK), :], g_ref[...], mod_ref[1:2, :], mod_ref[0:1, :])
            hb = h.astype(BF16)
            h_ref[pl.ds(r, ROW_CHUNK), :] = hb
            sm_ref[pl.ds(r, ROW_CHUNK), :] = _dot(hb, ws_ref[...])
            return carry
        lax.fori_loop(0, tm // ROW_CHUNK, chunk, 0)

    z_ref[...] = _dot(h_ref[...], w_ref[...]).astype(BF16)


def _inproj(x2, mod_l, g, wz, ws, layer, seq):
    m, d = x2.shape
    tm, tn = TM_INPROJ, TN_INPROJ
    assert seq % tm == 0 and NZ % tn == 0
    return pl.pallas_call(
        functools.partial(_inproj_kernel, tm=tm),
        grid=(m // tm, NZ // tn),
        in_specs=[pl.BlockSpec((tm, d), lambda i, j: (i, 0)),
                  pl.BlockSpec((None, 6, d), lambda i, j: (i * tm // seq, 0, 0)),
                  pl.BlockSpec((1, d), lambda i, j: (0, 0)),
                  pl.BlockSpec((None, d, tn), lambda i, j: (layer, 0, j)),
                  pl.BlockSpec((None, d, LANES), lambda i, j: (layer, 0, 0))],
        out_specs=[pl.BlockSpec((tm, tn), lambda i, j: (i, j)),
                   pl.BlockSpec((tm, LANES), lambda i, j: (i, 0))],
        out_shape=[jax.ShapeDtypeStruct((m, NZ), BF16), jax.ShapeDtypeStruct((m, LANES), F32)],
        scratch_shapes=[pltpu.VMEM((tm, d), BF16)],
        compiler_params=_cparams(("parallel", "arbitrary")),
        name="inproj",
    )(x2, mod_l, g, wz, ws)


def _foxcum_kernel(sm_ref, bf_ref, fcol_ref, frow_ref, carry_ref, *, ts):
    @pl.when(pl.program_id(1) == 0)
    def _():
        carry_ref[...] = jnp.zeros_like(carry_ref)

    x = sm_ref[...] + bf_ref[...]
    ls = (jnp.minimum(x, 0.0) - jnp.log1p(jnp.exp(-jnp.abs(x)))) * LOG2E
    r = lax.broadcasted_iota(jnp.int32, (ts, ts), 0)
    c = lax.broadcasted_iota(jnp.int32, (ts, ts), 1)
    tri = jnp.where(r >= c, 1.0, 0.0).astype(F32)
    cum = jnp.dot(tri, ls, precision=HIGHEST, preferred_element_type=F32) + carry_ref[0:1, :]
    carry_ref[0:1, :] = cum[ts - 1:ts, :]
    cum_t = cum.T
    for hp in range(N_FOX // 2):
        fcol_ref[hp] = cum if hp == 0 else pltpu.roll(cum, LANES - 2 * hp, 1)
        frow_ref[hp, 0:2, :] = cum_t[2 * hp:2 * hp + 2, :]
        frow_ref[hp, 2:8, :] = jnp.zeros((6, ts), F32)


def _fox_cumsum(small, bf_row, batch, seq):
    ts = TS_CUM
    ns = seq // ts
    npair = N_FOX // 2
    return pl.pallas_call(
        functools.partial(_foxcum_kernel, ts=ts),
        grid=(batch, ns),
        in_specs=[pl.BlockSpec((ts, LANES), lambda b, s: (b * ns + s, 0)),
                  pl.BlockSpec((1, LANES), lambda b, s: (0, 0))],
        out_specs=[pl.BlockSpec((None, npair, ts, LANES), lambda b, s: (b, 0, s, 0)),
                   pl.BlockSpec((None, npair, 8, ts), lambda b, s: (b, 0, 0, s))],
        out_shape=[jax.ShapeDtypeStruct((batch, npair, seq, LANES), F32),
                   jax.ShapeDtypeStruct((batch, npair, 8, seq), F32)],
        scratch_shapes=[pltpu.VMEM((8, LANES), F32)],
        compiler_params=_cparams(("parallel", "arbitrary")),
        name="fox_cumsum",
    )(small, bf_row)


def _fox_kernel(q_ref, k_ref, v_ref, fc_ref, fr_ref, o_ref, m_ref, fq_ref, acc_ref, *, t, nsub):
    qb = pl.program_id(2)
    kb = pl.program_id(3)
    tb = nsub * t

    @pl.when(kb == 0)
    def _():
        m_ref[...] = jnp.full(m_ref.shape, NEG, F32)
        acc_ref[...] = jnp.zeros_like(acc_ref)
        for h in range(2):
            fq_ref[h] = jnp.broadcast_to(fc_ref[:, h:h + 1], (tb, LANES))

    lane = lax.broadcasted_iota(jnp.int32, (1, LANES), 1)

    def sub_tile(i, j, diagonal):
        rows = slice(i * t, (i + 1) * t)
        cols = slice(j * t, (j + 1) * t)
        q2 = q_ref[rows, :]
        k2 = k_ref[cols, :]
        v2 = v_ref[cols, :]
        if diagonal:
            causal = (lax.broadcasted_iota(jnp.int32, (t, t), 1) <= lax.broadcasted_iota(jnp.int32, (t, t), 0))
        zero = jnp.zeros_like(q2)
        s2 = _dot_nt(jnp.concatenate([jnp.where(lane < HALF, q2, zero), jnp.where(lane >= HALF, q2, zero)], axis=0), k2)
        for h in range(2):
            keep = (lane < HALF) if h == 0 else (lane >= HALF)
            vh = jnp.where(keep, v2, jnp.ones_like(v2))
            s = s2[h * t:(h + 1) * t, :] - fr_ref[h:h + 1, cols]
            if diagonal:
                s = jnp.where(causal, s, NEG)
            fq = fq_ref[h, rows, :]
            m_old = m_ref[h, rows, :]
            m_new = jnp.maximum(m_old, jnp.max(s, axis=1, keepdims=True) + fq)
            shift = jnp.tile(fq - m_new, (1, t // LANES))
            p = jnp.exp2(s + shift)
            acc_ref[h, rows, :] = jnp.exp2(m_old - m_new) * acc_ref[h, rows, :] + _dot(p.astype(BF16), vh)
            m_ref[h, rows, :] = m_new

    @pl.when(kb < qb)
    def _():
        for i in range(nsub):
            for j in range(nsub):
                sub_tile(i, j, False)

    @pl.when(kb == qb)
    def _():
        for i in range(nsub):
            for j in range(i + 1):
                sub_tile(i, j, i == j)
        a0 = acc_ref[0]
        a1 = acc_ref[1]
        o_ref[...] = jnp.where(lane < HALF, a0 / pltpu.roll(a0, HALF, 1),
                               a1 / pltpu.roll(a1, HALF, 1)).astype(BF16)


def _fox_attention(z, fcol, frow, batch, seq):
    t, nsub = T_FOX, NSUB_FOX
    tb = t * nsub
    nb = seq // tb
    npair = N_FOX // 2
    m = batch * seq
    kv_row = lambda b, hp, qb, kb: b * nb + jnp.minimum(kb, qb)
    return pl.pallas_call(
        functools.partial(_fox_kernel, t=t, nsub=nsub),
        grid=(batch, npair, nb, nb),
        in_specs=[pl.BlockSpec((tb, LANES), lambda b, hp, qb, kb: (b * nb + qb, ZB_FQ + hp)),
                  pl.BlockSpec((tb, LANES), lambda b, hp, qb, kb: (kv_row(b, hp, qb, kb), ZB_FK + hp)),
                  pl.BlockSpec((tb, LANES), lambda b, hp, qb, kb: (kv_row(b, hp, qb, kb), ZB_FV + hp)),
                  pl.BlockSpec((None, None, tb, LANES), lambda b, hp, qb, kb: (b, hp, qb, 0)),
                  pl.BlockSpec((None, None, 8, tb), lambda b, hp, qb, kb: (b, hp, 0, jnp.minimum(kb, qb)))],
        out_specs=pl.BlockSpec((tb, LANES), lambda b, hp, qb, kb: (b * nb + qb, hp)),
        out_shape=jax.ShapeDtypeStruct((m, N_FOX * HEAD_DIM), BF16),
        scratch_shapes=[pltpu.VMEM((2, tb, LANES), F32), pltpu.VMEM((2, tb, LANES), F32),
                        pltpu.VMEM((2, tb, LANES), F32)],
        compiler_params=_cparams(("parallel", "parallel", "parallel", "arbitrary")),
        name="fox_attention",
    )(z, z, z, fcol, frow)


def _compress_kernel(r_ref, wlo_ref, whi_ref, plo_ref, phi_ref, o_ref, *, nc):
    bias = _dot(plo_ref[...], wlo_ref[...]) + _dot(phi_ref[...], whi_ref[...])
    for g in range(N_KV):
        r = r_ref[g]
        out = _dot(r, wlo_ref[...]) + pltpu.roll(_dot(r, whi_ref[...]), nc - 1, 0) + bias[0:1, :]
        o_ref[:, g * LANES:(g + 1) * LANES] = out.astype(BF16)


def _compress(r, wlo, whi, plo, phi):
    batch, ng, nc, k = r.shape
    return pl.pallas_call(
        functools.partial(_compress_kernel, nc=nc),
        grid=(batch,),
        in_specs=[pl.BlockSpec((None, ng, nc, k), lambda b: (b, 0, 0, 0)),
                  pl.BlockSpec((k, LANES), lambda b: (0, 0)),
                  pl.BlockSpec((k, LANES), lambda b: (0, 0)),
                  pl.BlockSpec((8, k), lambda b: (0, 0)),
                  pl.BlockSpec((8, k), lambda b: (0, 0))],
        out_specs=pl.BlockSpec((None, nc, ng * LANES), lambda b: (b, 0, 0)),
        out_shape=jax.ShapeDtypeStruct((batch, nc, ng * LANES), BF16),
        compiler_params=_cparams(("parallel",)),
        name="nsa_compress",
    )(r, wlo, whi, plo, phi)


def _head_queries(qf, lane):
    zero = jnp.zeros_like(qf)
    qa = jnp.where(lane < HALF, qf, zero).astype(BF16)
    qb = jnp.where(lane < HALF, pltpu.roll(qf, HALF, 1), zero).astype(BF16)
    return qa, qb


def _gather_bias(frow, idx, tq):
    tab = jnp.broadcast_to(frow, (tq, LANES))
    parts = [jnp.take_along_axis(tab, idx[:, c * LANES:(c + 1) * LANES], axis=1)
             for c in range(idx.shape[1] // LANES)]
    return parts[0] if len(parts) == 1 else jnp.concatenate(parts, axis=1)


def _nsacmp_kernel(q_ref, kvc_ref, sm_ref, fv_ref, ov_ref, oc_ref, sel_ref, gs_ref, val_ref, *, tq, nc):
    qi = pl.program_id(1)
    t0 = qi * tq
    lane = lax.broadcasted_iota(jnp.int32, (1, LANES), 1)
    t_col = t0 + lax.broadcasted_iota(jnp.int32, (tq, 1), 0)
    any_valid = t_col >= CMP_LEN - 1

    gs = jax.nn.sigmoid(sm_ref[...])
    n_sel = SEL_LEN
    j_row = lax.broadcasted_iota(jnp.int32, (n_sel, tq), 0)
    sub_iota = lax.broadcasted_iota(jnp.int32, (SUBLANES, tq), 0)
    t_row = t0 + lax.broadcasted_iota(jnp.int32, (n_sel, tq), 1)
    cur = jnp.right_shift(t_row, SEL_LEN.bit_length() - 1)
    forced = (j_row == 0) | (j_row == cur) | (j_row == cur - 1)
    valid_s = j_row * SEL_LEN <= t_row

    def attend(ncols):
        cmp_end = lax.broadcasted_iota(jnp.int32, (1, ncols), 1) * CMP_STRIDE + (CMP_LEN - 1)
        dist = t_col - cmp_end
        masked = jnp.where(dist >= 0, 0.0, NEG)
        idx = jnp.clip(dist, 0, MAX_DISTANCE - 1)
        for g in range(N_KV):
            kv = kvc_ref[0:ncols, g * LANES:(g + 1) * LANES]
            gs_g = pltpu.roll(gs, LANES - (SMALL_NG + g * GROUP * 3), 1)
            gs_ref[:, g * LANES:(g + 1) * LANES] = gs_g
            psum = jnp.zeros((tq, ncols), F32)
            outs = []
            for pair in range(2):
                c0 = (g * 2 + pair) * LANES
                qpair = _head_queries(q_ref[:, c0:c0 + LANES].astype(F32), lane)
                for sub in range(2):
                    r = pair * 2 + sub
                    h = g * GROUP + r
                    s = _dot_nt(qpair[sub], kv) + _gather_bias(fv_ref[h:h + 1, :], idx, tq) + masked
                    e = jnp.exp2(s - jnp.max(s, axis=1, keepdims=True))
                    p = e * jnp.where(any_valid, 1.0 / jnp.sum(e, axis=1, keepdims=True), 0.0)
                    psum = psum + p
                    o = _dot(p.astype(BF16), kv)
                    outs.append(o * gs_g[:, r * 3:r * 3 + 1])
                oc_ref[:, c0:c0 + LANES] = jnp.where(lane < HALF, pltpu.roll(outs[-2], HALF, 1), outs[-1])

            p_hi = psum.astype(BF16)
            p_lo = (psum - p_hi.astype(F32)).astype(BF16)
            imp = _dot(p_hi, ov_ref[0:ncols, :]) + _dot(p_lo, ov_ref[0:ncols, :])
            val_ref[g] = jnp.where(forced, BIG, jnp.where(valid_s, imp.T[0:n_sel, :], NEG))

    n_low = (CMP_STRIDE * LANES + CMP_LEN - 1) // tq
    if nc > LANES and n_low > 0:
        @pl.when(qi < n_low)
        def _():
            attend(LANES)

        @pl.when(qi >= n_low)
        def _():
            attend(nc)
    else:
        attend(nc)

    for g in range(N_KV):
        groups = [val_ref[g, SUBLANES * b:SUBLANES * (b + 1), :] for b in range(n_sel // SUBLANES)]
        counts = [jnp.zeros((SUBLANES, tq), F32) for _ in groups]
        for j in range(n_sel):
            jb, js = divmod(j, SUBLANES)
            row = groups[jb][js:js + 1, :]
            for b, grp in enumerate(groups):
                if b < jb:
                    ahead = row > grp
                elif b > jb:
                    ahead = row >= grp
                else:
                    ahead = (row > grp) | ((row == grp) & (js < sub_iota))
                counts[b] = counts[b] + jnp.where(ahead, 1.0, 0.0)
        cnt = jnp.concatenate(counts, axis=0)
        unchosen = jnp.where(cnt < float(N_SEL), 0.0, NEG)
        unchosen = jnp.concatenate([jnp.zeros((LANES - n_sel, tq), F32), unchosen], axis=0)
        sel_ref[:, g * LANES:(g + 1) * LANES] = unchosen.T.astype(BF16)


def _nsa_compressed(z, kvcmp, small, fvec, overlap, batch, seq):
    tq = TQ_CMP
    nq = seq // tq
    nc = kvcmp.shape[1]
    m = batch * seq
    w3 = N_KV * LANES
    return pl.pallas_call(
        functools.partial(_nsacmp_kernel, tq=tq, nc=nc),
        grid=(batch, nq),
        in_specs=[pl.BlockSpec((tq, N_NSA * HEAD_DIM), lambda b, i: (b * nq + i, ZB_NQ * LANES // (N_NSA * HEAD_DIM))),
                  pl.BlockSpec((None, nc, w3), lambda b, i: (b, 0, 0)),
                  pl.BlockSpec((tq, LANES), lambda b, i: (b * nq + i, 0)),
                  pl.BlockSpec((16, LANES), lambda b, i: (0, 0)),
                  pl.BlockSpec((nc, LANES), lambda b, i: (0, 0))],
        out_specs=[pl.BlockSpec((tq, N_NSA * HEAD_DIM), lambda b, i: (b * nq + i, 0)),
                   pl.BlockSpec((tq, w3), lambda b, i: (b * nq + i, 0)),
                   pl.BlockSpec((tq, w3), lambda b, i: (b * nq + i, 0))],
        out_shape=[jax.ShapeDtypeStruct((m, N_NSA * HEAD_DIM), F32),
                   jax.ShapeDtypeStruct((m, w3), BF16),
                   jax.ShapeDtypeStruct((m, w3), F32)],
        scratch_shapes=[pltpu.VMEM((N_KV, SEL_LEN, tq), F32)],
        compiler_params=_cparams(("parallel", "parallel")),
        name="nsa_compressed",
    )(z, kvcmp, small, fvec, overlap)


def _nsaflash_kernel(*refs, t, nsub, nkk, windowed, gate_col):
    if windowed:
        qa_ref, qb_ref, kv_ref, gs_ref, tb_ref, o_ref, q4_ref, m_ref, acc_ref = refs
        sel_ref = e_ref = None
    else:
        qa_ref, qb_ref, kv_ref, sel_ref, e_ref, gs_ref, tb_ref, o_ref, q4_ref, m_ref, acc_ref = refs
    g = pl.program_id(1)
    qb = pl.program_id(2)
    kk = pl.program_id(3)
    far = tb_ref.shape[0] - 1
    lane = lax.broadcasted_iota(jnp.int32, (1, LANES), 1)

    @pl.when(kk == 0)
    def _():
        m_ref[...] = jnp.full(m_ref.shape, NEG, F32)
        acc_ref[...] = jnp.zeros_like(acc_ref)
        for i in range(nsub):
            rows = slice(i * t, (i + 1) * t)
            queries = (_head_queries(qa_ref[rows, :].astype(F32), lane)
                       + _head_queries(qb_ref[rows, :].astype(F32), lane))
            for r in range(GROUP):
                q = queries[r] if windowed else jnp.where(lane < HALF, queries[r], sel_ref[rows, :])
                q4_ref[(i * GROUP + r) * t:(i * GROUP + r + 1) * t, :] = q

    def sub_tile(i, j, rel):
        tidx = rel if (windowed or rel <= 1) else far
        kv = kv_ref[j * t:(j + 1) * t, :]
        ones_v = jnp.where(lane < HALF, jnp.ones_like(kv), kv)
        keys = kv if windowed else jnp.where(lane < HALF, kv, e_ref[j * t:(j + 1) * t, :])
        s4 = _dot_nt(q4_ref[i * GROUP * t:(i + 1) * GROUP * t, :], keys)
        for pair in range(2):
            base = (i * GROUP + 2 * pair) * t
            rows2 = slice(base, base + 2 * t)
            probs, alphas = [], []
            for sub in range(2):
                r = 2 * pair + sub
                h = g * GROUP + r
                rows = slice(base + sub * t, base + (sub + 1) * t)
                m_old = m_ref[rows, :]
                s = s4[r * t:(r + 1) * t, :] + tb_ref[tidx, h]
                m_new = jnp.maximum(m_old, jnp.max(s, axis=1, keepdims=True))
                p = jnp.exp2(s - jnp.tile(m_new, (1, t // LANES)))
                m_ref[rows, :] = m_new
                alphas.append(jnp.exp2(m_old - m_new))
                probs.append(p.astype(BF16))
            acc_ref[rows2, :] = (jnp.concatenate(alphas, axis=0) * acc_ref[rows2, :]
                                 + _dot(jnp.concatenate(probs, axis=0), ones_v))

    def block(offset):
        max_rel = far - 1 if windowed else None
        for j in range(nsub):
            for i in range(nsub):
                rel = offset * nsub + i - j
                if rel >= 0 and (max_rel is None or rel <= max_rel):
                    sub_tile(i, j, rel)

    if windowed:
        for c in range(nkk):
            @pl.when((kk == c) & (qb - (nkk - 1) + c >= 0))
            def _(c=c):
                block(nkk - 1 - c)
        last = kk == nkk - 1
    else:
        @pl.when(qb - kk == 0)
        def _():
            block(0)

        @pl.when(qb - kk == 1)
        def _():
            block(1)

        @pl.when(qb - kk >= 2)
        def _():
            block(2)
        last = kk == qb

    @pl.when(last)
    def _():
        for i in range(nsub):
            rows = slice(i * t, (i + 1) * t)
            gs = gs_ref[rows, :]
            o = []
            for r in range(GROUP):
                a = acc_ref[(i * GROUP + r) * t:(i * GROUP + r + 1) * t, :]
                o.append(a / pltpu.roll(a, HALF, 1) * gs[:, r * 3 + gate_col:r * 3 + gate_col + 1])
            o_ref[rows, 0:LANES] = jnp.where(lane < HALF, pltpu.roll(o[0], HALF, 1), o[1])
            o_ref[rows, LANES:2 * LANES] = jnp.where(lane < HALF, pltpu.roll(o[2], HALF, 1), o[3])


def _bias_tile_kernel(fv_ref, o_ref, *, t, nrel):
    rel = pl.program_id(0)
    h = pl.program_id(1)
    dist = rel * t + lax.broadcasted_iota(jnp.int32, (t, t), 0) - lax.broadcasted_iota(jnp.int32, (t, t), 1)
    bias = _gather_bias(fv_ref[pl.ds(h, 1), :], jnp.clip(dist, 0, MAX_DISTANCE - 1), t)
    o_ref[...] = jnp.where((dist >= 0) & ((dist < WINDOW) | (rel == nrel)), bias, NEG)


def _bias_tiles(fvec):
    t = T_NSA
    nrel = WINDOW // t + 1
    return pl.pallas_call(
        functools.partial(_bias_tile_kernel, t=t, nrel=nrel),
        grid=(nrel + 1, N_NSA),
        in_specs=[pl.BlockSpec((16, LANES), lambda r, h: (0, 0))],
        out_specs=pl.BlockSpec((None, None, t, t), lambda r, h: (r, h, 0, 0)),
        out_shape=jax.ShapeDtypeStruct((nrel + 1, N_NSA, t, t), F32),
        compiler_params=_cparams(("parallel", "parallel")),
        name="nsa_bias_tiles",
    )(fvec)


def _nsa_flash(z, sel, emat, gsig, tiles, batch, seq, windowed):
    t, nsub = T_NSA, NSUB_NSA
    tb = t * nsub
    nb = seq // tb
    m = batch * seq
    max_rel = WINDOW // t
    nkk = ((max_rel + nsub - 1) // nsub + 1) if windowed else nb
    assert t >= MAX_DISTANCE and 2 * t <= WINDOW and WINDOW % t == 0 and seq % tb == 0
    zb_kv = ZB_KVW if windowed else ZB_KVS
    if windowed:
        kidx = lambda qb, kk: jnp.maximum(qb - (nkk - 1) + kk, 0)
    else:
        kidx = lambda qb, kk: jnp.minimum(kk, qb)
    in_specs = [pl.BlockSpec((tb, LANES), lambda b, g, qb, kk: (b * nb + qb, ZB_NQ + 2 * g)),
                pl.BlockSpec((tb, LANES), lambda b, g, qb, kk: (b * nb + qb, ZB_NQ + 2 * g + 1)),
                pl.BlockSpec((tb, LANES), lambda b, g, qb, kk: (b * nb + kidx(qb, kk), zb_kv + g))]
    args = [z, z, z]
    if not windowed:
        in_specs += [pl.BlockSpec((tb, LANES), lambda b, g, qb, kk: (b * nb + qb, g)),
                     pl.BlockSpec((tb, LANES), lambda b, g, qb, kk: (kidx(qb, kk), 0))]
        args += [sel, emat]
    in_specs += [pl.BlockSpec((tb, LANES), lambda b, g, qb, kk: (b * nb + qb, g)),
                 pl.BlockSpec(tiles.shape, lambda b, g, qb, kk: (0, 0, 0, 0))]
    args += [gsig, tiles]
    return pl.pallas_call(
        functools.partial(_nsaflash_kernel, t=t, nsub=nsub, nkk=nkk, windowed=windowed,
                          gate_col=2 if windowed else 1),
        grid=(batch, N_KV, nb, nkk),
        in_specs=in_specs,
        out_specs=pl.BlockSpec((tb, 2 * LANES), lambda b, g, qb, kk: (b * nb + qb, g)),
        out_shape=jax.ShapeDtypeStruct((m, N_NSA * HEAD_DIM), F32),
        scratch_shapes=[pltpu.VMEM((nsub * GROUP * t, LANES), BF16), pltpu.VMEM((nsub * GROUP * t, LANES), F32),
                        pltpu.VMEM((nsub * GROUP * t, LANES), F32)],
        compiler_params=_cparams(("parallel", "parallel", "parallel", "arbitrary")),
        name="nsa_window" if windowed else "nsa_selected",
    )(*args)


def _conv_kernel(a_ref, g_ref, ah_ref, gh_ref, w_ref, b_ref, lg_ref, lb_ref, o_ref, u_ref, us_ref, *, ts):
    first = pl.program_id(1) == 0
    halo = ah_ref[...].astype(F32) * jax.nn.sigmoid(gh_ref[...].astype(F32))
    u_ref[0:HALO, :] = jnp.where(first, 0.0, halo)
    u_ref[HALO:HALO + ts, :] = a_ref[...].astype(F32) * jax.nn.sigmoid(g_ref[...].astype(F32))
    n_shift = HALO + ts - SUBLANES
    for s in range(1, SUBLANES):
        us_ref[s - 1, 0:n_shift, :] = u_ref[s:s + n_shift, :]
    base = HALO - (CONV_WIDTH - 1)
    for c in range(ts // CONV_ROWS):
        r0 = c * CONV_ROWS
        y = jnp.zeros((CONV_ROWS, D_CONV), F32)
        for k in range(CONV_WIDTH):
            start, s = divmod(base + k + r0, SUBLANES)
            start *= SUBLANES
            window = (u_ref[start:start + CONV_ROWS, :] if s == 0
                      else us_ref[s - 1, start:start + CONV_ROWS, :])
            y = y + window * w_ref[k:k + 1, :]
        y = y + b_ref[...]
        mu = jnp.mean(y, axis=-1, keepdims=True)
        yc = y - mu
        var = jnp.mean(yc * yc, axis=-1, keepdims=True)
        yn = yc * lax.rsqrt(var + EPS) * lg_ref[...] + lb_ref[...]
        o_ref[r0:r0 + CONV_ROWS, :] = (yn * jax.nn.sigmoid(yn)).astype(BF16)


def _conformer_conv(z, w, b, lg, lb, batch, seq):
    ts = TS_CONV
    ns = seq // ts
    m = batch * seq
    cb = D_CONV // LANES
    hb = ts // HALO
    halo_row = lambda bb, s: jnp.maximum((bb * ns + s) * hb - 1, 0)
    return pl.pallas_call(
        functools.partial(_conv_kernel, ts=ts),
        grid=(batch, ns),
        in_specs=[pl.BlockSpec((ts, D_CONV), lambda bb, s: (bb * ns + s, ZB_CA // cb)),
                  pl.BlockSpec((ts, D_CONV), lambda bb, s: (bb * ns + s, ZB_CG // cb)),
                  pl.BlockSpec((HALO, D_CONV), lambda bb, s: (halo_row(bb, s), ZB_CA // cb)),
                  pl.BlockSpec((HALO, D_CONV), lambda bb, s: (halo_row(bb, s), ZB_CG // cb)),
                  pl.BlockSpec((CONV_WIDTH, D_CONV), lambda bb, s: (0, 0)),
                  pl.BlockSpec((1, D_CONV), lambda bb, s: (0, 0)),
                  pl.BlockSpec((1, D_CONV), lambda bb, s: (0, 0)),
                  pl.BlockSpec((1, D_CONV), lambda bb, s: (0, 0))],
        out_specs=pl.BlockSpec((ts, D_CONV), lambda bb, s: (bb * ns + s, 0)),
        out_shape=jax.ShapeDtypeStruct((m, D_CONV), BF16),
        scratch_shapes=[pltpu.VMEM((HALO + ts, D_CONV), F32),
                        pltpu.VMEM((SUBLANES - 1, HALO + ts, D_CONV), F32)],
        compiler_params=_cparams(("parallel", "parallel")),
        name="conformer_conv",
    )(z, z, z, z, w, b, lg, lb)


def _outproj_kernel(x_ref, mod_ref, fox_ref, oc_ref, os_ref, ow_ref, cv_ref, w_ref, o_ref):
    d_fox = N_FOX * HEAD_DIM
    d_nsa = N_NSA * HEAD_DIM
    nsa = (oc_ref[...] + os_ref[...] + ow_ref[...]).astype(BF16)
    acc = _dot(fox_ref[...], w_ref[0:d_fox, :])
    acc = acc + _dot(nsa, w_ref[d_fox:d_fox + d_nsa, :])
    acc = acc + _dot(cv_ref[...], w_ref[d_fox + d_nsa:, :])
    o_ref[...] = x_ref[...] + mod_ref[2:3, :] * acc


def _outproj(x2, mod_l, fox, oc, osel, owin, conv, w, layer, seq):
    m, d = x2.shape
    tm = TM_OUT
    row = lambda i: (i, 0)
    return pl.pallas_call(
        _outproj_kernel,
        grid=(m // tm,),
        in_specs=[pl.BlockSpec((tm, d), row),
                  pl.BlockSpec((None, 6, d), lambda i: (i * tm // seq, 0, 0)),
                  pl.BlockSpec((tm, N_FOX * HEAD_DIM), row),
                  pl.BlockSpec((tm, N_NSA * HEAD_DIM), row),
                  pl.BlockSpec((tm, N_NSA * HEAD_DIM), row),
                  pl.BlockSpec((tm, N_NSA * HEAD_DIM), row),
                  pl.BlockSpec((tm, D_CONV), row),
                  pl.BlockSpec((None, d, d), lambda i: (layer, 0, 0))],
        out_specs=pl.BlockSpec((tm, d), row),
        out_shape=jax.ShapeDtypeStruct((m, d), F32),
        compiler_params=_cparams(("parallel",)),
        name="outproj",
    )(x2, mod_l, fox, oc, osel, owin, conv, w)


def _mlp_kernel(x_ref, mod_ref, g_ref, w1_ref, w2_ref, fg_ref, o_ref, h_ref, *, tm, final):
    f = pl.program_id(1)

    @pl.when(f == 0)
    def _():
        def chunk(c, carry):
            r = pl.multiple_of(c * ROW_CHUNK, ROW_CHUNK)
            h = _norm_mod(x_ref[pl.ds(r, ROW_CHUNK), :], g_ref[...], mod_ref[4:5, :], mod_ref[3:4, :])
            h_ref[pl.ds(r, ROW_CHUNK), :] = h.astype(BF16)
            return carry
        lax.fori_loop(0, tm // ROW_CHUNK, chunk, 0)
        o_ref[...] = jnp.zeros_like(o_ref)

    a = jnp.maximum(_dot(h_ref[...], w1_ref[...]), 0.0)
    o_ref[...] += _dot((a * a).astype(BF16), w2_ref[...])

    @pl.when(f == pl.num_programs(1) - 1)
    def _():
        def chunk(c, carry):
            r = pl.multiple_of(c * ROW_CHUNK, ROW_CHUNK)
            y = x_ref[pl.ds(r, ROW_CHUNK), :] + mod_ref[5:6, :] * o_ref[pl.ds(r, ROW_CHUNK), :]
            if final:
                y = y * lax.rsqrt(jnp.mean(y * y, axis=-1, keepdims=True) + EPS) * fg_ref[...]
            o_ref[pl.ds(r, ROW_CHUNK), :] = y
            return carry
        lax.fori_loop(0, tm // ROW_CHUNK, chunk, 0)


def _mlp(x2, mod_l, g, w1, w2, fg, layer, seq, final):
    m, d = x2.shape
    tm, tf = TM_MLP, TF_MLP
    dff = w1.shape[2]
    return pl.pallas_call(
        functools.partial(_mlp_kernel, tm=tm, final=final),
        grid=(m // tm, dff // tf),
        in_specs=[pl.BlockSpec((tm, d), lambda i, f: (i, 0), pipeline_mode=pl.Buffered(1)),
                  pl.BlockSpec((None, 6, d), lambda i, f: (i * tm // seq, 0, 0)),
                  pl.BlockSpec((1, d), lambda i, f: (0, 0)),
                  pl.BlockSpec((None, d, tf), lambda i, f: (layer, 0, f)),
                  pl.BlockSpec((None, tf, d), lambda i, f: (layer, f, 0)),
                  pl.BlockSpec((1, d), lambda i, f: (0, 0))],
        out_specs=pl.BlockSpec((tm, d), lambda i, f: (i, 0)),
        out_shape=jax.ShapeDtypeStruct((m, d), F32),
        scratch_shapes=[pltpu.VMEM((tm, d), BF16)],
        compiler_params=_cparams(("parallel", "arbitrary")),
        name="mlp_final" if final else "mlp",
    )(x2, mod_l, g, w1, w2, fg)


def _prep_inproj_weights(w_in):
    depth, d, _ = w_in.shape
    def sl(name, width, off=0, scale=None):
        piece = w_in[:, :, _SRC[name] + off:_SRC[name] + off + width]
        return (piece if scale is None else piece * scale).astype(BF16)
    qs = HEAD_DIM ** -0.5 * LOG2E
    pieces = [sl("fq", 768, scale=qs), sl("fk", 768), sl("fv", 768), sl("nq", 768, scale=qs),
              sl("ca", D_CONV), sl("cg", D_CONV)]
    for kname, vname in (("kc", "vc"), ("ks", "vs"), ("kw", "vw")):
        for g in range(N_KV):
            pieces += [sl(kname, HEAD_DIM, g * HEAD_DIM), sl(vname, HEAD_DIM, g * HEAD_DIM)]
    pieces.append(jnp.zeros((depth, d, LANES), BF16))
    wz = jnp.concatenate(pieces, axis=2)
    assert wz.shape[2] == NZ
    ws = jnp.concatenate([sl("ff", N_FOX), sl("ng", 3 * N_NSA),
                          jnp.zeros((depth, d, LANES - N_FOX - 3 * N_NSA), BF16)], axis=2)
    return wz, ws


def _prep_compress_weights(w_cmp_k, w_cmp_v, pos_cmp):
    depth = w_cmp_k.shape[0]
    slots = 2
    wkv = jnp.stack([w_cmp_k, w_cmp_v], axis=1)
    big = jnp.einsum("zklde,kq->zlkdqe", wkv, jnp.eye(2, dtype=F32))
    big = big.reshape(depth, CMP_LEN, slots * HEAD_DIM, slots * HEAD_DIM)
    half = CMP_LEN // 2
    kdim = half * slots * HEAD_DIM
    wlo = big[:, :half].reshape(depth, kdim, slots * HEAD_DIM).astype(BF16)
    whi = big[:, half:].reshape(depth, kdim, slots * HEAD_DIM).astype(BF16)
    pos = jnp.broadcast_to(pos_cmp[:, :, None, :], (depth, CMP_LEN, slots, HEAD_DIM))
    plo = jnp.zeros((depth, 8, kdim), F32).at[:, 0].set(pos[:, :half].reshape(depth, kdim)).astype(BF16)
    phi = jnp.zeros((depth, 8, kdim), F32).at[:, 0].set(pos[:, half:].reshape(depth, kdim)).astype(BF16)
    return wlo, whi, plo, phi


def _selection_constants(seq):
    nc = seq // CMP_STRIDE
    n_cmp = (seq - CMP_LEN) // CMP_STRIDE + 1
    n = np.arange(nc)[:, None]
    j = np.arange(LANES)[None, :]
    overlap = ((n * CMP_STRIDE < j * SEL_LEN + SEL_LEN) & (n * CMP_STRIDE + CMP_LEN - 1 >= j * SEL_LEN)
               & (n < n_cmp) & (j < seq // SEL_LEN))
    emat = (np.arange(seq)[:, None] // SEL_LEN + HALF) == np.arange(LANES)[None, :]
    return jnp.asarray(overlap, BF16), jnp.asarray(emat, BF16)


def kernel(x, c, w_mod, b_mod, norm1_g, w_in, b_f, w_cmp_k, w_cmp_v, pos_cmp, conv_w, conv_b, conv_ln_g,
           conv_ln_b, w_out, norm2_g, w_mlp1, w_mlp2, rel_bias, final_g):
    batch, seq, d = x.shape
    depth = w_in.shape[0]
    assert d == D_MODEL and seq % TM_INPROJ == 0 and seq // SEL_LEN <= SEL_LEN

    wz, ws = _prep_inproj_weights(w_in)
    wlo, whi, plo, phi = _prep_compress_weights(w_cmp_k, w_cmp_v, pos_cmp)
    w_out_b = w_out.astype(BF16)
    w1_b = w_mlp1.astype(BF16)
    w2_b = w_mlp2.astype(BF16)
    overlap, emat = _selection_constants(seq)
    bf_rows = jnp.zeros((depth, 1, LANES), F32).at[:, 0, SMALL_FF:SMALL_FF + N_FOX].set(b_f)

    mod = _modulation(c, w_mod, b_mod)
    fvec = _bias_by_distance(rel_bias)
    tiles = _bias_tiles(fvec)
    fg = final_g.reshape(1, d)

    x2 = x.reshape(batch * seq, d)
    kv0 = ZB_KVC * LANES
    for l in range(depth):
        z, small = _inproj(x2, mod[l], norm1_g[l].reshape(1, d), wz, ws, l, seq)
        fcol, frow = _fox_cumsum(small, bf_rows[l], batch, seq)
        o_fox = _fox_attention(z, fcol, frow, batch, seq)
        r = z[:, kv0:kv0 + N_KV * LANES].reshape(batch, seq // CMP_STRIDE, CMP_STRIDE, N_KV, LANES)
        r = r.transpose(0, 3, 1, 2, 4).reshape(batch, N_KV, seq // CMP_STRIDE, CMP_STRIDE * LANES)
        kvcmp = _compress(r, wlo[l], whi[l], plo[l], phi[l])
        o_cmp, sel, gsig = _nsa_compressed(z, kvcmp, small, fvec, overlap, batch, seq)
        o_sel = _nsa_flash(z, sel, emat, gsig, tiles, batch, seq, windowed=False)
        o_win = _nsa_flash(z, sel, emat, gsig, tiles, batch, seq, windowed=True)
        o_conv = _conformer_conv(z, conv_w[l], conv_b[l].reshape(1, -1), conv_ln_g[l].reshape(1, -1),
                                 conv_ln_b[l].reshape(1, -1), batch, seq)
        x2 = _outproj(x2, mod[l], o_fox, o_cmp, o_sel, o_win, o_conv, w_out_b, l, seq)
        x2 = _mlp(x2, mod[l], norm2_g[l].reshape(1, d), w1_b, w2_b, fg, l, seq, final=(l == depth - 1))
    return x2.reshape(batch, seq, d)
```

```python
import functools
import math

import numpy as np
import jax
import jax.numpy as jnp
from jax import lax
from jax.experimental import pallas as pl
from jax.experimental.pallas import tpu as pltpu

F32 = jnp.float32
BF16 = jnp.bfloat16
HIGHEST = lax.Precision.HIGHEST

D_MODEL = 2048
HEAD_DIM = 64
N_FOX = 12
N_NSA = 12
N_KV = 3
GROUP = 4
D_CONV = 512
D_FF = 4 * D_MODEL
CMP_LEN = 32
CMP_STRIDE = 16
SEL_LEN = 64
N_SEL = 16
WINDOW = 512
CONV_WIDTH = 31
N_BUCKETS = 32
MAX_DISTANCE = 128
EPS = 1e-6
NEG = -1e30
BIG = 1e9
LOG2E = math.log2(math.e)

LANES = 128
SUBLANES = 8
HALF = LANES // 2

ZB_FQ, ZB_FK, ZB_FV, ZB_NQ = 0, 6, 12, 18
ZB_CA, ZB_CG = 24, 28
ZB_KVC, ZB_KVS, ZB_KVW = 32, 35, 38
NZ_BLOCKS = 42
NZ = NZ_BLOCKS * LANES
SMALL_FF, SMALL_NG = 0, 12

_SRC = dict(fq=0, fk=768, fv=1536, ff=2304, nq=2316, kc=3084, vc=3276, ks=3468, vs=3660,
            kw=3852, vw=4044, ng=4236, ca=4272, cg=4784)

TM_INPROJ = 1024
TN_INPROJ = 1792
TM_OUT = 512
TM_MLP = 1024
TF_MLP = 1024
ROW_CHUNK = 128
T_FOX = 512
NSUB_FOX = 4
T_NSA = 256
NSUB_NSA = 4
TQ_CMP = 512
TS_CONV = 512
CONV_ROWS = 64
TS_CUM = 512
HALO = 32
VMEM_LIMIT = 56 * 1024 * 1024


def _cparams(sem):
    return pltpu.CompilerParams(dimension_semantics=sem, vmem_limit_bytes=VMEM_LIMIT)


def _dot(a, b):
    return jnp.dot(a, b, preferred_element_type=F32)


def _dot_nt(a, b):
    return lax.dot_general(a, b, (((1,), (1,)), ((), ())), preferred_element_type=F32)


def _bucket_thresholds():
    d = np.arange(MAX_DISTANCE, dtype=np.int32)
    max_exact = N_BUCKETS // 2
    nf = np.maximum(d, 1).astype(np.float32)
    large = max_exact + (np.log(nf / np.float32(max_exact)) / np.float32(math.log(MAX_DISTANCE / max_exact))
                         * np.float32(N_BUCKETS - max_exact)).astype(np.int32)
    large = np.minimum(large, N_BUCKETS - 1)
    bucket = np.where(d < max_exact, d, large)
    assert bucket[-1] == N_BUCKETS - 1 and np.all(np.diff(bucket) >= 0)
    return [int(np.argmax(bucket >= k)) for k in range(N_BUCKETS)]


_T5_THRESH = _bucket_thresholds()


def _fvec_kernel(tbl_ref, out_ref):
    d = lax.broadcasted_iota(jnp.int32, out_ref.shape, 1)
    out = jnp.broadcast_to(tbl_ref[:, 0:1], out_ref.shape)
    for k in range(1, N_BUCKETS):
        out = jnp.where(d >= _T5_THRESH[k], tbl_ref[:, k:k + 1], out)
    out_ref[...] = out * LOG2E


def _bias_by_distance(rel_bias):
    tbl = jnp.zeros((16, LANES), F32).at[:N_NSA, :N_BUCKETS].set(rel_bias.astype(F32).T)
    return pl.pallas_call(
        _fvec_kernel, out_shape=jax.ShapeDtypeStruct((16, LANES), F32), name="t5_bias_by_distance",
    )(tbl)


def _mod_kernel(ct_ref, w_ref, b_ref, o_ref, *, nb):
    c = ct_ref[...]
    ca = c * jax.nn.sigmoid(c)
    w = w_ref[...]
    rows = [jnp.sum(w * ca[:, b:b + 1], axis=0, keepdims=True) for b in range(nb)]
    rows.append(jnp.zeros((SUBLANES - nb, w.shape[1]), F32))
    o_ref[...] = jnp.concatenate(rows, axis=0) + b_ref[...]


def _modulation(c, w_mod, b_mod):
    depth, d, n = w_mod.shape
    b = c.shape[0]
    assert b < SUBLANES
    tn = 1024
    ct = jnp.zeros((d, LANES), F32).at[:, :b].set(c.T)
    out = pl.pallas_call(
        functools.partial(_mod_kernel, nb=b),
        grid=(depth, n // tn),
        in_specs=[pl.BlockSpec((d, LANES), lambda l, j: (0, 0)),
                  pl.BlockSpec((None, d, tn), lambda l, j: (l, 0, j)),
                  pl.BlockSpec((None, 1, tn), lambda l, j: (l, 0, j))],
        out_specs=pl.BlockSpec((None, SUBLANES, tn), lambda l, j: (l, 0, j)),
        out_shape=jax.ShapeDtypeStruct((depth, SUBLANES, n), F32),
        compiler_params=_cparams(("parallel", "parallel")),
        name="adaln_modulation",
    )(ct, w_mod, b_mod.reshape(depth, 1, n))
    return out[:, :b].reshape(depth, b, 6, d)


def _norm_mod(x, g, sc, sh):
    y = x * lax.rsqrt(jnp.mean(x * x, axis=-1, keepdims=True) + EPS) * g
    return y * (1.0 + sc) + sh


def _inproj_kernel(x_ref, mod_ref, g_ref, w_ref, ws_ref, z_ref, sm_ref, h_ref, *, tm):
    @pl.when(pl.program_id(1) == 0)
    def _():
        def chunk(c, carry):
            r = pl.multiple_of(c * ROW_CHUNK, ROW_CHUNK)
            h = _norm_mod(x_ref[pl.ds(r, ROW_CHUNK), :], g_ref[...], mod_ref[1:2, :], mod_ref[0:1, :])
            hb = h.astype(BF16)
            h_ref[pl.ds(r, ROW_CHUNK), :] = hb
            sm_ref[pl.ds(r, ROW_CHUNK), :] = _dot(hb, ws_ref[...])
            return carry
        lax.fori_loop(0, tm // ROW_CHUNK, chunk, 0)

    z_ref[...] = _dot(h_ref[...], w_ref[...]).astype(BF16)


def _inproj(x2, mod_l, g, wz, ws, layer, seq):
    m, d = x2.shape
    tm, tn = TM_INPROJ, TN_INPROJ
    assert seq % tm == 0 and NZ % tn == 0
    return pl.pallas_call(
        functools.partial(_inproj_kernel, tm=tm),
        grid=(m // tm, NZ // tn),
        in_specs=[pl.BlockSpec((tm, d), lambda i, j: (i, 0)),
                  pl.BlockSpec((None, 6, d), lambda i, j: (i * tm // seq, 0, 0)),
                  pl.BlockSpec((1, d), lambda i, j: (0, 0)),
                  pl.BlockSpec((None, d, tn), lambda i, j: (layer, 0, j)),
                  pl.BlockSpec((None, d, LANES), lambda i, j: (layer, 0, 0))],
        out_specs=[pl.BlockSpec((tm, tn), lambda i, j: (i, j)),
                   pl.BlockSpec((tm, LANES), lambda i, j: (i, 0))],
        out_shape=[jax.ShapeDtypeStruct((m, NZ), BF16), jax.ShapeDtypeStruct((m, LANES), F32)],
        scratch_shapes=[pltpu.VMEM((tm, d), BF16)],
        compiler_params=_cparams(("parallel", "arbitrary")),
        name="inproj",
    )(x2, mod_l, g, wz, ws)


def _foxcum_kernel(sm_ref, bf_ref, fcol_ref, frow_ref, carry_ref, *, ts):
    @pl.when(pl.program_id(1) == 0)
    def _():
        carry_ref[...] = jnp.zeros_like(carry_ref)

    x = sm_ref[...] + bf_ref[...]
    ls = (jnp.minimum(x, 0.0) - jnp.log1p(jnp.exp(-jnp.abs(x)))) * LOG2E
    r = lax.broadcasted_iota(jnp.int32, (ts, ts), 0)
    c = lax.broadcasted_iota(jnp.int32, (ts, ts), 1)
    tri = jnp.where(r >= c, 1.0, 0.0).astype(F32)
    cum = jnp.dot(tri, ls, precision=HIGHEST, preferred_element_type=F32) + carry_ref[0:1, :]
    carry_ref[0:1, :] = cum[ts - 1:ts, :]
    cum_t = cum.T
    for hp in range(N_FOX // 2):
        fcol_ref[hp] = cum if hp == 0 else pltpu.roll(cum, LANES - 2 * hp, 1)
        frow_ref[hp, 0:2, :] = cum_t[2 * hp:2 * hp + 2, :]
        frow_ref[hp, 2:8, :] = jnp.zeros((6, ts), F32)


def _fox_cumsum(small, bf_row, batch, seq):
    ts = TS_CUM
    ns = seq // ts
    npair = N_FOX // 2
    return pl.pallas_call(
        functools.partial(_foxcum_kernel, ts=ts),
        grid=(batch, ns),
        in_specs=[pl.BlockSpec((ts, LANES), lambda b, s: (b * ns + s, 0)),
                  pl.BlockSpec((1, LANES), lambda b, s: (0, 0))],
        out_specs=[pl.BlockSpec((None, npair, ts, LANES), lambda b, s: (b, 0, s, 0)),
                   pl.BlockSpec((None, npair, 8, ts), lambda b, s: (b, 0, 0, s))],
        out_shape=[jax.ShapeDtypeStruct((batch, npair, seq, LANES), F32),
                   jax.ShapeDtypeStruct((batch, npair, 8, seq), F32)],
        scratch_shapes=[pltpu.VMEM((8, LANES), F32)],
        compiler_params=_cparams(("parallel", "arbitrary")),
        name="fox_cumsum",
    )(small, bf_row)


def _fox_kernel(q_ref, k_ref, v_ref, fc_ref, fr_ref, o_ref, m_ref, fq_ref, acc_ref, *, t, nsub):
    qb = pl.program_id(2)
    kb = pl.program_id(3)
    tb = nsub * t

    @pl.when(kb == 0)
    def _():
        m_ref[...] = jnp.full(m_ref.shape, NEG, F32)
        acc_ref[...] = jnp.zeros_like(acc_ref)
        for h in range(2):
            fq_ref[h] = jnp.broadcast_to(fc_ref[:, h:h + 1], (tb, LANES))

    lane = lax.broadcasted_iota(jnp.int32, (1, LANES), 1)

    def sub_tile(i, j, diagonal):
        rows = slice(i * t, (i + 1) * t)
        cols = slice(j * t, (j + 1) * t)
        q2 = q_ref[rows, :]
        k2 = k_ref[cols, :]
        v2 = v_ref[cols, :]
        if diagonal:
            causal = (lax.broadcasted_iota(jnp.int32, (t, t), 1) <= lax.broadcasted_iota(jnp.int32, (t, t), 0))
        zero = jnp.zeros_like(q2)
        s2 = _dot_nt(jnp.concatenate([jnp.where(lane < HALF, q2, zero), jnp.where(lane >= HALF, q2, zero)], axis=0), k2)
        for h in range(2):
            keep = (lane < HALF) if h == 0 else (lane >= HALF)
            vh = jnp.where(keep, v2, jnp.ones_like(v2))
            s = s2[h * t:(h + 1) * t, :] - fr_ref[h:h + 1, cols]
            if diagonal:
                s = jnp.where(causal, s, NEG)
            fq = fq_ref[h, rows, :]
            m_old = m_ref[h, rows, :]
            m_new = jnp.maximum(m_old, jnp.max(s, axis=1, keepdims=True) + fq)
            shift = jnp.tile(fq - m_new, (1, t // LANES))
            p = jnp.exp2(s + shift)
            acc_ref[h, rows, :] = jnp.exp2(m_old - m_new) * acc_ref[h, rows, :] + _dot(p.astype(BF16), vh)
            m_ref[h, rows, :] = m_new

    @pl.when(kb < qb)
    def _():
        for i in range(nsub):
            for j in range(nsub):
                sub_tile(i, j, False)

    @pl.when(kb == qb)
    def _():
        for i in range(nsub):
            for j in range(i + 1):
                sub_tile(i, j, i == j)
        a0 = acc_ref[0]
        a1 = acc_ref[1]
        o_ref[...] = jnp.where(lane < HALF, a0 / pltpu.roll(a0, HALF, 1),
                               a1 / pltpu.roll(a1, HALF, 1)).astype(BF16)


def _fox_attention(z, fcol, frow, batch, seq):
    t, nsub = T_FOX, NSUB_FOX
    tb = t * nsub
    nb = seq // tb
    npair = N_FOX // 2
    m = batch * seq
    kv_row = lambda b, hp, qb, kb: b * nb + jnp.minimum(kb, qb)
    return pl.pallas_call(
        functools.partial(_fox_kernel, t=t, nsub=nsub),
        grid=(batch, npair, nb, nb),
        in_specs=[pl.BlockSpec((tb, LANES), lambda b, hp, qb, kb: (b * nb + qb, ZB_FQ + hp)),
                  pl.BlockSpec((tb, LANES), lambda b, hp, qb, kb: (kv_row(b, hp, qb, kb), ZB_FK + hp)),
                  pl.BlockSpec((tb, LANES), lambda b, hp, qb, kb: (kv_row(b, hp, qb, kb), ZB_FV + hp)),
                  pl.BlockSpec((None, None, tb, LANES), lambda b, hp, qb, kb: (b, hp, qb, 0)),
                  pl.BlockSpec((None, None, 8, tb), lambda b, hp, qb, kb: (b, hp, 0, jnp.minimum(kb, qb)))],
        out_specs=pl.BlockSpec((tb, LANES), lambda b, hp, qb, kb: (b * nb + qb, hp)),
        out_shape=jax.ShapeDtypeStruct((m, N_FOX * HEAD_DIM), BF16),
        scratch_shapes=[pltpu.VMEM((2, tb, LANES), F32), pltpu.VMEM((2, tb, LANES), F32),
                        pltpu.VMEM((2, tb, LANES), F32)],
        compiler_params=_cparams(("parallel", "parallel", "parallel", "arbitrary")),
        name="fox_attention",
    )(z, z, z, fcol, frow)


def _compress_kernel(r_ref, wlo_ref, whi_ref, plo_ref, phi_ref, o_ref, *, nc):
    bias = _dot(plo_ref[...], wlo_ref[...]) + _dot(phi_ref[...], whi_ref[...])
    for g in range(N_KV):
        r = r_ref[g]
        out = _dot(r, wlo_ref[...]) + pltpu.roll(_dot(r, whi_ref[...]), nc - 1, 0) + bias[0:1, :]
        o_ref[:, g * LANES:(g + 1) * LANES] = out.astype(BF16)


def _compress(r, wlo, whi, plo, phi):
    batch, ng, nc, k = r.shape
    return pl.pallas_call(
        functools.partial(_compress_kernel, nc=nc),
        grid=(batch,),
        in_specs=[pl.BlockSpec((None, ng, nc, k), lambda b: (b, 0, 0, 0)),
                  pl.BlockSpec((k, LANES), lambda b: (0, 0)),
                  pl.BlockSpec((k, LANES), lambda b: (0, 0)),
                  pl.BlockSpec((8, k), lambda b: (0, 0)),
                  pl.BlockSpec((8, k), lambda b: (0, 0))],
        out_specs=pl.BlockSpec((None, nc, ng * LANES), lambda b: (b, 0, 0)),
        out_shape=jax.ShapeDtypeStruct((batch, nc, ng * LANES), BF16),
        compiler_params=_cparams(("parallel",)),
        name="nsa_compress",
    )(r, wlo, whi, plo, phi)


def _head_queries(qf, lane):
    zero = jnp.zeros_like(qf)
    qa = jnp.where(lane < HALF, qf, zero).astype(BF16)
    qb = jnp.where(lane < HALF, pltpu.roll(qf, HALF, 1), zero).astype(BF16)
    return qa, qb


def _gather_bias(frow, idx, tq):
    tab = jnp.broadcast_to(frow, (tq, LANES))
    parts = [jnp.take_along_axis(tab, idx[:, c * LANES:(c + 1) * LANES], axis=1)
             for c in range(idx.shape[1] // LANES)]
    return parts[0] if len(parts) == 1 else jnp.concatenate(parts, axis=1)


def _nsacmp_kernel(q_ref, kvc_ref, sm_ref, fv_ref, ov_ref, oc_ref, sel_ref, gs_ref, val_ref, *, tq, nc):
    qi = pl.program_id(1)
    t0 = qi * tq
    lane = lax.broadcasted_iota(jnp.int32, (1, LANES), 1)
    t_col = t0 + lax.broadcasted_iota(jnp.int32, (tq, 1), 0)
    any_valid = t_col >= CMP_LEN - 1

    gs = jax.nn.sigmoid(sm_ref[...])
    n_sel = SEL_LEN
    j_row = lax.broadcasted_iota(jnp.int32, (n_sel, tq), 0)
    sub_iota = lax.broadcasted_iota(jnp.int32, (SUBLANES, tq), 0)
    t_row = t0 + lax.broadcasted_iota(jnp.int32, (n_sel, tq), 1)
    cur = jnp.right_shift(t_row, SEL_LEN.bit_length() - 1)
    forced = (j_row == 0) | (j_row == cur) | (j_row == cur - 1)
    valid_s = j_row * SEL_LEN <= t_row

    def attend(ncols):
        cmp_end = lax.broadcasted_iota(jnp.int32, (1, ncols), 1) * CMP_STRIDE + (CMP_LEN - 1)
        dist = t_col - cmp_end
        masked = jnp.where(dist >= 0, 0.0, NEG)
        idx = jnp.clip(dist, 0, MAX_DISTANCE - 1)
        for g in range(N_KV):
            kv = kvc_ref[0:ncols, g * LANES:(g + 1) * LANES]
            gs_g = pltpu.roll(gs, LANES - (SMALL_NG + g * GROUP * 3), 1)
            gs_ref[:, g * LANES:(g + 1) * LANES] = gs_g
            psum = jnp.zeros((tq, ncols), F32)
            outs = []
            for pair in range(2):
                c0 = (g * 2 + pair) * LANES
                qpair = _head_queries(q_ref[:, c0:c0 + LANES].astype(F32), lane)
                for sub in range(2):
                    r = pair * 2 + sub
                    h = g * GROUP + r
                    s = _dot_nt(qpair[sub], kv) + _gather_bias(fv_ref[h:h + 1, :], idx, tq) + masked
                    e = jnp.exp2(s - jnp.max(s, axis=1, keepdims=True))
                    p = e * jnp.where(any_valid, 1.0 / jnp.sum(e, axis=1, keepdims=True), 0.0)
                    psum = psum + p
                    o = _dot(p.astype(BF16), kv)
                    outs.append(o * gs_g[:, r * 3:r * 3 + 1])
                oc_ref[:, c0:c0 + LANES] = jnp.where(lane < HALF, pltpu.roll(outs[-2], HALF, 1), outs[-1])

            p_hi = psum.astype(BF16)
            p_lo = (psum - p_hi.astype(F32)).astype(BF16)
            imp = _dot(p_hi, ov_ref[0:ncols, :]) + _dot(p_lo, ov_ref[0:ncols, :])
            val_ref[g] = jnp.where(forced, BIG, jnp.where(valid_s, imp.T[0:n_sel, :], NEG))

    n_low = (CMP_STRIDE * LANES + CMP_LEN - 1) // tq
    if nc > LANES and n_low > 0:
        @pl.when(qi < n_low)
        def _():
            attend(LANES)

        @pl.when(qi >= n_low)
        def _():
            attend(nc)
    else:
        attend(nc)

    for g in range(N_KV):
        groups = [val_ref[g, SUBLANES * b:SUBLANES * (b + 1), :] for b in range(n_sel // SUBLANES)]
        counts = [jnp.zeros((SUBLANES, tq), F32) for _ in groups]
        for j in range(n_sel):
            jb, js = divmod(j, SUBLANES)
            row = groups[jb][js:js + 1, :]
            for b, grp in enumerate(groups):
                if b < jb:
                    ahead = row > grp
                elif b > jb:
                    ahead = row >= grp
                else:
                    ahead = (row > grp) | ((row == grp) & (js < sub_iota))
                counts[b] = counts[b] + jnp.where(ahead, 1.0, 0.0)
        cnt = jnp.concatenate(counts, axis=0)
        unchosen = jnp.where(cnt < float(N_SEL), 0.0, NEG)
        unchosen = jnp.concatenate([jnp.zeros((LANES - n_sel, tq), F32), unchosen], axis=0)
        sel_ref[:, g * LANES:(g + 1) * LANES] = unchosen.T.astype(BF16)


def _nsa_compressed(z, kvcmp, small, fvec, overlap, batch, seq):
    tq = TQ_CMP
    nq = seq // tq
    nc = kvcmp.shape[1]
    m = batch * seq
    w3 = N_KV * LANES
    return pl.pallas_call(
        functools.partial(_nsacmp_kernel, tq=tq, nc=nc),
        grid=(batch, nq),
        in_specs=[pl.BlockSpec((tq, N_NSA * HEAD_DIM), lambda b, i: (b * nq + i, ZB_NQ * LANES // (N_NSA * HEAD_DIM))),
                  pl.BlockSpec((None, nc, w3), lambda b, i: (b, 0, 0)),
                  pl.BlockSpec((tq, LANES), lambda b, i: (b * nq + i, 0)),
                  pl.BlockSpec((16, LANES), lambda b, i: (0, 0)),
                  pl.BlockSpec((nc, LANES), lambda b, i: (0, 0))],
        out_specs=[pl.BlockSpec((tq, N_NSA * HEAD_DIM), lambda b, i: (b * nq + i, 0)),
                   pl.BlockSpec((tq, w3), lambda b, i: (b * nq + i, 0)),
                   pl.BlockSpec((tq, w3), lambda b, i: (b * nq + i, 0))],
        out_shape=[jax.ShapeDtypeStruct((m, N_NSA * HEAD_DIM), F32),
                   jax.ShapeDtypeStruct((m, w3), BF16),
                   jax.ShapeDtypeStruct((m, w3), F32)],
        scratch_shapes=[pltpu.VMEM((N_KV, SEL_LEN, tq), F32)],
        compiler_params=_cparams(("parallel", "parallel")),
        name="nsa_compressed",
    )(z, kvcmp, small, fvec, overlap)


def _nsaflash_kernel(*refs, t, nsub, nkk, windowed, gate_col):
    if windowed:
        qa_ref, qb_ref, kv_ref, gs_ref, tb_ref, o_ref, q4_ref, m_ref, acc_ref = refs
        sel_ref = e_ref = None
    else:
        qa_ref, qb_ref, kv_ref, sel_ref, e_ref, gs_ref, tb_ref, o_ref, q4_ref, m_ref, acc_ref = refs
    g = pl.program_id(1)
    qb = pl.program_id(2)
    kk = pl.program_id(3)
    far = tb_ref.shape[0] - 1
    lane = lax.broadcasted_iota(jnp.int32, (1, LANES), 1)

    @pl.when(kk == 0)
    def _():
        m_ref[...] = jnp.full(m_ref.shape, NEG, F32)
        acc_ref[...] = jnp.zeros_like(acc_ref)
        for i in range(nsub):
            rows = slice(i * t, (i + 1) * t)
            queries = (_head_queries(qa_ref[rows, :].astype(F32), lane)
                       + _head_queries(qb_ref[rows, :].astype(F32), lane))
            for r in range(GROUP):
                q = queries[r] if windowed else jnp.where(lane < HALF, queries[r], sel_ref[rows, :])
                q4_ref[(i * GROUP + r) * t:(i * GROUP + r + 1) * t, :] = q

    def sub_tile(i, j, rel):
        tidx = rel if (windowed or rel <= 1) else far
        kv = kv_ref[j * t:(j + 1) * t, :]
        ones_v = jnp.where(lane < HALF, jnp.ones_like(kv), kv)
        keys = kv if windowed else jnp.where(lane < HALF, kv, e_ref[j * t:(j + 1) * t, :])
        s4 = _dot_nt(q4_ref[i * GROUP * t:(i + 1) * GROUP * t, :], keys)
        for pair in range(2):
            base = (i * GROUP + 2 * pair) * t
            rows2 = slice(base, base + 2 * t)
            probs, alphas = [], []
            for sub in range(2):
                r = 2 * pair + sub
                h = g * GROUP + r
                rows = slice(base + sub * t, base + (sub + 1) * t)
                m_old = m_ref[rows, :]
                s = s4[r * t:(r + 1) * t, :] + tb_ref[tidx, h]
                m_new = jnp.maximum(m_old, jnp.max(s, axis=1, keepdims=True))
                p = jnp.exp2(s - jnp.tile(m_new, (1, t // LANES)))
                m_ref[rows, :] = m_new
                alphas.append(jnp.exp2(m_old - m_new))
                probs.append(p.astype(BF16))
            acc_ref[rows2, :] = (jnp.concatenate(alphas, axis=0) * acc_ref[rows2, :]
                                 + _dot(jnp.concatenate(probs, axis=0), ones_v))

    def block(offset):
        max_rel = far - 1 if windowed else None
        for j in range(nsub):
            for i in range(nsub):
                rel = offset * nsub + i - j
                if rel >= 0 and (max_rel is None or rel <= max_rel):
                    sub_tile(i, j, rel)

    if windowed:
        for c in range(nkk):
            @pl.when((kk == c) & (qb - (nkk - 1) + c >= 0))
            def _(c=c):
                block(nkk - 1 - c)
        last = kk == nkk - 1
    else:
        @pl.when(qb - kk == 0)
        def _():
            block(0)

        @pl.when(qb - kk == 1)
        def _():
            block(1)

        @pl.when(qb - kk >= 2)
        def _():
            block(2)
        last = kk == qb

    @pl.when(last)
    def _():
        for i in range(nsub):
            rows = slice(i * t, (i + 1) * t)
            gs = gs_ref[rows, :]
            o = []
            for r in range(GROUP):
                a = acc_ref[(i * GROUP + r) * t:(i * GROUP + r + 1) * t, :]
                o.append(a / pltpu.roll(a, HALF, 1) * gs[:, r * 3 + gate_col:r * 3 + gate_col + 1])
            o_ref[rows, 0:LANES] = jnp.where(lane < HALF, pltpu.roll(o[0], HALF, 1), o[1])
            o_ref[rows, LANES:2 * LANES] = jnp.where(lane < HALF, pltpu.roll(o[2], HALF, 1), o[3])


def _bias_tile_kernel(fv_ref, o_ref, *, t, nrel):
    rel = pl.program_id(0)
    h = pl.program_id(1)
    dist = rel * t + lax.broadcasted_iota(jnp.int32, (t, t), 0) - lax.broadcasted_iota(jnp.int32, (t, t), 1)
    bias = _gather_bias(fv_ref[pl.ds(h, 1), :], jnp.clip(dist, 0, MAX_DISTANCE - 1), t)
    o_ref[...] = jnp.where((dist >= 0) & ((dist < WINDOW) | (rel == nrel)), bias, NEG)


def _bias_tiles(fvec):
    t = T_NSA
    nrel = WINDOW // t + 1
    return pl.pallas_call(
        functools.partial(_bias_tile_kernel, t=t, nrel=nrel),
        grid=(nrel + 1, N_NSA),
        in_specs=[pl.BlockSpec((16, LANES), lambda r, h: (0, 0))],
        out_specs=pl.BlockSpec((None, None, t, t), lambda r, h: (r, h, 0, 0)),
        out_shape=jax.ShapeDtypeStruct((nrel + 1, N_NSA, t, t), F32),
        compiler_params=_cparams(("parallel", "parallel")),
        name="nsa_bias_tiles",
    )(fvec)


def _nsa_flash(z, sel, emat, gsig, tiles, batch, seq, windowed):
    t, nsub = T_NSA, NSUB_NSA
    tb = t * nsub
    nb = seq // tb
    m = batch * seq
    max_rel = WINDOW // t
    nkk = ((max_rel + nsub - 1) // nsub + 1) if windowed else nb
    assert t >= MAX_DISTANCE and 2 * t <= WINDOW and WINDOW % t == 0 and seq % tb == 0
    zb_kv = ZB_KVW if windowed else ZB_KVS
    if windowed:
        kidx = lambda qb, kk: jnp.maximum(qb - (nkk - 1) + kk, 0)
    else:
        kidx = lambda qb, kk: jnp.minimum(kk, qb)
    in_specs = [pl.BlockSpec((tb, LANES), lambda b, g, qb, kk: (b * nb + qb, ZB_NQ + 2 * g)),
                pl.BlockSpec((tb, LANES), lambda b, g, qb, kk: (b * nb + qb, ZB_NQ + 2 * g + 1)),
                pl.BlockSpec((tb, LANES), lambda b, g, qb, kk: (b * nb + kidx(qb, kk), zb_kv + g))]
    args = [z, z, z]
    if not windowed:
        in_specs += [pl.BlockSpec((tb, LANES), lambda b, g, qb, kk: (b * nb + qb, g)),
                     pl.BlockSpec((tb, LANES), lambda b, g, qb, kk: (kidx(qb, kk), 0))]
        args += [sel, emat]
    in_specs += [pl.BlockSpec((tb, LANES), lambda b, g, qb, kk: (b * nb + qb, g)),
                 pl.BlockSpec(tiles.shape, lambda b, g, qb, kk: (0, 0, 0, 0))]
    args += [gsig, tiles]
    return pl.pallas_call(
        functools.partial(_nsaflash_kernel, t=t, nsub=nsub, nkk=nkk, windowed=windowed,
                          gate_col=2 if windowed else 1),
        grid=(batch, N_KV, nb, nkk),
        in_specs=in_specs,
        out_specs=pl.BlockSpec((tb, 2 * LANES), lambda b, g, qb, kk: (b * nb + qb, g)),
        out_shape=jax.ShapeDtypeStruct((m, N_NSA * HEAD_DIM), F32),
        scratch_shapes=[pltpu.VMEM((nsub * GROUP * t, LANES), BF16), pltpu.VMEM((nsub * GROUP * t, LANES), F32),
                        pltpu.VMEM((nsub * GROUP * t, LANES), F32)],
        compiler_params=_cparams(("parallel", "parallel", "parallel", "arbitrary")),
        name="nsa_window" if windowed else "nsa_selected",
    )(*args)


def _conv_kernel(a_ref, g_ref, ah_ref, gh_ref, w_ref, b_ref, lg_ref, lb_ref, o_ref, u_ref, us_ref, *, ts):
    first = pl.program_id(1) == 0
    halo = ah_ref[...].astype(F32) * jax.nn.sigmoid(gh_ref[...].astype(F32))
    u_ref[0:HALO, :] = jnp.where(first, 0.0, halo)
    u_ref[HALO:HALO + ts, :] = a_ref[...].astype(F32) * jax.nn.sigmoid(g_ref[...].astype(F32))
    n_shift = HALO + ts - SUBLANES
    for s in range(1, SUBLANES):
        us_ref[s - 1, 0:n_shift, :] = u_ref[s:s + n_shift, :]
    base = HALO - (CONV_WIDTH - 1)
    for c in range(ts // CONV_ROWS):
        r0 = c * CONV_ROWS
        y = jnp.zeros((CONV_ROWS, D_CONV), F32)
        for k in range(CONV_WIDTH):
            start, s = divmod(base + k + r0, SUBLANES)
            start *= SUBLANES
            window = (u_ref[start:start + CONV_ROWS, :] if s == 0
                      else us_ref[s - 1, start:start + CONV_ROWS, :])
            y = y + window * w_ref[k:k + 1, :]
        y = y + b_ref[...]
        mu = jnp.mean(y, axis=-1, keepdims=True)
        yc = y - mu
        var = jnp.mean(yc * yc, axis=-1, keepdims=True)
        yn = yc * lax.rsqrt(var + EPS) * lg_ref[...] + lb_ref[...]
        o_ref[r0:r0 + CONV_ROWS, :] = (yn * jax.nn.sigmoid(yn)).astype(BF16)


def _conformer_conv(z, w, b, lg, lb, batch, seq):
    ts = TS_CONV
    ns = seq // ts
    m = batch * seq
    cb = D_CONV // LANES
    hb = ts // HALO
    halo_row = lambda bb, s: jnp.maximum((bb * ns + s) * hb - 1, 0)
    return pl.pallas_call(
        functools.partial(_conv_kernel, ts=ts),
        grid=(batch, ns),
        in_specs=[pl.BlockSpec((ts, D_CONV), lambda bb, s: (bb * ns + s, ZB_CA // cb)),
                  pl.BlockSpec((ts, D_CONV), lambda bb, s: (bb * ns + s, ZB_CG // cb)),
                  pl.BlockSpec((HALO, D_CONV), lambda bb, s: (halo_row(bb, s), ZB_CA // cb)),
                  pl.BlockSpec((HALO, D_CONV), lambda bb, s: (halo_row(bb, s), ZB_CG // cb)),
                  pl.BlockSpec((CONV_WIDTH, D_CONV), lambda bb, s: (0, 0)),
                  pl.BlockSpec((1, D_CONV), lambda bb, s: (0, 0)),
                  pl.BlockSpec((1, D_CONV), lambda bb, s: (0, 0)),
                  pl.BlockSpec((1, D_CONV), lambda bb, s: (0, 0))],
        out_specs=pl.BlockSpec((ts, D_CONV), lambda bb, s: (bb * ns + s, 0)),
        out_shape=jax.ShapeDtypeStruct((m, D_CONV), BF16),
        scratch_shapes=[pltpu.VMEM((HALO + ts, D_CONV), F32),
                        pltpu.VMEM((SUBLANES - 1, HALO + ts, D_CONV), F32)],
        compiler_params=_cparams(("parallel", "parallel")),
        name="conformer_conv",
    )(z, z, z, z, w, b, lg, lb)


def _outproj_kernel(x_ref, mod_ref, fox_ref, oc_ref, os_ref, ow_ref, cv_ref, w_ref, o_ref):
    d_fox = N_FOX * HEAD_DIM
    d_nsa = N_NSA * HEAD_DIM
    nsa = (oc_ref[...] + os_ref[...] + ow_ref[...]).astype(BF16)
    acc = _dot(fox_ref[...], w_ref[0:d_fox, :])
    acc = acc + _dot(nsa, w_ref[d_fox:d_fox + d_nsa, :])
    acc = acc + _dot(cv_ref[...], w_ref[d_fox + d_nsa:, :])
    o_ref[...] = x_ref[...] + mod_ref[2:3, :] * acc


def _outproj(x2, mod_l, fox, oc, osel, owin, conv, w, layer, seq):
    m, d = x2.shape
    tm = TM_OUT
    row = lambda i: (i, 0)
    return pl.pallas_call(
        _outproj_kernel,
        grid=(m // tm,),
        in_specs=[pl.BlockSpec((tm, d), row),
                  pl.BlockSpec((None, 6, d), lambda i: (i * tm // seq, 0, 0)),
                  pl.BlockSpec((tm, N_FOX * HEAD_DIM), row),
                  pl.BlockSpec((tm, N_NSA * HEAD_DIM), row),
                  pl.BlockSpec((tm, N_NSA * HEAD_DIM), row),
                  pl.BlockSpec((tm, N_NSA * HEAD_DIM), row),
                  pl.BlockSpec((tm, D_CONV), row),
                  pl.BlockSpec((None, d, d), lambda i: (layer, 0, 0))],
        out_specs=pl.BlockSpec((tm, d), row),
        out_shape=jax.ShapeDtypeStruct((m, d), F32),
        compiler_params=_cparams(("parallel",)),
        name="outproj",
    )(x2, mod_l, fox, oc, osel, owin, conv, w)


def _mlp_kernel(x_ref, mod_ref, g_ref, w1_ref, w2_ref, fg_ref, o_ref, h_ref, *, tm, final):
    f = pl.program_id(1)

    @pl.when(f == 0)
    def _():
        def chunk(c, carry):
            r = pl.multiple_of(c * ROW_CHUNK, ROW_CHUNK)
            h = _norm_mod(x_ref[pl.ds(r, ROW_CHUNK), :], g_ref[...], mod_ref[4:5, :], mod_ref[3:4, :])
            h_ref[pl.ds(r, ROW_CHUNK), :] = h.astype(BF16)
            return carry
        lax.fori_loop(0, tm // ROW_CHUNK, chunk, 0)
        o_ref[...] = jnp.zeros_like(o_ref)

    a = jnp.maximum(_dot(h_ref[...], w1_ref[...]), 0.0)
    o_ref[...] += _dot((a * a).astype(BF16), w2_ref[...])

    @pl.when(f == pl.num_programs(1) - 1)
    def _():
        def chunk(c, carry):
            r = pl.multiple_of(c * ROW_CHUNK, ROW_CHUNK)
            y = x_ref[pl.ds(r, ROW_CHUNK), :] + mod_ref[5:6, :] * o_ref[pl.ds(r, ROW_CHUNK), :]
            if final:
                y = y * lax.rsqrt(jnp.mean(y * y, axis=-1, keepdims=True) + EPS) * fg_ref[...]
            o_ref[pl.ds(r, ROW_CHUNK), :] = y
            return carry
        lax.fori_loop(0, tm // ROW_CHUNK, chunk, 0)


def _mlp(x2, mod_l, g, w1, w2, fg, layer, seq, final):
    m, d = x2.shape
    tm, tf = TM_MLP, TF_MLP
    dff = w1.shape[2]
    return pl.pallas_call(
        functools.partial(_mlp_kernel, tm=tm, final=final),
        grid=(m // tm, dff // tf),
        in_specs=[pl.BlockSpec((tm, d), lambda i, f: (i, 0), pipeline_mode=pl.Buffered(1)),
                  pl.BlockSpec((None, 6, d), lambda i, f: (i * tm // seq, 0, 0)),
                  pl.BlockSpec((1, d), lambda i, f: (0, 0)),
                  pl.BlockSpec((None, d, tf), lambda i, f: (layer, 0, f)),
                  pl.BlockSpec((None, tf, d), lambda i, f: (layer, f, 0)),
                  pl.BlockSpec((1, d), lambda i, f: (0, 0))],
        out_specs=pl.BlockSpec((tm, d), lambda i, f: (i, 0)),
        out_shape=jax.ShapeDtypeStruct((m, d), F32),
        scratch_shapes=[pltpu.VMEM((tm, d), BF16)],
        compiler_params=_cparams(("parallel", "arbitrary")),
        name="mlp_final" if final else "mlp",
    )(x2, mod_l, g, w1, w2, fg)


def _prep_inproj_weights(w_in):
    depth, d, _ = w_in.shape
    def sl(name, width, off=0, scale=None):
        piece = w_in[:, :, _SRC[name] + off:_SRC[name] + off + width]
        return (piece if scale is None else piece * scale).astype(BF16)
    qs = HEAD_DIM ** -0.5 * LOG2E
    pieces = [sl("fq", 768, scale=qs), sl("fk", 768), sl("fv", 768), sl("nq", 768, scale=qs),
              sl("ca", D_CONV), sl("cg", D_CONV)]
    for kname, vname in (("kc", "vc"), ("ks", "vs"), ("kw", "vw")):
        for g in range(N_KV):
            pieces += [sl(kname, HEAD_DIM, g * HEAD_DIM), sl(vname, HEAD_DIM, g * HEAD_DIM)]
    pieces.append(jnp.zeros((depth, d, LANES), BF16))
    wz = jnp.concatenate(pieces, axis=2)
    assert wz.shape[2] == NZ
    ws = jnp.concatenate([sl("ff", N_FOX), sl("ng", 3 * N_NSA),
                          jnp.zeros((depth, d, LANES - N_FOX - 3 * N_NSA), BF16)], axis=2)
    return wz, ws


def _prep_compress_weights(w_cmp_k, w_cmp_v, pos_cmp):
    depth = w_cmp_k.shape[0]
    slots = 2
    wkv = jnp.stack([w_cmp_k, w_cmp_v], axis=1)
    big = jnp.einsum("zklde,kq->zlkdqe", wkv, jnp.eye(2, dtype=F32))
    big = big.reshape(depth, CMP_LEN, slots * HEAD_DIM, slots * HEAD_DIM)
    half = CMP_LEN // 2
    kdim = half * slots * HEAD_DIM
    wlo = big[:, :half].reshape(depth, kdim, slots * HEAD_DIM).astype(BF16)
    whi = big[:, half:].reshape(depth, kdim, slots * HEAD_DIM).astype(BF16)
    pos = jnp.broadcast_to(pos_cmp[:, :, None, :], (depth, CMP_LEN, slots, HEAD_DIM))
    plo = jnp.zeros((depth, 8, kdim), F32).at[:, 0].set(pos[:, :half].reshape(depth, kdim)).astype(BF16)
    phi = jnp.zeros((depth, 8, kdim), F32).at[:, 0].set(pos[:, half:].reshape(depth, kdim)).astype(BF16)
    return wlo, whi, plo, phi


def _selection_constants(seq):
    nc = seq // CMP_STRIDE
    n_cmp = (seq - CMP_LEN) // CMP_STRIDE + 1
    n = np.arange(nc)[:, None]
    j = np.arange(LANES)[None, :]
    overlap = ((n * CMP_STRIDE < j * SEL_LEN + SEL_LEN) & (n * CMP_STRIDE + CMP_LEN - 1 >= j * SEL_LEN)
               & (n < n_cmp) & (j < seq // SEL_LEN))
    emat = (np.arange(seq)[:, None] // SEL_LEN + HALF) == np.arange(LANES)[None, :]
    return jnp.asarray(overlap, BF16), jnp.asarray(emat, BF16)


def kernel(x, c, w_mod, b_mod, norm1_g, w_in, b_f, w_cmp_k, w_cmp_v, pos_cmp, conv_w, conv_b, conv_ln_g,
           conv_ln_b, w_out, norm2_g, w_mlp1, w_mlp2, rel_bias, final_g):
    batch, seq, d = x.shape
    depth = w_in.shape[0]
    assert d == D_MODEL and seq % TM_INPROJ == 0 and seq // SEL_LEN <= SEL_LEN

    wz, ws = _prep_inproj_weights(w_in)
    wlo, whi, plo, phi = _prep_compress_weights(w_cmp_k, w_cmp_v, pos_cmp)
    w_out_b = w_out.astype(BF16)
    w1_b = w_mlp1.astype(BF16)
    w2_b = w_mlp2.astype(BF16)
    overlap, emat = _selection_constants(seq)
    bf_rows = jnp.zeros((depth, 1, LANES), F32).at[:, 0, SMALL_FF:SMALL_FF + N_FOX].set(b_f)

    mod = _modulation(c, w_mod, b_mod)
    fvec = _bias_by_distance(rel_bias)
    tiles = _bias_tiles(fvec)
    fg = final_g.reshape(1, d)

    x2 = x.reshape(batch * seq, d)
    kv0 = ZB_KVC * LANES
    for l in range(depth):
        z, small = _inproj(x2, mod[l], norm1_g[l].reshape(1, d), wz, ws, l, seq)
        fcol, frow = _fox_cumsum(small, bf_rows[l], batch, seq)
        o_fox = _fox_attention(z, fcol, frow, batch, seq)
        r = z[:, kv0:kv0 + N_KV * LANES].reshape(batch, seq // CMP_STRIDE, CMP_STRIDE, N_KV, LANES)
        r = r.transpose(0, 3, 1, 2, 4).reshape(batch, N_KV, seq // CMP_STRIDE, CMP_STRIDE * LANES)
        kvcmp = _compress(r, wlo[l], whi[l], plo[l], phi[l])
        o_cmp, sel, gsig = _nsa_compressed(z, kvcmp, small, fvec, overlap, batch, seq)
        o_sel = _nsa_flash(z, sel, emat, gsig, tiles, batch, seq, windowed=False)
        o_win = _nsa_flash(z, sel, emat, gsig, tiles, batch, seq, windowed=True)
        o_conv = _conformer_conv(z, conv_w[l], conv_b[l].reshape(1, -1), conv_ln_g[l].reshape(1, -1),
                                 conv_ln_b[l].reshape(1, -1), batch, seq)
        x2 = _outproj(x2, mod[l], o_fox, o_cmp, o_sel, o_win, o_conv, w_out_b, l, seq)
        x2 = _mlp(x2, mod[l], norm2_g[l].reshape(1, d), w1_b, w2_b, fg, l, seq, final=(l == depth - 1))
    return x2.reshape(batch, seq, d)
```

```python
import functools
import math

import numpy as np
import jax
import jax.numpy as jnp
from jax import lax
from jax.experimental import pallas as pl
from jax.experimental.pallas import tpu as pltpu

F32 = jnp.float32
BF16 = jnp.bfloat16
HIGHEST = lax.Precision.HIGHEST

D_MODEL = 2048
HEAD_DIM = 64
N_FOX = 12
N_NSA = 12
N_KV = 3
GROUP = 4
D_CONV = 512
D_FF = 4 * D_MODEL
CMP_LEN = 32
CMP_STRIDE = 16
SEL_LEN = 64
N_SEL = 16
WINDOW = 512
CONV_WIDTH = 31
N_BUCKETS = 32
MAX_DISTANCE = 128
EPS = 1e-6
NEG = -1e30
BIG = 1e9
LOG2E = math.log2(math.e)

LANES = 128
SUBLANES = 8
HALF = LANES // 2

ZB_FQ, ZB_FK, ZB_FV, ZB_NQ = 0, 6, 12, 18
ZB_CA, ZB_CG = 24, 28
ZB_KVC, ZB_KVS, ZB_KVW = 32, 35, 38
NZ_BLOCKS = 42
NZ = NZ_BLOCKS * LANES
SMALL_FF, SMALL_NG = 0, 12

_SRC = dict(fq=0, fk=768, fv=1536, ff=2304, nq=2316, kc=3084, vc=3276, ks=3468, vs=3660,
            kw=3852, vw=4044, ng=4236, ca=4272, cg=4784)

TM_INPROJ = 1024
TN_INPROJ = 1792
TM_OUT = 512
TM_MLP = 1024
TF_MLP = 1024
ROW_CHUNK = 128
T_FOX = 512
NSUB_FOX = 4
T_NSA = 256
NSUB_NSA = 4
TQ_CMP = 512
TS_CONV = 512
CONV_ROWS = 64
TS_CUM = 512
HALO = 32
VMEM_LIMIT = 56 * 1024 * 1024


def _cparams(sem):
    return pltpu.CompilerParams(dimension_semantics=sem, vmem_limit_bytes=VMEM_LIMIT)


def _dot(a, b):
    return jnp.dot(a, b, preferred_element_type=F32)


def _dot_nt(a, b):
    return lax.dot_general(a, b, (((1,), (1,)), ((), ())), preferred_element_type=F32)


def _bucket_thresholds():
    d = np.arange(MAX_DISTANCE, dtype=np.int32)
    max_exact = N_BUCKETS // 2
    nf = np.maximum(d, 1).astype(np.float32)
    large = max_exact + (np.log(nf / np.float32(max_exact)) / np.float32(math.log(MAX_DISTANCE / max_exact))
                         * np.float32(N_BUCKETS - max_exact)).astype(np.int32)
    large = np.minimum(large, N_BUCKETS - 1)
    bucket = np.where(d < max_exact, d, large)
    assert bucket[-1] == N_BUCKETS - 1 and np.all(np.diff(bucket) >= 0)
    return [int(np.argmax(bucket >= k)) for k in range(N_BUCKETS)]


_T5_THRESH = _bucket_thresholds()


def _fvec_kernel(tbl_ref, out_ref):
    d = lax.broadcasted_iota(jnp.int32, out_ref.shape, 1)
    out = jnp.broadcast_to(tbl_ref[:, 0:1], out_ref.shape)
    for k in range(1, N_BUCKETS):
        out = jnp.where(d >= _T5_THRESH[k], tbl_ref[:, k:k + 1], out)
    out_ref[...] = out * LOG2E


def _bias_by_distance(rel_bias):
    tbl = jnp.zeros((16, LANES), F32).at[:N_NSA, :N_BUCKETS].set(rel_bias.astype(F32).T)
    return pl.pallas_call(
        _fvec_kernel, out_shape=jax.ShapeDtypeStruct((16, LANES), F32), name="t5_bias_by_distance",
    )(tbl)


def _mod_kernel(ct_ref, w_ref, b_ref, o_ref, *, nb):
    c = ct_ref[...]
    ca = c * jax.nn.sigmoid(c)
    w = w_ref[...]
    rows = [jnp.sum(w * ca[:, b:b + 1], axis=0, keepdims=True) for b in range(nb)]
    rows.append(jnp.zeros((SUBLANES - nb, w.shape[1]), F32))
    o_ref[...] = jnp.concatenate(rows, axis=0) + b_ref[...]


def _modulation(c, w_mod, b_mod):
    depth, d, n = w_mod.shape
    b = c.shape[0]
    assert b < SUBLANES
    tn = 1024
    ct = jnp.zeros((d, LANES), F32).at[:, :b].set(c.T)
    out = pl.pallas_call(
        functools.partial(_mod_kernel, nb=b),
        grid=(depth, n // tn),
        in_specs=[pl.BlockSpec((d, LANES), lambda l, j: (0, 0)),
                  pl.BlockSpec((None, d, tn), lambda l, j: (l, 0, j)),
                  pl.BlockSpec((None, 1, tn), lambda l, j: (l, 0, j))],
        out_specs=pl.BlockSpec((None, SUBLANES, tn), lambda l, j: (l, 0, j)),
        out_shape=jax.ShapeDtypeStruct((depth, SUBLANES, n), F32),
        compiler_params=_cparams(("parallel", "parallel")),
        name="adaln_modulation",
    )(ct, w_mod, b_mod.reshape(depth, 1, n))
    return out[:, :b].reshape(depth, b, 6, d)


def _norm_mod(x, g, sc, sh):
    y = x * lax.rsqrt(jnp.mean(x * x, axis=-1, keepdims=True) + EPS) * g
    return y * (1.0 + sc) + sh


def _inproj_kernel(x_ref, mod_ref, g_ref, w_ref, ws_ref, z_ref, sm_ref, h_ref, *, tm):
    @pl.when(pl.program_id(1) == 0)
    def _():
        def chunk(c, carry):
            r = pl.multiple_of(c * ROW_CHUNK, ROW_CHUNK)
            h = _norm_mod(x_ref[pl.ds(r, ROW_CHUNK), :], g_ref[...], mod_ref[1:2, :], mod_ref[0:1, :])
            hb = h.astype(BF16)
            h_ref[pl.ds(r, ROW_CHUNK), :] = hb
            sm_ref[pl.ds(r, ROW_CHUNK), :] = _dot(hb, ws_ref[...])
            return carry
        lax.fori_loop(0, tm // ROW_CHUNK, chunk, 0)

    z_ref[...] = _dot(h_ref[...], w_ref[...]).astype(BF16)


def _inproj(x2, mod_l, g, wz, ws, layer, seq):
    m, d = x2.shape
    tm, tn = TM_INPROJ, TN_INPROJ
    assert seq % tm == 0 and NZ % tn == 0
    return pl.pallas_call(
        functools.partial(_inproj_kernel, tm=tm),
        grid=(m // tm, NZ // tn),
        in_specs=[pl.BlockSpec((tm, d), lambda i, j: (i, 0)),
                  pl.BlockSpec((None, 6, d), lambda i, j: (i * tm // seq, 0, 0)),
                  pl.BlockSpec((1, d), lambda i, j: (0, 0)),
                  pl.BlockSpec((None, d, tn), lambda i, j: (layer, 0, j)),
                  pl.BlockSpec((None, d, LANES), lambda i, j: (layer, 0, 0))],
        out_specs=[pl.BlockSpec((tm, tn), lambda i, j: (i, j)),
                   pl.BlockSpec((tm, LANES), lambda i, j: (i, 0))],
        out_shape=[jax.ShapeDtypeStruct((m, NZ), BF16), jax.ShapeDtypeStruct((m, LANES), F32)],
        scratch_shapes=[pltpu.VMEM((tm, d), BF16)],
        compiler_params=_cparams(("parallel", "arbitrary")),
        name="inproj",
    )(x2, mod_l, g, wz, ws)


def _foxcum_kernel(sm_ref, bf_ref, fcol_ref, frow_ref, carry_ref, *, ts):
    @pl.when(pl.program_id(1) == 0)
    def _():
        carry_ref[...] = jnp.zeros_like(carry_ref)

    x = sm_ref[...] + bf_ref[...]
    ls = (jnp.minimum(x, 0.0) - jnp.log1p(jnp.exp(-jnp.abs(x)))) * LOG2E
    r = lax.broadcasted_iota(jnp.int32, (ts, ts), 0)
    c = lax.broadcasted_iota(jnp.int32, (ts, ts), 1)
    tri = jnp.where(r >= c, 1.0, 0.0).astype(F32)
    cum = jnp.dot(tri, ls, precision=HIGHEST, preferred_element_type=F32) + carry_ref[0:1, :]
    carry_ref[0:1, :] = cum[ts - 1:ts, :]
    cum_t = cum.T
    for hp in range(N_FOX // 2):
        fcol_ref[hp] = cum if hp == 0 else pltpu.roll(cum, LANES - 2 * hp, 1)
        frow_ref[hp, 0:2, :] = cum_t[2 * hp:2 * hp + 2, :]
        frow_ref[hp, 2:8, :] = jnp.zeros((6, ts), F32)


def _fox_cumsum(small, bf_row, batch, seq):
    ts = TS_CUM
    ns = seq // ts
    npair = N_FOX // 2
    return pl.pallas_call(
        functools.partial(_foxcum_kernel, ts=ts),
        grid=(batch, ns),
        in_specs=[pl.BlockSpec((ts, LANES), lambda b, s: (b * ns + s, 0)),
                  pl.BlockSpec((1, LANES), lambda b, s: (0, 0))],
        out_specs=[pl.BlockSpec((None, npair, ts, LANES), lambda b, s: (b, 0, s, 0)),
                   pl.BlockSpec((None, npair, 8, ts), lambda b, s: (b, 0, 0, s))],
        out_shape=[jax.ShapeDtypeStruct((batch, npair, seq, LANES), F32),
                   jax.ShapeDtypeStruct((batch, npair, 8, seq), F32)],
        scratch_shapes=[pltpu.VMEM((8, LANES), F32)],
        compiler_params=_cparams(("parallel", "arbitrary")),
        name="fox_cumsum",
    )(small, bf_row)


def _fox_kernel(q_ref, k_ref, v_ref, fc_ref, fr_ref, o_ref, m_ref, fq_ref, acc_ref, *, t, nsub):
    qb = pl.program_id(2)
    kb = pl.program_id(3)
    tb = nsub * t

    @pl.when(kb == 0)
    def _():
        m_ref[...] = jnp.full(m_ref.shape, NEG, F32)
        acc_ref[...] = jnp.zeros_like(acc_ref)
        for h in range(2):
            fq_ref[h] = jnp.broadcast_to(fc_ref[:, h:h + 1], (tb, LANES))

    lane = lax.broadcasted_iota(jnp.int32, (1, LANES), 1)

    def sub_tile(i, j, diagonal):
        rows = slice(i * t, (i + 1) * t)
        cols = slice(j * t, (j + 1) * t)
        q2 = q_ref[rows, :]
        k2 = k_ref[cols, :]
        v2 = v_ref[cols, :]
        if diagonal:
            causal = (lax.broadcasted_iota(jnp.int32, (t, t), 1) <= lax.broadcasted_iota(jnp.int32, (t, t), 0))
        zero = jnp.zeros_like(q2)
        s2 = _dot_nt(jnp.concatenate([jnp.where(lane < HALF, q2, zero), jnp.where(lane >= HALF, q2, zero)], axis=0), k2)
        for h in range(2):
            keep = (lane < HALF) if h == 0 else (lane >= HALF)
            vh = jnp.where(keep, v2, jnp.ones_like(v2))
            s = s2[h * t:(h + 1) * t, :] - fr_ref[h:h + 1, cols]
            if diagonal:
                s = jnp.where(causal, s, NEG)
            fq = fq_ref[h, rows, :]
            m_old = m_ref[h, rows, :]
            m_new = jnp.maximum(m_old, jnp.max(s, axis=1, keepdims=True) + fq)
            shift = jnp.tile(fq - m_new, (1, t // LANES))
            p = jnp.exp2(s + shift)
            acc_ref[h, rows, :] = jnp.exp2(m_old - m_new) * acc_ref[h, rows, :] + _dot(p.astype(BF16), vh)
            m_ref[h, rows, :] = m_new

    @pl.when(kb < qb)
    def _():
        for i in range(nsub):
            for j in range(nsub):
                sub_tile(i, j, False)

    @pl.when(kb == qb)
    def _():
        for i in range(nsub):
            for j in range(i + 1):
                sub_tile(i, j, i == j)
        a0 = acc_ref[0]
        a1 = acc_ref[1]
        o_ref[...] = jnp.where(lane < HALF, a0 / pltpu.roll(a0, HALF, 1),
                               a1 / pltpu.roll(a1, HALF, 1)).astype(BF16)


def _fox_attention(z, fcol, frow, batch, seq):
    t, nsub = T_FOX, NSUB_FOX
    tb = t * nsub
    nb = seq // tb
    npair = N_FOX // 2
    m = batch * seq
    kv_row = lambda b, hp, qb, kb: b * nb + jnp.minimum(kb, qb)
    return pl.pallas_call(
        functools.partial(_fox_kernel, t=t, nsub=nsub),
        grid=(batch, npair, nb, nb),
        in_specs=[pl.BlockSpec((tb, LANES), lambda b, hp, qb, kb: (b * nb + qb, ZB_FQ + hp)),
                  pl.BlockSpec((tb, LANES), lambda b, hp, qb, kb: (kv_row(b, hp, qb, kb), ZB_FK + hp)),
                  pl.BlockSpec((tb, LANES), lambda b, hp, qb, kb: (kv_row(b, hp, qb, kb), ZB_FV + hp)),
                  pl.BlockSpec((None, None, tb, LANES), lambda b, hp, qb, kb: (b, hp, qb, 0)),
                  pl.BlockSpec((None, None, 8, tb), lambda b, hp, qb, kb: (b, hp, 0, jnp.minimum(kb, qb)))],
        out_specs=pl.BlockSpec((tb, LANES), lambda b, hp, qb, kb: (b * nb + qb, hp)),
        out_shape=jax.ShapeDtypeStruct((m, N_FOX * HEAD_DIM), BF16),
        scratch_shapes=[pltpu.VMEM((2, tb, LANES), F32), pltpu.VMEM((2, tb, LANES), F32),
                        pltpu.VMEM((2, tb, LANES), F32)],
        compiler_params=_cparams(("parallel", "parallel", "parallel", "arbitrary")),
        name="fox_attention",
    )(z, z, z, fcol, frow)


def _compress_kernel(r_ref, wlo_ref, whi_ref, plo_ref, phi_ref, o_ref, *, nc):
    bias = _dot(plo_ref[...], wlo_ref[...]) + _dot(phi_ref[...], whi_ref[...])
    for g in range(N_KV):
        r = r_ref[g]
        out = _dot(r, wlo_ref[...]) + pltpu.roll(_dot(r, whi_ref[...]), nc - 1, 0) + bias[0:1, :]
        o_ref[:, g * LANES:(g + 1) * LANES] = out.astype(BF16)


def _compress(r, wlo, whi, plo, phi):
    batch, ng, nc, k = r.shape
    return pl.pallas_call(
        functools.partial(_compress_kernel, nc=nc),
        grid=(batch,),
        in_specs=[pl.BlockSpec((None, ng, nc, k), lambda b: (b, 0, 0, 0)),
                  pl.BlockSpec((k, LANES), lambda b: (0, 0)),
                  pl.BlockSpec((k, LANES), lambda b: (0, 0)),
                  pl.BlockSpec((8, k), lambda b: (0, 0)),
                  pl.BlockSpec((8, k), lambda b: (0, 0))],
        out_specs=pl.BlockSpec((None, nc, ng * LANES), lambda b: (b, 0, 0)),
        out_shape=jax.ShapeDtypeStruct((batch, nc, ng * LANES), BF16),
        compiler_params=_cparams(("parallel",)),
        name="nsa_compress",
    )(r, wlo, whi, plo, phi)


def _head_queries(qf, lane):
    zero = jnp.zeros_like(qf)
    qa = jnp.where(lane < HALF, qf, zero).astype(BF16)
    qb = jnp.where(lane < HALF, pltpu.roll(qf, HALF, 1), zero).astype(BF16)
    return qa, qb


def _gather_bias(frow, idx, tq):
    tab = jnp.broadcast_to(frow, (tq, LANES))
    parts = [jnp.take_along_axis(tab, idx[:, c * LANES:(c + 1) * LANES], axis=1)
             for c in range(idx.shape[1] // LANES)]
    return parts[0] if len(parts) == 1 else jnp.concatenate(parts, axis=1)


def _nsacmp_kernel(q_ref, kvc_ref, sm_ref, fv_ref, ov_ref, oc_ref, sel_ref, gs_ref, val_ref, *, tq, nc):
    qi = pl.program_id(1)
    t0 = qi * tq
    lane = lax.broadcasted_iota(jnp.int32, (1, LANES), 1)
    t_col = t0 + lax.broadcasted_iota(jnp.int32, (tq, 1), 0)
    any_valid = t_col >= CMP_LEN - 1

    gs = jax.nn.sigmoid(sm_ref[...])
    n_sel = SEL_LEN
    j_row = lax.broadcasted_iota(jnp.int32, (n_sel, tq), 0)
    sub_iota = lax.broadcasted_iota(jnp.int32, (SUBLANES, tq), 0)
    t_row = t0 + lax.broadcasted_iota(jnp.int32, (n_sel, tq), 1)
    cur = jnp.right_shift(t_row, SEL_LEN.bit_length() - 1)
    forced = (j_row == 0) | (j_row == cur) | (j_row == cur - 1)
    valid_s = j_row * SEL_LEN <= t_row

    def attend(ncols):
        cmp_end = lax.broadcasted_iota(jnp.int32, (1, ncols), 1) * CMP_STRIDE + (CMP_LEN - 1)
        dist = t_col - cmp_end
        masked = jnp.where(dist >= 0, 0.0, NEG)
        idx = jnp.clip(dist, 0, MAX_DISTANCE - 1)
        for g in range(N_KV):
            kv = kvc_ref[0:ncols, g * LANES:(g + 1) * LANES]
            gs_g = pltpu.roll(gs, LANES - (SMALL_NG + g * GROUP * 3), 1)
            gs_ref[:, g * LANES:(g + 1) * LANES] = gs_g
            psum = jnp.zeros((tq, ncols), F32)
            outs = []
            for pair in range(2):
                c0 = (g * 2 + pair) * LANES
                qpair = _head_queries(q_ref[:, c0:c0 + LANES].astype(F32), lane)
                for sub in range(2):
                    r = pair * 2 + sub
                    h = g * GROUP + r
                    s = _dot_nt(qpair[sub], kv) + _gather_bias(fv_ref[h:h + 1, :], idx, tq) + masked
                    e = jnp.exp2(s - jnp.max(s, axis=1, keepdims=True))
                    p = e * jnp.where(any_valid, 1.0 / jnp.sum(e, axis=1, keepdims=True), 0.0)
                    psum = psum + p
                    o = _dot(p.astype(BF16), kv)
                    outs.append(o * gs_g[:, r * 3:r * 3 + 1])
                oc_ref[:, c0:c0 + LANES] = jnp.where(lane < HALF, pltpu.roll(outs[-2], HALF, 1),
                                                     outs[-1]).astype(BF16)

            p_hi = psum.astype(BF16)
            p_lo = (psum - p_hi.astype(F32)).astype(BF16)
            imp = _dot(p_hi, ov_ref[0:ncols, :]) + _dot(p_lo, ov_ref[0:ncols, :])
            val_ref[g] = jnp.where(forced, BIG, jnp.where(valid_s, imp.T[0:n_sel, :], NEG))

    n_low = (CMP_STRIDE * LANES + CMP_LEN - 1) // tq
    if nc > LANES and n_low > 0:
        @pl.when(qi < n_low)
        def _():
            attend(LANES)

        @pl.when(qi >= n_low)
        def _():
            attend(nc)
    else:
        attend(nc)

    for g in range(N_KV):
        groups = [val_ref[g, SUBLANES * b:SUBLANES * (b + 1), :] for b in range(n_sel // SUBLANES)]
        counts = [jnp.zeros((SUBLANES, tq), F32) for _ in groups]
        for j in range(n_sel):
            jb, js = divmod(j, SUBLANES)
            row = groups[jb][js:js + 1, :]
            for b, grp in enumerate(groups):
                if b < jb:
                    ahead = row > grp
                elif b > jb:
                    ahead = row >= grp
                else:
                    ahead = (row > grp) | ((row == grp) & (js < sub_iota))
                counts[b] = counts[b] + jnp.where(ahead, 1.0, 0.0)
        cnt = jnp.concatenate(counts, axis=0)
        unchosen = jnp.where(cnt < float(N_SEL), 0.0, NEG)
        unchosen = jnp.concatenate([jnp.zeros((LANES - n_sel, tq), F32), unchosen], axis=0)
        sel_ref[:, g * LANES:(g + 1) * LANES] = unchosen.T.astype(BF16)


def _nsa_compressed(z, kvcmp, small, fvec, overlap, batch, seq):
    tq = TQ_CMP
    nq = seq // tq
    nc = kvcmp.shape[1]
    m = batch * seq
    w3 = N_KV * LANES
    return pl.pallas_call(
        functools.partial(_nsacmp_kernel, tq=tq, nc=nc),
        grid=(batch, nq),
        in_specs=[pl.BlockSpec((tq, N_NSA * HEAD_DIM), lambda b, i: (b * nq + i, ZB_NQ * LANES // (N_NSA * HEAD_DIM))),
                  pl.BlockSpec((None, nc, w3), lambda b, i: (b, 0, 0)),
                  pl.BlockSpec((tq, LANES), lambda b, i: (b * nq + i, 0)),
                  pl.BlockSpec((16, LANES), lambda b, i: (0, 0)),
                  pl.BlockSpec((nc, LANES), lambda b, i: (0, 0))],
        out_specs=[pl.BlockSpec((tq, N_NSA * HEAD_DIM), lambda b, i: (b * nq + i, 0)),
                   pl.BlockSpec((tq, w3), lambda b, i: (b * nq + i, 0)),
                   pl.BlockSpec((tq, w3), lambda b, i: (b * nq + i, 0))],
        out_shape=[jax.ShapeDtypeStruct((m, N_NSA * HEAD_DIM), BF16),
                   jax.ShapeDtypeStruct((m, w3), BF16),
                   jax.ShapeDtypeStruct((m, w3), F32)],
        scratch_shapes=[pltpu.VMEM((N_KV, SEL_LEN, tq), F32)],
        compiler_params=_cparams(("parallel", "parallel")),
        name="nsa_compressed",
    )(z, kvcmp, small, fvec, overlap)


def _nsaflash_kernel(*refs, t, nsub, nkk, windowed, gate_col):
    if windowed:
        qa_ref, qb_ref, kv_ref, gs_ref, tb_ref, o_ref, q4_ref, m_ref, acc_ref = refs
        sel_ref = e_ref = None
    else:
        qa_ref, qb_ref, kv_ref, sel_ref, e_ref, gs_ref, tb_ref, o_ref, q4_ref, m_ref, acc_ref = refs
    g = pl.program_id(1)
    qb = pl.program_id(2)
    kk = pl.program_id(3)
    far = tb_ref.shape[0] - 1
    lane = lax.broadcasted_iota(jnp.int32, (1, LANES), 1)

    @pl.when(kk == 0)
    def _():
        m_ref[...] = jnp.full(m_ref.shape, NEG, F32)
        acc_ref[...] = jnp.zeros_like(acc_ref)
        for i in range(nsub):
            rows = slice(i * t, (i + 1) * t)
            queries = (_head_queries(qa_ref[rows, :].astype(F32), lane)
                       + _head_queries(qb_ref[rows, :].astype(F32), lane))
            for r in range(GROUP):
                q = queries[r] if windowed else jnp.where(lane < HALF, queries[r], sel_ref[rows, :])
                q4_ref[(i * GROUP + r) * t:(i * GROUP + r + 1) * t, :] = q

    def sub_tile(i, j, rel):
        tidx = rel if (windowed or rel <= 1) else far
        kv = kv_ref[j * t:(j + 1) * t, :]
        ones_v = jnp.where(lane < HALF, jnp.ones_like(kv), kv)
        keys = kv if windowed else jnp.where(lane < HALF, kv, e_ref[j * t:(j + 1) * t, :])
        s4 = _dot_nt(q4_ref[i * GROUP * t:(i + 1) * GROUP * t, :], keys)
        for pair in range(2):
            base = (i * GROUP + 2 * pair) * t
            rows2 = slice(base, base + 2 * t)
            probs, alphas = [], []
            for sub in range(2):
                r = 2 * pair + sub
                h = g * GROUP + r
                rows = slice(base + sub * t, base + (sub + 1) * t)
                m_old = m_ref[rows, :]
                s = s4[r * t:(r + 1) * t, :] + tb_ref[tidx, h]
                m_new = jnp.maximum(m_old, jnp.max(s, axis=1, keepdims=True))
                p = jnp.exp2(s - jnp.tile(m_new, (1, t // LANES)))
                m_ref[rows, :] = m_new
                alphas.append(jnp.exp2(m_old - m_new))
                probs.append(p.astype(BF16))
            acc_ref[rows2, :] = (jnp.concatenate(alphas, axis=0) * acc_ref[rows2, :]
                                 + _dot(jnp.concatenate(probs, axis=0), ones_v))

    def block(offset):
        max_rel = far - 1 if windowed else None
        for j in range(nsub):
            for i in range(nsub):
                rel = offset * nsub + i - j
                if rel >= 0 and (max_rel is None or rel <= max_rel):
                    sub_tile(i, j, rel)

    if windowed:
        for c in range(nkk):
            @pl.when((kk == c) & (qb - (nkk - 1) + c >= 0))
            def _(c=c):
                block(nkk - 1 - c)
        last = kk == nkk - 1
    else:
        @pl.when(qb - kk == 0)
        def _():
            block(0)

        @pl.when(qb - kk == 1)
        def _():
            block(1)

        @pl.when(qb - kk >= 2)
        def _():
            block(2)
        last = kk == qb

    @pl.when(last)
    def _():
        for i in range(nsub):
            rows = slice(i * t, (i + 1) * t)
            gs = gs_ref[rows, :]
            o = []
            for r in range(GROUP):
                a = acc_ref[(i * GROUP + r) * t:(i * GROUP + r + 1) * t, :]
                o.append(a / pltpu.roll(a, HALF, 1) * gs[:, r * 3 + gate_col:r * 3 + gate_col + 1])
            o_ref[rows, 0:LANES] = jnp.where(lane < HALF, pltpu.roll(o[0], HALF, 1), o[1]).astype(BF16)
            o_ref[rows, LANES:2 * LANES] = jnp.where(lane < HALF, pltpu.roll(o[2], HALF, 1), o[3]).astype(BF16)


def _bias_tile_kernel(fv_ref, o_ref, *, t, nrel):
    rel = pl.program_id(0)
    h = pl.program_id(1)
    dist = rel * t + lax.broadcasted_iota(jnp.int32, (t, t), 0) - lax.broadcasted_iota(jnp.int32, (t, t), 1)
    bias = _gather_bias(fv_ref[pl.ds(h, 1), :], jnp.clip(dist, 0, MAX_DISTANCE - 1), t)
    o_ref[...] = jnp.where((dist >= 0) & ((dist < WINDOW) | (rel == nrel)), bias, NEG)


def _bias_tiles(fvec):
    t = T_NSA
    nrel = WINDOW // t + 1
    return pl.pallas_call(
        functools.partial(_bias_tile_kernel, t=t, nrel=nrel),
        grid=(nrel + 1, N_NSA),
        in_specs=[pl.BlockSpec((16, LANES), lambda r, h: (0, 0))],
        out_specs=pl.BlockSpec((None, None, t, t), lambda r, h: (r, h, 0, 0)),
        out_shape=jax.ShapeDtypeStruct((nrel + 1, N_NSA, t, t), F32),
        compiler_params=_cparams(("parallel", "parallel")),
        name="nsa_bias_tiles",
    )(fvec)


def _nsa_flash(z, sel, emat, gsig, tiles, batch, seq, windowed):
    t, nsub = T_NSA, NSUB_NSA
    tb = t * nsub
    nb = seq // tb
    m = batch * seq
    max_rel = WINDOW // t
    nkk = ((max_rel + nsub - 1) // nsub + 1) if windowed else nb
    assert t >= MAX_DISTANCE and 2 * t <= WINDOW and WINDOW % t == 0 and seq % tb == 0
    zb_kv = ZB_KVW if windowed else ZB_KVS
    if windowed:
        kidx = lambda qb, kk: jnp.maximum(qb - (nkk - 1) + kk, 0)
    else:
        kidx = lambda qb, kk: jnp.minimum(kk, qb)
    in_specs = [pl.BlockSpec((tb, LANES), lambda b, g, qb, kk: (b * nb + qb, ZB_NQ + 2 * g)),
                pl.BlockSpec((tb, LANES), lambda b, g, qb, kk: (b * nb + qb, ZB_NQ + 2 * g + 1)),
                pl.BlockSpec((tb, LANES), lambda b, g, qb, kk: (b * nb + kidx(qb, kk), zb_kv + g))]
    args = [z, z, z]
    if not windowed:
        in_specs += [pl.BlockSpec((tb, LANES), lambda b, g, qb, kk: (b * nb + qb, g)),
                     pl.BlockSpec((tb, LANES), lambda b, g, qb, kk: (kidx(qb, kk), 0))]
        args += [sel, emat]
    in_specs += [pl.BlockSpec((tb, LANES), lambda b, g, qb, kk: (b * nb + qb, g)),
                 pl.BlockSpec(tiles.shape, lambda b, g, qb, kk: (0, 0, 0, 0))]
    args += [gsig, tiles]
    return pl.pallas_call(
        functools.partial(_nsaflash_kernel, t=t, nsub=nsub, nkk=nkk, windowed=windowed,
                          gate_col=2 if windowed else 1),
        grid=(batch, N_KV, nb, nkk),
        in_specs=in_specs,
        out_specs=pl.BlockSpec((tb, 2 * LANES), lambda b, g, qb, kk: (b * nb + qb, g)),
        out_shape=jax.ShapeDtypeStruct((m, N_NSA * HEAD_DIM), BF16),
        scratch_shapes=[pltpu.VMEM((nsub * GROUP * t, LANES), BF16), pltpu.VMEM((nsub * GROUP * t, LANES), F32),
                        pltpu.VMEM((nsub * GROUP * t, LANES), F32)],
        compiler_params=_cparams(("parallel", "parallel", "parallel", "arbitrary")),
        name="nsa_window" if windowed else "nsa_selected",
    )(*args)


def _conv_kernel(a_ref, g_ref, ah_ref, gh_ref, w_ref, b_ref, lg_ref, lb_ref, o_ref, u_ref, us_ref, *, ts):
    first = pl.program_id(1) == 0
    halo = ah_ref[...].astype(F32) * jax.nn.sigmoid(gh_ref[...].astype(F32))
    u_ref[0:HALO, :] = jnp.where(first, 0.0, halo)
    u_ref[HALO:HALO + ts, :] = a_ref[...].astype(F32) * jax.nn.sigmoid(g_ref[...].astype(F32))
    n_shift = HALO + ts - SUBLANES
    for s in range(1, SUBLANES):
        us_ref[s - 1, 0:n_shift, :] = u_ref[s:s + n_shift, :]
    base = HALO - (CONV_WIDTH - 1)
    for c in range(ts // CONV_ROWS):
        r0 = c * CONV_ROWS
        y = jnp.zeros((CONV_ROWS, D_CONV), F32)
        for k in range(CONV_WIDTH):
            start, s = divmod(base + k + r0, SUBLANES)
            start *= SUBLANES
            window = (u_ref[start:start + CONV_ROWS, :] if s == 0
                      else us_ref[s - 1, start:start + CONV_ROWS, :])
            y = y + window * w_ref[k:k + 1, :]
        y = y + b_ref[...]
        mu = jnp.mean(y, axis=-1, keepdims=True)
        yc = y - mu
        var = jnp.mean(yc * yc, axis=-1, keepdims=True)
        yn = yc * lax.rsqrt(var + EPS) * lg_ref[...] + lb_ref[...]
        o_ref[r0:r0 + CONV_ROWS, :] = (yn * jax.nn.sigmoid(yn)).astype(BF16)


def _conformer_conv(z, w, b, lg, lb, batch, seq):
    ts = TS_CONV
    ns = seq // ts
    m = batch * seq
    cb = D_CONV // LANES
    hb = ts // HALO
    halo_row = lambda bb, s: jnp.maximum((bb * ns + s) * hb - 1, 0)
    return pl.pallas_call(
        functools.partial(_conv_kernel, ts=ts),
        grid=(batch, ns),
        in_specs=[pl.BlockSpec((ts, D_CONV), lambda bb, s: (bb * ns + s, ZB_CA // cb)),
                  pl.BlockSpec((ts, D_CONV), lambda bb, s: (bb * ns + s, ZB_CG // cb)),
                  pl.BlockSpec((HALO, D_CONV), lambda bb, s: (halo_row(bb, s), ZB_CA // cb)),
                  pl.BlockSpec((HALO, D_CONV), lambda bb, s: (halo_row(bb, s), ZB_CG // cb)),
                  pl.BlockSpec((CONV_WIDTH, D_CONV), lambda bb, s: (0, 0)),
                  pl.BlockSpec((1, D_CONV), lambda bb, s: (0, 0)),
                  pl.BlockSpec((1, D_CONV), lambda bb, s: (0, 0)),
                  pl.BlockSpec((1, D_CONV), lambda bb, s: (0, 0))],
        out_specs=pl.BlockSpec((ts, D_CONV), lambda bb, s: (bb * ns + s, 0)),
        out_shape=jax.ShapeDtypeStruct((m, D_CONV), BF16),
        scratch_shapes=[pltpu.VMEM((HALO + ts, D_CONV), F32),
                        pltpu.VMEM((SUBLANES - 1, HALO + ts, D_CONV), F32)],
        compiler_params=_cparams(("parallel", "parallel")),
        name="conformer_conv",
    )(z, z, z, z, w, b, lg, lb)


def _outproj_kernel(x_ref, mod_ref, fox_ref, oc_ref, os_ref, ow_ref, cv_ref, w_ref, o_ref):
    d_fox = N_FOX * HEAD_DIM
    d_nsa = N_NSA * HEAD_DIM
    nsa = (oc_ref[...].astype(F32) + os_ref[...].astype(F32) + ow_ref[...].astype(F32)).astype(BF16)
    acc = _dot(fox_ref[...], w_ref[0:d_fox, :])
    acc = acc + _dot(nsa, w_ref[d_fox:d_fox + d_nsa, :])
    acc = acc + _dot(cv_ref[...], w_ref[d_fox + d_nsa:, :])
    o_ref[...] = x_ref[...] + mod_ref[2:3, :] * acc


def _outproj(x2, mod_l, fox, oc, osel, owin, conv, w, layer, seq):
    m, d = x2.shape
    tm = TM_OUT
    row = lambda i: (i, 0)
    return pl.pallas_call(
        _outproj_kernel,
        grid=(m // tm,),
        in_specs=[pl.BlockSpec((tm, d), row),
                  pl.BlockSpec((None, 6, d), lambda i: (i * tm // seq, 0, 0)),
                  pl.BlockSpec((tm, N_FOX * HEAD_DIM), row),
                  pl.BlockSpec((tm, N_NSA * HEAD_DIM), row),
                  pl.BlockSpec((tm, N_NSA * HEAD_DIM), row),
                  pl.BlockSpec((tm, N_NSA * HEAD_DIM), row),
                  pl.BlockSpec((tm, D_CONV), row),
                  pl.BlockSpec((None, d, d), lambda i: (layer, 0, 0))],
        out_specs=pl.BlockSpec((tm, d), row),
        out_shape=jax.ShapeDtypeStruct((m, d), F32),
        compiler_params=_cparams(("parallel",)),
        name="outproj",
    )(x2, mod_l, fox, oc, osel, owin, conv, w)


def _mlp_kernel(x_ref, mod_ref, g_ref, w1_ref, w2_ref, fg_ref, o_ref, h_ref, *, tm, final):
    f = pl.program_id(1)

    @pl.when(f == 0)
    def _():
        def chunk(c, carry):
            r = pl.multiple_of(c * ROW_CHUNK, ROW_CHUNK)
            h = _norm_mod(x_ref[pl.ds(r, ROW_CHUNK), :], g_ref[...], mod_ref[4:5, :], mod_ref[3:4, :])
            h_ref[pl.ds(r, ROW_CHUNK), :] = h.astype(BF16)
            return carry
        lax.fori_loop(0, tm // ROW_CHUNK, chunk, 0)
        o_ref[...] = jnp.zeros_like(o_ref)

    a = jnp.maximum(_dot(h_ref[...], w1_ref[...]), 0.0)
    o_ref[...] += _dot((a * a).astype(BF16), w2_ref[...])

    @pl.when(f == pl.num_programs(1) - 1)
    def _():
        def chunk(c, carry):
            r = pl.multiple_of(c * ROW_CHUNK, ROW_CHUNK)
            y = x_ref[pl.ds(r, ROW_CHUNK), :] + mod_ref[5:6, :] * o_ref[pl.ds(r, ROW_CHUNK), :]
            if final:
                y = y * lax.rsqrt(jnp.mean(y * y, axis=-1, keepdims=True) + EPS) * fg_ref[...]
            o_ref[pl.ds(r, ROW_CHUNK), :] = y
            return carry
        lax.fori_loop(0, tm // ROW_CHUNK, chunk, 0)


def _mlp(x2, mod_l, g, w1, w2, fg, layer, seq, final):
    m, d = x2.shape
    tm, tf = TM_MLP, TF_MLP
    dff = w1.shape[2]
    return pl.pallas_call(
        functools.partial(_mlp_kernel, tm=tm, final=final),
        grid=(m // tm, dff // tf),
        in_specs=[pl.BlockSpec((tm, d), lambda i, f: (i, 0), pipeline_mode=pl.Buffered(1)),
                  pl.BlockSpec((None, 6, d), lambda i, f: (i * tm // seq, 0, 0)),
                  pl.BlockSpec((1, d), lambda i, f: (0, 0)),
                  pl.BlockSpec((None, d, tf), lambda i, f: (layer, 0, f)),
                  pl.BlockSpec((None, tf, d), lambda i, f: (layer, f, 0)),
                  pl.BlockSpec((1, d), lambda i, f: (0, 0))],
        out_specs=pl.BlockSpec((tm, d), lambda i, f: (i, 0)),
        out_shape=jax.ShapeDtypeStruct((m, d), F32),
        scratch_shapes=[pltpu.VMEM((tm, d), BF16)],
        compiler_params=_cparams(("parallel", "arbitrary")),
        name="mlp_final" if final else "mlp",
    )(x2, mod_l, g, w1, w2, fg)


def _prep_inproj_weights(w_in):
    depth, d, _ = w_in.shape
    def sl(name, width, off=0, scale=None):
        piece = w_in[:, :, _SRC[name] + off:_SRC[name] + off + width]
        return (piece if scale is None else piece * scale).astype(BF16)
    qs = HEAD_DIM ** -0.5 * LOG2E
    pieces = [sl("fq", 768, scale=qs), sl("fk", 768), sl("fv", 768), sl("nq", 768, scale=qs),
              sl("ca", D_CONV), sl("cg", D_CONV)]
    for kname, vname in (("kc", "vc"), ("ks", "vs"), ("kw", "vw")):
        for g in range(N_KV):
            pieces += [sl(kname, HEAD_DIM, g * HEAD_DIM), sl(vname, HEAD_DIM, g * HEAD_DIM)]
    pieces.append(jnp.zeros((depth, d, LANES), BF16))
    wz = jnp.concatenate(pieces, axis=2)
    assert wz.shape[2] == NZ
    ws = jnp.concatenate([sl("ff", N_FOX), sl("ng", 3 * N_NSA),
                          jnp.zeros((depth, d, LANES - N_FOX - 3 * N_NSA), BF16)], axis=2)
    return wz, ws


def _prep_compress_weights(w_cmp_k, w_cmp_v, pos_cmp):
    depth = w_cmp_k.shape[0]
    slots = 2
    wkv = jnp.stack([w_cmp_k, w_cmp_v], axis=1)
    big = jnp.einsum("zklde,kq->zlkdqe", wkv, jnp.eye(2, dtype=F32))
    big = big.reshape(depth, CMP_LEN, slots * HEAD_DIM, slots * HEAD_DIM)
    half = CMP_LEN // 2
    kdim = half * slots * HEAD_DIM
    wlo = big[:, :half].reshape(depth, kdim, slots * HEAD_DIM).astype(BF16)
    whi = big[:, half:].reshape(depth, kdim, slots * HEAD_DIM).astype(BF16)
    pos = jnp.broadcast_to(pos_cmp[:, :, None, :], (depth, CMP_LEN, slots, HEAD_DIM))
    plo = jnp.zeros((depth, 8, kdim), F32).at[:, 0].set(pos[:, :half].reshape(depth, kdim)).astype(BF16)
    phi = jnp.zeros((depth, 8, kdim), F32).at[:, 0].set(pos[:, half:].reshape(depth, kdim)).astype(BF16)
    return wlo, whi, plo, phi


def _selection_constants(seq):
    nc = seq // CMP_STRIDE
    n_cmp = (seq - CMP_LEN) // CMP_STRIDE + 1
    n = np.arange(nc)[:, None]
    j = np.arange(LANES)[None, :]
    overlap = ((n * CMP_STRIDE < j * SEL_LEN + SEL_LEN) & (n * CMP_STRIDE + CMP_LEN - 1 >= j * SEL_LEN)
               & (n < n_cmp) & (j < seq // SEL_LEN))
    emat = (np.arange(seq)[:, None] // SEL_LEN + HALF) == np.arange(LANES)[None, :]
    return jnp.asarray(overlap, BF16), jnp.asarray(emat, BF16)


def kernel(x, c, w_mod, b_mod, norm1_g, w_in, b_f, w_cmp_k, w_cmp_v, pos_cmp, conv_w, conv_b, conv_ln_g,
           conv_ln_b, w_out, norm2_g, w_mlp1, w_mlp2, rel_bias, final_g):
    batch, seq, d = x.shape
    depth = w_in.shape[0]
    assert d == D_MODEL and seq % TM_INPROJ == 0 and seq // SEL_LEN <= SEL_LEN

    wz, ws = _prep_inproj_weights(w_in)
    wlo, whi, plo, phi = _prep_compress_weights(w_cmp_k, w_cmp_v, pos_cmp)
    w_out_b = w_out.astype(BF16)
    w1_b = w_mlp1.astype(BF16)
    w2_b = w_mlp2.astype(BF16)
    overlap, emat = _selection_constants(seq)
    bf_rows = jnp.zeros((depth, 1, LANES), F32).at[:, 0, SMALL_FF:SMALL_FF + N_FOX].set(b_f)

    mod = _modulation(c, w_mod, b_mod)
    fvec = _bias_by_distance(rel_bias)
    tiles = _bias_tiles(fvec)
    fg = final_g.reshape(1, d)

    x2 = x.reshape(batch * seq, d)
    kv0 = ZB_KVC * LANES
    for l in range(depth):
        z, small = _inproj(x2, mod[l], norm1_g[l].reshape(1, d), wz, ws, l, seq)
        fcol, frow = _fox_cumsum(small, bf_rows[l], batch, seq)
        o_fox = _fox_attention(z, fcol, frow, batch, seq)
        r = z[:, kv0:kv0 + N_KV * LANES].reshape(batch, seq // CMP_STRIDE, CMP_STRIDE, N_KV, LANES)
        r = r.transpose(0, 3, 1, 2, 4).reshape(batch, N_KV, seq // CMP_STRIDE, CMP_STRIDE * LANES)
        kvcmp = _compress(r, wlo[l], whi[l], plo[l], phi[l])
        o_cmp, sel, gsig = _nsa_compressed(z, kvcmp, small, fvec, overlap, batch, seq)
        o_sel = _nsa_flash(z, sel, emat, gsig, tiles, batch, seq, windowed=False)
        o_win = _nsa_flash(z, sel, emat, gsig, tiles, batch, seq, windowed=True)
        o_conv = _conformer_conv(z, conv_w[l], conv_b[l].reshape(1, -1), conv_ln_g[l].reshape(1, -1),
                                 conv_ln_b[l].reshape(1, -1), batch, seq)
        x2 = _outproj(x2, mod[l], o_fox, o_cmp, o_sel, o_win, o_conv, w_out_b, l, seq)
        x2 = _mlp(x2, mod[l], norm2_g[l].reshape(1, d), w1_b, w2_b, fg, l, seq, final=(l == depth - 1))
    return x2.reshape(batch, seq, d)
```
